```python
import math
import jax, jax.numpy as jnp
from jax import lax
import numpy as np

D_MODEL = 1024
BATCH = 4
SEQ = 4096
DEPTH = 2

N_BRANCH = 4
MIX_W = D_MODEL // N_BRANCH
HEAD_DIM = 64
BLOCK = 128
A_HEADS = MIX_W // HEAD_DIM
A_KV = A_HEADS // 2
WINDOW = 128
B_HEADS = MIX_W // HEAD_DIM
B_KV = B_HEADS // 2
ROPE_THETA = 10000.0
GRID_W = 64
C_VDIM = HEAD_DIM
C_DIM = C_VDIM // 2
C_HEADS = MIX_W // C_VDIM
C_KV = C_HEADS // 2
D_HEADS = 4
D_KDIM = MIX_W // D_HEADS
D_VDIM = MIX_W // D_HEADS
CHUNK = 64
N_META = 16
FRONT = (-N_META) % BLOCK
META_END = FRONT + N_META
N_BUCKETS = 32
MAX_DIST = 128
D_FF = -(-(8 * D_MODEL) // (3 * 256)) * 256
RMS_EPS = 1e-6
MASK_VALUE = -1e30
F_FLOOR = 1e-30
IN_WIDTHS = (A_HEADS * HEAD_DIM, A_KV * HEAD_DIM, A_KV * HEAD_DIM,
             B_HEADS * HEAD_DIM, B_KV * HEAD_DIM, B_KV * HEAD_DIM,
             C_HEADS * 2 * C_DIM, C_KV * 2 * C_DIM, C_KV * C_VDIM,
             D_HEADS * D_KDIM, D_HEADS * D_KDIM, D_HEADS * D_KDIM, D_HEADS * D_VDIM, D_HEADS * D_VDIM,
             N_BRANCH * D_MODEL)
IN_TOTAL = sum(IN_WIDTHS)

kernel_name = 'hybrid_gated_parallel_encoder'


def rms_norm(x, g):
    x32 = x.astype(jnp.float32)
    y = x32 * lax.rsqrt(jnp.mean(x32 * x32, axis=-1, keepdims=True) + RMS_EPS)
    return (y * g.astype(jnp.float32)).astype(x.dtype)


def split_in(proj):
    idx, acc = [], 0
    for w in IN_WIDTHS[:-1]:
        acc += w
        idx.append(acc)
    return jnp.split(proj, idx, axis=-1)


def t5_bucket(rel):
    half = N_BUCKETS // 2
    exact = half // 2
    n = jnp.abs(rel)
    nf = jnp.maximum(n, exact).astype(jnp.float32)
    big = exact + (jnp.log(nf / exact) / math.log(MAX_DIST / exact) * (half - exact)).astype(jnp.int32)
    big = jnp.minimum(big, half - 1)
    return jnp.where(rel > 0, half, 0) + jnp.where(n < exact, n, big)


def windowed_sink_attention(q, k, v, sink, bias_tab):
    Bn, L = q.shape[:2]
    nb = L // BLOCK
    G = A_HEADS // A_KV
    qb = q.reshape(Bn, nb, BLOCK, A_KV, G, HEAD_DIM)

    def band(t):
        tp = jnp.pad(t, ((0, 0), (BLOCK, BLOCK), (0, 0), (0, 0)))
        tb = tp.reshape(Bn, nb + 2, BLOCK, A_KV, HEAD_DIM)
        return jnp.concatenate([tb[:, :-2], tb[:, 1:-1], tb[:, 2:]], axis=2)

    kb, vb = band(k), band(v)
    km, vm = k[:, FRONT:META_END], v[:, FRONT:META_END]
    qpos = jnp.arange(L).reshape(nb, BLOCK)
    kpos = qpos[:, :1] - BLOCK + jnp.arange(3 * BLOCK)[None, :]
    rel_b = kpos[:, None, :] - qpos[:, :, None]
    ok_b = (jnp.abs(rel_b) <= WINDOW) & (kpos >= META_END)[:, None, :] & (kpos < L)[:, None, :]
    rel_m = jnp.arange(FRONT, META_END)[None, None, :] - qpos[:, :, None]

    def head_bias(rel):
        b = bias_tab[t5_bucket(rel)]
        return jnp.moveaxis(b, -1, 0).reshape((A_KV, G) + rel.shape).astype(jnp.float32)

    scale = HEAD_DIM ** -0.5
    s_b = jnp.einsum('bnqhgd,bnkhd->bhgnqk', qb, kb).astype(jnp.float32) * scale + head_bias(rel_b)
    s_b = jnp.where(ok_b, s_b, MASK_VALUE)
    s_m = jnp.einsum('bnqhgd,bmhd->bhgnqm', qb, km).astype(jnp.float32) * scale + head_bias(rel_m)
    s_sink = jnp.broadcast_to(sink.astype(jnp.float32).reshape(A_KV, G, 1, 1, 1), s_m.shape[:-1] + (1,))
    p = jax.nn.softmax(jnp.concatenate([s_b, s_m, s_sink], axis=-1), axis=-1).astype(v.dtype)
    p_b = p[..., :3 * BLOCK]
    p_m = p[..., 3 * BLOCK:3 * BLOCK + N_META]
    o = jnp.einsum('bhgnqk,bnkhd->bnqhgd', p_b, vb) + jnp.einsum('bhgnqm,bmhd->bnqhgd', p_m, vm)
    return o.reshape(Bn, L, A_HEADS * HEAD_DIM)


def rotate_half_pairs(x, ang):
    c = jnp.cos(ang)[:, None, :]
    s = jnp.sin(ang)[:, None, :]
    x1, x2 = jnp.split(x, 2, axis=-1)
    return jnp.concatenate([x1 * c - x2 * s, x2 * c + x1 * s], axis=-1)


def axial_rope(x, row, col):
    half = x.shape[-1] // 2
    inv = ROPE_THETA ** (-jnp.arange(0, half, 2, dtype=jnp.float32) / half)
    x32 = x.astype(jnp.float32)
    xr = rotate_half_pairs(x32[..., :half], row[:, None] * inv[None, :])
    xc = rotate_half_pairs(x32[..., half:], col[:, None] * inv[None, :])
    return jnp.concatenate([xr, xc], axis=-1).astype(x.dtype)


def axial_rope_attention(q, k, v, row, col, key_ok):
    Bn, L = q.shape[:2]
    nb = L // BLOCK
    G = B_HEADS // B_KV
    q = axial_rope(q, row, col)
    k = axial_rope(k, row, col)
    qb = jnp.moveaxis(q.reshape(Bn, nb, BLOCK, B_KV, G, HEAD_DIM), 1, 0)
    scale = HEAD_DIM ** -0.5

    def one_block(qblk):
        s = jnp.einsum('bqhgd,bkhd->bhgqk', qblk, k).astype(jnp.float32) * scale
        s = jnp.where(key_ok, s, MASK_VALUE)
        p = jax.nn.softmax(s, axis=-1).astype(v.dtype)
        return jnp.einsum('bhgqk,bkhd->bqhgd', p, v)

    o = lax.map(one_block, qb)
    return jnp.moveaxis(o, 0, 1).reshape(Bn, L, B_HEADS * HEAD_DIM)


def differential_attention(q, k, v, lam, bias_tab, key_ok, sub_g, lambda_init):
    Bn, L = q.shape[:2]
    nb = L // BLOCK
    G = C_HEADS // C_KV
    qb = jnp.moveaxis(q.reshape(Bn, nb, BLOCK, C_KV, G, 2, C_DIM), 1, 0)
    qpos = jnp.arange(L).reshape(nb, BLOCK)
    kpos = jnp.arange(L)
    scale = C_DIM ** -0.5

    def one_block(args):
        qblk, qp = args
        bias = bias_tab[t5_bucket(kpos[None, :] - qp[:, None])]
        bias = jnp.moveaxis(bias, -1, 0).reshape(C_KV, G, BLOCK, L).astype(jnp.float32)
        s = jnp.einsum('bqhgcd,bkhcd->bchgqk', qblk, k).astype(jnp.float32) * scale + bias
        s = jnp.where(key_ok, s, MASK_VALUE)
        p = jax.nn.softmax(s, axis=-1)
        a = p[:, 0] - lam * p[:, 1]
        return jnp.einsum('bhgqk,bkhd->bqhgd', a.astype(v.dtype), v)

    o = lax.map(one_block, (qb, qpos))
    o = jnp.moveaxis(o, 0, 1).reshape(Bn, L, C_HEADS, C_VDIM)
    o = rms_norm(o, sub_g) * (1.0 - lambda_init)
    return o.reshape(Bn, L, C_HEADS * C_VDIM)


def chunk_gla(q, k, v, log_f):
    Bn, L, H, dk = q.shape
    dv = v.shape[-1]
    n = L // CHUNK

    def to_chunks(t):
        return t.reshape(Bn, n, CHUNK, H, t.shape[-1]).transpose(1, 0, 3, 2, 4)

    tri = jnp.tril(jnp.ones((CHUNK, CHUNK), bool))[:, :, None]

    def step(S, inp):
        qi, ki, vi, gi = inp
        b = jnp.cumsum(gi, axis=2)
        diff = b[:, :, :, None, :] - b[:, :, None, :, :]
        decay = jnp.where(tri, jnp.exp(jnp.where(tri, diff, 0.0)), 0.0)
        attn = jnp.einsum('bhtk,bhtsk,bhsk->bhts', qi, decay, ki)
        o = jnp.einsum('bhts,bhsv->bhtv', attn, vi) + jnp.einsum('bhtk,bhkv->bhtv', qi * jnp.exp(b), S)
        b_last = b[:, :, -1:, :]
        S = S * jnp.exp(b_last[:, :, 0, :, None]) + jnp.einsum('bhsk,bhsv->bhkv', ki * jnp.exp(b_last - b), vi)
        return S, o

    S0 = jnp.zeros((Bn, H, dk, dv), jnp.float32)
    _, o = lax.scan(step, S0, (to_chunks(q), to_chunks(k), to_chunks(v), to_chunks(log_f)))
    return o.transpose(1, 0, 3, 2, 4).reshape(Bn, L, H, dv)


def hgrn2_bidirectional(q, zf, zb, i, g, lb_f, lb_b, valid, out_g):
    Bn, L = q.shape[:2]

    def heads(t, dh):
        return t.reshape(Bn, L, D_HEADS, dh).astype(jnp.float32)

    qh = heads(q, D_KDIM) * (D_KDIM ** -0.5)
    vh = heads(i, D_VDIM)
    vmask = valid[None, :, None, None]

    def gates(z, lb):
        z = heads(z, D_KDIM)
        lb = lb.astype(jnp.float32).reshape(D_HEADS, D_KDIM)
        f = lb + (1.0 - lb) * jax.nn.sigmoid(z)
        log_f = jnp.log(jnp.maximum(f, F_FLOOR))
        kk = (1.0 - lb) * jax.nn.sigmoid(-z) * vmask
        return log_f, kk

    lf_f, k_f = gates(zf, lb_f)
    lf_b, k_b = gates(zb, lb_b)
    flip = lambda t: jnp.flip(t, axis=1)
    o_f = chunk_gla(qh, k_f, vh, lf_f)
    o_b = flip(chunk_gla(flip(qh), flip(k_b), flip(vh), flip(lf_b)))
    o = rms_norm(o_f + o_b, out_g) * jax.nn.silu(heads(g, D_VDIM))
    return o.reshape(Bn, L, D_HEADS * D_VDIM).astype(q.dtype)


def setup_inputs(seed: int = 0) -> dict:
    key = jax.random.key(seed)
    ks = jax.random.split(key, 20)
    nrm = jax.random.normal
    f32 = jnp.float32
    return {
        'x': nrm(ks[0], (BATCH, SEQ, D_MODEL), f32),
        'meta_tokens': nrm(ks[1], (N_META, D_MODEL), f32),
        'rel_bias': 0.5 * nrm(ks[2], (N_BUCKETS, A_HEADS + C_HEADS), f32),
        'hgrn_lb_logits': 0.5 * nrm(ks[3], (2, DEPTH, D_HEADS * D_KDIM), f32),
        'ln_mix': 1.0 + 0.05 * nrm(ks[4], (DEPTH, D_MODEL), f32),
        'w_in': nrm(ks[5], (DEPTH, D_MODEL, IN_TOTAL), f32) * D_MODEL ** -0.5,
        'attn_sink': 0.5 * nrm(ks[6], (DEPTH, A_HEADS), f32),
        'qk_norm_q': 1.0 + 0.05 * nrm(ks[7], (DEPTH, HEAD_DIM), f32),
        'qk_norm_k': 1.0 + 0.05 * nrm(ks[8], (DEPTH, HEAD_DIM), f32),
        'diff_lambda': 0.1 * nrm(ks[9], (DEPTH, 4, C_DIM), f32),
        'diff_subnorm': 1.0 + 0.05 * nrm(ks[10], (DEPTH, C_VDIM), f32),
        'hgrn_out_norm': 1.0 + 0.05 * nrm(ks[11], (DEPTH, D_VDIM), f32),
        'w_branch': nrm(ks[12], (DEPTH, N_BRANCH, MIX_W, D_MODEL), f32) * MIX_W ** -0.5,
        'w_out': nrm(ks[13], (DEPTH, D_MODEL, D_MODEL), f32) * D_MODEL ** -0.5,
        'ln_ffn': 1.0 + 0.05 * nrm(ks[14], (DEPTH, D_MODEL), f32),
        'w_ffn_gate': nrm(ks[15], (DEPTH, D_MODEL, D_FF), f32) * D_MODEL ** -0.5,
        'w_ffn_up': nrm(ks[16], (DEPTH, D_MODEL, D_FF), f32) * D_MODEL ** -0.5,
        'w_ffn_down': nrm(ks[17], (DEPTH, D_FF, D_MODEL), f32) * D_FF ** -0.5,
        'ln_final': 1.0 + 0.05 * nrm(ks[18], (D_MODEL,), f32),
    }


def reference(x, meta_tokens, rel_bias, hgrn_lb_logits, ln_mix, w_in, attn_sink, qk_norm_q, qk_norm_k,
              diff_lambda, diff_subnorm, hgrn_out_norm, w_branch, w_out, ln_ffn, w_ffn_gate, w_ffn_up,
              w_ffn_down, ln_final):
    Bn, S, _ = x.shape
    ROWS = S // GRID_W
    L = META_END + S
    h = jnp.concatenate([jnp.zeros((Bn, FRONT, D_MODEL), x.dtype),
                         jnp.broadcast_to(meta_tokens.astype(x.dtype)[None], (Bn, N_META, D_MODEL)),
                         x], axis=1)
    pos = jnp.arange(L)
    key_ok = pos >= FRONT
    valid = key_ok.astype(jnp.float32)
    row = jnp.concatenate([jnp.zeros((FRONT,), jnp.int32), -jnp.ones((N_META,), jnp.int32),
                           jnp.repeat(jnp.arange(ROWS, dtype=jnp.int32), GRID_W)]).astype(jnp.float32)
    col = jnp.concatenate([jnp.zeros((FRONT,), jnp.int32), jnp.arange(N_META, dtype=jnp.int32),
                           jnp.tile(jnp.arange(GRID_W, dtype=jnp.int32), ROWS)]).astype(jnp.float32)
    lb_p = jax.nn.softmax(hgrn_lb_logits.astype(jnp.float32), axis=1)
    lb_all = jnp.cumsum(lb_p, axis=1) - lb_p[:, :1]
    bias_a = rel_bias[:, :A_HEADS]
    bias_c = rel_bias[:, A_HEADS:]

    for l in range(DEPTH):
        u = rms_norm(h, ln_mix[l])
        (aq, ak, av, bq, bk, bv, cq, ck, cv, dq, dzf, dzb, di, dg, gz) = split_in(u @ w_in[l])
        y_a = windowed_sink_attention(aq.reshape(Bn, L, A_HEADS, HEAD_DIM), ak.reshape(Bn, L, A_KV, HEAD_DIM),
                                      av.reshape(Bn, L, A_KV, HEAD_DIM), attn_sink[l], bias_a)
        y_b = axial_rope_attention(rms_norm(bq.reshape(Bn, L, B_HEADS, HEAD_DIM), qk_norm_q[l]),
                                   rms_norm(bk.reshape(Bn, L, B_KV, HEAD_DIM), qk_norm_k[l]),
                                   bv.reshape(Bn, L, B_KV, HEAD_DIM), row, col, key_ok)
        lam_init = 0.8 - 0.6 * math.exp(-0.3 * l)
        lam_p = diff_lambda[l].astype(jnp.float32)
        lam = jnp.exp(jnp.sum(lam_p[0] * lam_p[1])) - jnp.exp(jnp.sum(lam_p[2] * lam_p[3])) + lam_init
        y_c = differential_attention(cq.reshape(Bn, L, C_HEADS, 2, C_DIM), ck.reshape(Bn, L, C_KV, 2, C_DIM),
                                     cv.reshape(Bn, L, C_KV, C_VDIM), lam, bias_c, key_ok, diff_subnorm[l], lam_init)
        y_d = hgrn2_bidirectional(dq, dzf, dzb, di, dg, lb_all[0, l], lb_all[1, l], valid, hgrn_out_norm[l])
        gate = jax.nn.sigmoid(gz.astype(jnp.float32)).astype(h.dtype).reshape(Bn, L, N_BRANCH, D_MODEL)
        merged = gate[:, :, 0] * (y_a @ w_branch[l, 0])
        for n, y in enumerate((y_b, y_c, y_d), start=1):
            merged = merged + gate[:, :, n] * (y @ w_branch[l, n])
        h = h + merged @ w_out[l]
        u = rms_norm(h, ln_ffn[l])
        h = h + (jax.nn.silu(u @ w_ffn_gate[l]) * (u @ w_ffn_up[l])) @ w_ffn_down[l]

    return rms_norm(h, ln_final)[:, META_END:]
```

```python
import functools
import math

import jax
import jax.numpy as jnp
import numpy as np
from jax import lax
from jax.experimental import pallas as pl
from jax.experimental.pallas import tpu as pltpu

D_MODEL = 1024
N_BRANCH = 4
MIX_W = D_MODEL // N_BRANCH
HEAD_DIM = 64
BLOCK = 128
WINDOW = 128
ROPE_THETA = 10000.0
GRID_W = 64
N_HEADS = 4
N_KV = 2
C_DIM = 32
N_META = 16
FRONT = (-N_META) % BLOCK
META_END = FRONT + N_META
N_BUCKETS = 32
MAX_DIST = 128
D_FF = -(-(8 * D_MODEL) // (3 * 256)) * 256
RMS_EPS = 1e-6
NEG = -1e30
F_FLOOR = 1e-30
LOG2E = 1.4426950408889634
LANES = 128
CHUNK = 16
ROW_TILE = 384
QKV_W = 3 * 512
D_W = 5 * MIX_W
VMEM_LIMIT = 56 * 1024 * 1024

F32 = jnp.float32
BF16 = jnp.bfloat16


def _dot(a, b):
    return jnp.dot(a, b, preferred_element_type=F32)


def _dot_nt(a, b):
    return lax.dot_general(a, b, (((1,), (1,)), ((), ())), preferred_element_type=F32)


def _split_dot(x, sel):
    hi = x.astype(BF16)
    lo = (x - hi.astype(F32)).astype(BF16)
    return _dot(hi, sel) + _dot(lo, sel)


def _rms(x, g):
    return x * lax.rsqrt(jnp.mean(x * x, axis=-1, keepdims=True) + RMS_EPS) * g


def _params(sem):
    return pltpu.CompilerParams(dimension_semantics=sem, vmem_limit_bytes=VMEM_LIMIT)


def _const_spec(shape):
    nd = len(shape)
    return pl.BlockSpec(shape, lambda *_: (0,) * nd)


def _in_proj_kernel(h_ref, g_ref, wqkv_ref, wd_ref, cs_ref, cos_ref, sin_ref, gn_ref, hsel_ref, qkv_ref, d_ref):
    u = _rms(h_ref[0], g_ref[...]).astype(BF16)
    acc = _dot(u, wqkv_ref[...]) * cs_ref[...]
    qkv_ref[0, :, 0:512] = acc[:, 0:512].astype(BF16)
    for t in range(3):
        c0 = 512 + LANES * t
        xt = acc[:, c0:c0 + LANES]
        ms = _split_dot(xt * xt, hsel_ref[...]) * (1.0 / HEAD_DIM)
        xt = xt * lax.rsqrt(ms + RMS_EPS) * gn_ref[t:t + 1, :]
        xt = xt * cos_ref[...] + pltpu.roll(xt, LANES // 2, 1) * sin_ref[...]
        qkv_ref[0, :, c0:c0 + LANES] = xt.astype(BF16)
    qkv_ref[0, :, 896:QKV_W] = acc[:, 896:QKV_W].astype(BF16)
    d_ref[0] = _dot(u, wd_ref[...])


def _in_proj(h, g, wqkv, wd, cs, cos_t, sin_t, gn, hsel):
    bn, L, _ = h.shape
    nt = L // ROW_TILE
    return pl.pallas_call(
        _in_proj_kernel,
        grid=(bn, nt),
        in_specs=[
            pl.BlockSpec((1, ROW_TILE, D_MODEL), lambda b, i: (b, i, 0)),
            _const_spec((1, D_MODEL)),
            _const_spec((D_MODEL, QKV_W)),
            _const_spec((D_MODEL, D_W)),
            _const_spec((1, QKV_W)),
            pl.BlockSpec((ROW_TILE, LANES), lambda b, i: (i, 0)),
            pl.BlockSpec((ROW_TILE, LANES), lambda b, i: (i, 0)),
            _const_spec((3, LANES)),
            _const_spec((LANES, LANES)),
        ],
        out_specs=[
            pl.BlockSpec((1, ROW_TILE, QKV_W), lambda b, i: (b, i, 0)),
            pl.BlockSpec((1, ROW_TILE, D_W), lambda b, i: (b, i, 0)),
        ],
        out_shape=[jax.ShapeDtypeStruct((bn, L, QKV_W), BF16), jax.ShapeDtypeStruct((bn, L, D_W), F32)],
        compiler_params=_params(("parallel", "parallel")),
        name="in_proj",
    )(h, g, wqkv, wd, cs, cos_t, sin_t, gn, hsel)


def _stack_q(q_ref, masks):
    parts = []
    for t in range(2):
        qt = q_ref[0, :, t * LANES:(t + 1) * LANES]
        for m in masks:
            parts.append(jnp.where(m, qt, jnp.zeros_like(qt)))
    return jnp.concatenate(parts, axis=0)


def _blk(ref, j):
    if isinstance(j, int):
        return ref[0, j * BLOCK:(j + 1) * BLOCK, :]
    return ref[0, pl.ds(pl.multiple_of(j * BLOCK, BLOCK), BLOCK), :]


def _attn_a_kernel(sink_ref, q_ref, k_ref, v_ref, bband_ref, bmeta_ref, o_ref):
    i = pl.program_id(1)
    nb = pl.num_programs(1)
    prev = jnp.maximum(i - 1, 0)
    nxt = jnp.minimum(i + 1, nb - 1)
    kband = jnp.concatenate([_blk(k_ref, prev), _blk(k_ref, i), _blk(k_ref, nxt)], axis=0)
    vband = jnp.concatenate([_blk(v_ref, prev), _blk(v_ref, i), _blk(v_ref, nxt)], axis=0)
    kmeta = k_ref[0, 0:BLOCK, :]
    vmeta = v_ref[0, 0:BLOCK, :]
    col = lax.broadcasted_iota(jnp.int32, (1, 3 * BLOCK), 1)
    lo = jnp.where(i == 0, 2 * BLOCK, jnp.where(i == 1, BLOCK, 0))
    hi = jnp.where(i == nb - 1, 2 * BLOCK, 3 * BLOCK)
    dead = (col < lo) | (col >= hi)
    lane = lax.broadcasted_iota(jnp.int32, (1, LANES), 1)
    outs = []
    for kv in range(N_KV):
        qs = _stack_q(q_ref, [(lane >= HEAD_DIM) == bool(kv)])
        bb = jnp.concatenate([bband_ref[2 * kv], bband_ref[2 * kv + 1]], axis=0)
        bm = jnp.concatenate([bmeta_ref[0, 2 * kv], bmeta_ref[0, 2 * kv + 1]], axis=0)
        sb = jnp.where(dead, NEG, _dot_nt(qs, kband) + bb)
        sm = _dot_nt(qs, kmeta) + bm
        sk = jnp.concatenate([jnp.full((BLOCK, 1), sink_ref[2 * kv], F32),
                              jnp.full((BLOCK, 1), sink_ref[2 * kv + 1], F32)], axis=0)
        m = jnp.maximum(jnp.maximum(jnp.max(sb, axis=-1, keepdims=True), jnp.max(sm, axis=-1, keepdims=True)), sk)
        pb = jnp.exp2(sb - m)
        pm = jnp.exp2(sm - m)
        den = jnp.sum(pb, axis=-1, keepdims=True) + jnp.sum(pm, axis=-1, keepdims=True) + jnp.exp2(sk - m)
        outs.append((_dot(pb.astype(BF16), vband) + _dot(pm.astype(BF16), vmeta)) / den)
    for t in range(2):
        y = jnp.where(lane < HEAD_DIM, outs[0][t * BLOCK:(t + 1) * BLOCK], outs[1][t * BLOCK:(t + 1) * BLOCK])
        o_ref[0, :, t * LANES:(t + 1) * LANES] = y.astype(BF16)


def _attn_a(qkv, sink, bband, bmeta):
    bn, L, _ = qkv.shape
    nb = L // BLOCK
    return pl.pallas_call(
        _attn_a_kernel,
        grid=(bn, nb),
        in_specs=[
            pl.BlockSpec(memory_space=pltpu.SMEM),
            pl.BlockSpec((1, BLOCK, 2 * LANES), lambda b, i: (b, i, 0)),
            pl.BlockSpec((1, L, LANES), lambda b, i: (b, 0, 2)),
            pl.BlockSpec((1, L, LANES), lambda b, i: (b, 0, 3)),
            _const_spec((N_HEADS, BLOCK, 3 * BLOCK)),
            pl.BlockSpec((1, N_HEADS, BLOCK, BLOCK), lambda b, i: (jnp.minimum(i, 2), 0, 0, 0)),
        ],
        out_specs=pl.BlockSpec((1, BLOCK, MIX_W), lambda b, i: (b, i, 0)),
        out_shape=jax.ShapeDtypeStruct((bn, L, MIX_W), BF16),
        compiler_params=_params(("parallel", "parallel")),
        name="attn_window",
    )(sink, qkv, qkv, qkv, bband, bmeta)


def _attn_b_kernel(q_ref, k_ref, v_ref, o_ref):
    L = k_ref.shape[1]
    lane = lax.broadcasted_iota(jnp.int32, (1, LANES), 1)
    head_b = (lane // (HEAD_DIM // 2)) % 2
    keycol = lax.broadcasted_iota(jnp.int32, (1, BLOCK), 1)
    k0, v0 = k_ref[0, 0:BLOCK, :], v_ref[0, 0:BLOCK, :]
    k1, v1 = k_ref[0, BLOCK:L, :], v_ref[0, BLOCK:L, :]
    outs = []
    for kv in range(N_KV):
        qs = _stack_q(q_ref, [head_b == kv])
        s0 = jnp.where(keycol >= FRONT, _dot_nt(qs, k0), NEG)
        s1 = _dot_nt(qs, k1)
        m = jnp.maximum(jnp.max(s0, axis=-1, keepdims=True), jnp.max(s1, axis=-1, keepdims=True))
        p0 = jnp.exp2(s0 - m)
        p1 = jnp.exp2(s1 - m)
        den = jnp.sum(p0, axis=-1, keepdims=True) + jnp.sum(p1, axis=-1, keepdims=True)
        outs.append((_dot(p0.astype(BF16), v0) + _dot(p1.astype(BF16), v1)) / den)
    for t in range(2):
        y = jnp.where(lane < HEAD_DIM, outs[0][t * BLOCK:(t + 1) * BLOCK], outs[1][t * BLOCK:(t + 1) * BLOCK])
        o_ref[0, :, t * LANES:(t + 1) * LANES] = y.astype(BF16)


def _attn_b(qkv):
    bn, L, _ = qkv.shape
    nb = L // BLOCK
    return pl.pallas_call(
        _attn_b_kernel,
        grid=(bn, nb),
        in_specs=[
            pl.BlockSpec((1, BLOCK, 2 * LANES), lambda b, i: (b, i, 2)),
            pl.BlockSpec((1, L, LANES), lambda b, i: (b, 0, 6)),
            pl.BlockSpec((1, L, LANES), lambda b, i: (b, 0, 7)),
        ],
        out_specs=pl.BlockSpec((1, BLOCK, MIX_W), lambda b, i: (b, i, 0)),
        out_shape=jax.ShapeDtypeStruct((bn, L, MIX_W), BF16),
        compiler_params=_params(("parallel", "parallel")),
        name="attn_rope",
    )(qkv, qkv, qkv)


def _attn_c_kernel(sc_ref, q_ref, k_ref, v_ref, bias_ref, gsub_ref, hsel_ref, o_ref, qs_ref, s_ref, m_ref, l_ref, acc_ref):
    i = pl.program_id(1)
    nb = pl.num_programs(1)
    lam = sc_ref[0]
    lane = lax.broadcasted_iota(jnp.int32, (1, LANES), 1)
    keycol = lax.broadcasted_iota(jnp.int32, (1, BLOCK), 1)
    grp = lane // C_DIM
    outs = []
    for kv in range(N_KV):
        qs_ref[...] = _stack_q(q_ref, [grp == 2 * kv, grp == 2 * kv + 1])

        def scores(j, first):
            s = _dot_nt(qs_ref[...], _blk(k_ref, j))
            t = jnp.clip(j - i + 2, 0, 4)
            b0 = bias_ref[t, 2 * kv]
            b1 = bias_ref[t, 2 * kv + 1]
            s = s + jnp.concatenate([b0, b0, b1, b1], axis=0)
            if first:
                s = jnp.where(keycol >= FRONT, s, NEG)
            s_ref[j] = s
            return s

        m_ref[...] = scores(0, True)

        def pass1(j, carry):
            m_ref[...] = jnp.maximum(m_ref[...], scores(j, False))
            return carry

        lax.fori_loop(1, nb, pass1, 0)
        mrow = jnp.max(m_ref[...], axis=-1, keepdims=True)
        l_ref[...] = jnp.zeros_like(l_ref)
        acc_ref[...] = jnp.zeros_like(acc_ref)

        def pass2(j, carry):
            p = jnp.exp2(s_ref[j] - mrow)
            l_ref[...] += p
            acc_ref[...] += _dot(p.astype(BF16), _blk(v_ref, j))
            return carry

        lax.fori_loop(0, nb, pass2, 0)
        o = acc_ref[...] / jnp.sum(l_ref[...], axis=-1, keepdims=True)
        outs.append([o[(2 * g) * BLOCK:(2 * g + 1) * BLOCK] - lam * o[(2 * g + 1) * BLOCK:(2 * g + 2) * BLOCK]
                     for g in range(2)])
    for g in range(2):
        y = jnp.where(lane < HEAD_DIM, outs[0][g], outs[1][g])
        ms = _split_dot(y * y, hsel_ref[...]) * (1.0 / HEAD_DIM)
        y = y * lax.rsqrt(ms + RMS_EPS) * gsub_ref[...] * sc_ref[1]
        o_ref[0, :, g * LANES:(g + 1) * LANES] = y.astype(BF16)


def _attn_c(qkv, scal, bias5, gsub, hsel):
    bn, L, _ = qkv.shape
    nb = L // BLOCK
    return pl.pallas_call(
        _attn_c_kernel,
        grid=(bn, nb),
        in_specs=[
            pl.BlockSpec(memory_space=pltpu.SMEM),
            pl.BlockSpec((1, BLOCK, 2 * LANES), lambda b, i: (b, i, 4)),
            pl.BlockSpec((1, L, LANES), lambda b, i: (b, 0, 10)),
            pl.BlockSpec((1, L, LANES), lambda b, i: (b, 0, 11)),
            _const_spec((5, N_HEADS, BLOCK, BLOCK)),
            _const_spec((1, LANES)),
            _const_spec((LANES, LANES)),
        ],
        out_specs=pl.BlockSpec((1, BLOCK, MIX_W), lambda b, i: (b, i, 0)),
        out_shape=jax.ShapeDtypeStruct((bn, L, MIX_W), BF16),
        scratch_shapes=[
            pltpu.VMEM((4 * BLOCK, LANES), BF16),
            pltpu.VMEM((nb, 4 * BLOCK, BLOCK), F32),
            pltpu.VMEM((4 * BLOCK, BLOCK), F32),
            pltpu.VMEM((4 * BLOCK, BLOCK), F32),
            pltpu.VMEM((4 * BLOCK, LANES), F32),
        ],
        compiler_params=_params(("parallel", "parallel")),
        name="attn_diff",
    )(scal, qkv, qkv, qkv, bias5, gsub, hsel)


def _hgrn_direction(forward, first_tile, q_ref, z_ref, v_ref, lb_ref, bd_ref, o_ref, st_ref, u_ref, qd_ref, gam_ref):
    R = q_ref.shape[1]
    W = MIX_W
    nch = R // CHUNK
    z = z_ref[0]
    q = q_ref[0] * (HEAD_DIM ** -0.5)
    v = v_ref[0]
    lb = lb_ref[...]
    row = lax.broadcasted_iota(jnp.int32, (R, 1), 0)
    rin = row % CHUNK
    f = lb + (1.0 - lb) * jax.nn.sigmoid(z)
    g = jnp.log(jnp.maximum(f, F_FLOOR))
    kk = (1.0 - lb) * jax.nn.sigmoid(-z)
    kk = jnp.where(row < jnp.where(first_tile, FRONT, 0), 0.0, kk)
    b, c = g, g
    for sh in (1, 2, 4, 8):
        b = b + jnp.where(rin >= sh, pltpu.roll(b, sh, 0), 0.0)
        c = c + jnp.where(rin + sh < CHUNK, pltpu.roll(c, R - sh, 0), 0.0)
    x, e = (b, c - g) if forward else (c, b - g)
    qd_ref[...] = (q * jnp.exp(x)).astype(BF16)
    kd = kk * jnp.exp(e)
    gam_ref[...] = jnp.exp(b + c - g)
    o_intra = jnp.zeros((R, W), F32)
    for d in range(CHUNK):
        sh = d if forward else (R - d) % R
        ok = (rin >= d) if forward else (rin + d < CHUNK)
        xs, ks, vs = (x, kk, v) if d == 0 else (pltpu.roll(x, sh, 0), pltpu.roll(kk, sh, 0), pltpu.roll(v, sh, 0))
        e_d = jnp.exp(jnp.where(ok, x - xs, NEG)) * q * ks
        o_intra = o_intra + _dot(e_d.astype(BF16), bd_ref[...]) * vs
    lane = lax.broadcasted_iota(jnp.int32, (1, BLOCK), 1)
    vt = v.T.astype(BF16)
    kdb = kd.astype(BF16)
    for blk in range(R // BLOCK):
        vt_b = vt[:, blk * BLOCK:(blk + 1) * BLOCK]
        lhs = jnp.concatenate([jnp.where(lane // CHUNK == cidx, vt_b, jnp.zeros_like(vt_b))
                               for cidx in range(BLOCK // CHUNK)], axis=0)
        u = _dot(lhs, kdb[blk * BLOCK:(blk + 1) * BLOCK, :])
        for cidx in range(BLOCK // CHUNK):
            for hf in range(2):
                u_ref[blk * (BLOCK // CHUNK) + cidx, hf] = u[cidx * W + hf * LANES:cidx * W + (hf + 1) * LANES,
                                                             hf * LANES:(hf + 1) * LANES]
    o_ref[0] = o_intra
    hmask = (lax.broadcasted_iota(jnp.int32, (LANES, LANES), 0) // HEAD_DIM
             == lax.broadcasted_iota(jnp.int32, (LANES, LANES), 1) // HEAD_DIM)

    def step(n, carry):
        ci = n if forward else nch - 1 - n
        r0 = pl.multiple_of(ci * CHUNK, CHUNK)
        qd = qd_ref[pl.ds(r0, CHUNK), :]
        gam = gam_ref[pl.ds(r0, 1), :]
        for hf in range(2):
            st = st_ref[hf]
            o_ref[0, pl.ds(r0, CHUNK), hf * LANES:(hf + 1) * LANES] += _dot_nt(
                qd[:, hf * LANES:(hf + 1) * LANES], st.astype(BF16))
            st_ref[hf] = st * gam[:, hf * LANES:(hf + 1) * LANES] + jnp.where(hmask, u_ref[ci, hf], 0.0)
        return carry

    lax.fori_loop(0, nch, step, 0)


def _hgrn_kernel(qf_ref, zf_ref, vf_ref, qb_ref, zb_ref, vb_ref, lb_ref, bd_ref, of_ref, ob_ref,
                 st_ref, u_ref, qd_ref, gam_ref):
    j = pl.program_id(1)
    nt = pl.num_programs(1)

    @pl.when(j == 0)
    def _():
        st_ref[...] = jnp.zeros_like(st_ref)

    _hgrn_direction(True, j == 0, qf_ref, zf_ref, vf_ref, lb_ref.at[0], bd_ref, of_ref, st_ref.at[0], u_ref, qd_ref, gam_ref)
    _hgrn_direction(False, j == nt - 1, qb_ref, zb_ref, vb_ref, lb_ref.at[1], bd_ref, ob_ref, st_ref.at[1], u_ref, qd_ref, gam_ref)


def _hgrn(dproj, lb2, bd):
    bn, L, _ = dproj.shape
    nt = L // ROW_TILE
    fw = lambda c: pl.BlockSpec((1, ROW_TILE, MIX_W), lambda b, j: (b, j, c))
    bw = lambda c: pl.BlockSpec((1, ROW_TILE, MIX_W), lambda b, j: (b, nt - 1 - j, c))
    return pl.pallas_call(
        _hgrn_kernel,
        grid=(bn, nt),
        in_specs=[fw(0), fw(1), fw(3), bw(0), bw(2), bw(3), _const_spec((2, 1, MIX_W)), _const_spec((MIX_W, MIX_W))],
        out_specs=[pl.BlockSpec((1, ROW_TILE, MIX_W), lambda b, j: (b, j, 0)),
                   pl.BlockSpec((1, ROW_TILE, MIX_W), lambda b, j: (b, nt - 1 - j, 0))],
        out_shape=[jax.ShapeDtypeStruct((bn, L, MIX_W), F32)] * 2,
        scratch_shapes=[
            pltpu.VMEM((2, 2, LANES, LANES), F32),
            pltpu.VMEM((ROW_TILE // CHUNK, 2, LANES, LANES), F32),
            pltpu.VMEM((ROW_TILE, MIX_W), BF16),
            pltpu.VMEM((ROW_TILE, MIX_W), F32),
        ],
        compiler_params=_params(("parallel", "arbitrary")),
        name="hgrn",
    )(dproj, dproj, dproj, dproj, dproj, dproj, lb2, bd)


def _merge_kernel(h_ref, ya_ref, yb_ref, yc_ref, of_ref, ob_ref, dg_ref, g_ref, gout_ref, bd_ref,
                  wgz_ref, wb_ref, wo_ref, o_ref):
    h = h_ref[0]
    u = _rms(h, g_ref[...]).astype(BF16)
    od = of_ref[0] + ob_ref[0]
    ms = _split_dot(od * od, bd_ref[...]) * (1.0 / HEAD_DIM)
    yd = (od * lax.rsqrt(ms + RMS_EPS) * gout_ref[...] * jax.nn.silu(dg_ref[0])).astype(BF16)
    ys = (ya_ref[0], yb_ref[0], yc_ref[0], yd)
    merged = None
    for n in range(N_BRANCH):
        gate = jax.nn.sigmoid(_dot(u, wgz_ref[:, n * D_MODEL:(n + 1) * D_MODEL]))
        term = gate * _dot(ys[n], wb_ref[n])
        merged = term if merged is None else merged + term
    o_ref[0] = h + _dot(merged.astype(BF16), wo_ref[...])


def _merge(h, ya, yb, yc, of, ob, dproj, g, gout, bd, wgz, wb, wo):
    bn, L, _ = h.shape
    nt = L // ROW_TILE
    row = lambda w, c=0: pl.BlockSpec((1, ROW_TILE, w), lambda b, i: (b, i, c))
    return pl.pallas_call(
        _merge_kernel,
        grid=(bn, nt),
        in_specs=[row(D_MODEL), row(MIX_W), row(MIX_W), row(MIX_W), row(MIX_W), row(MIX_W), row(MIX_W, 4),
                  _const_spec((1, D_MODEL)), _const_spec((1, MIX_W)), _const_spec((MIX_W, MIX_W)),
                  _const_spec((D_MODEL, N_BRANCH * D_MODEL)), _const_spec((N_BRANCH, MIX_W, D_MODEL)),
                  _const_spec((D_MODEL, D_MODEL))],
        out_specs=row(D_MODEL),
        out_shape=jax.ShapeDtypeStruct((bn, L, D_MODEL), F32),
        compiler_params=_params(("parallel", "parallel")),
        name="merge",
    )(h, ya, yb, yc, of, ob, dproj, g, gout, bd, wgz, wb, wo)


def _ffn_kernel(h_ref, g_ref, wg_ref, wu_ref, wd_ref, o_ref):
    h = h_ref[0]
    u = _rms(h, g_ref[...]).astype(BF16)
    half = D_FF // 2
    out = h
    for c in range(2):
        a = _dot(u, wg_ref[:, c * half:(c + 1) * half])
        t = (jax.nn.silu(a) * _dot(u, wu_ref[:, c * half:(c + 1) * half])).astype(BF16)
        out = out + _dot(t, wd_ref[c * half:(c + 1) * half, :])
    o_ref[0] = out


def _ffn(h, g, wg, wu, wd):
    bn, L, _ = h.shape
    nt = L // ROW_TILE
    row = pl.BlockSpec((1, ROW_TILE, D_MODEL), lambda b, i: (b, i, 0))
    return pl.pallas_call(
        _ffn_kernel,
        grid=(bn, nt),
        in_specs=[row, _const_spec((1, D_MODEL)), _const_spec((D_MODEL, D_FF)), _const_spec((D_MODEL, D_FF)),
                  _const_spec((D_FF, D_MODEL))],
        out_specs=row,
        out_shape=jax.ShapeDtypeStruct((bn, L, D_MODEL), F32),
        compiler_params=_params(("parallel", "parallel")),
        name="ffn",
    )(h, g, wg, wu, wd)


def _final_kernel(h_ref, g_ref, o_ref):
    o_ref[0] = _rms(h_ref[0], g_ref[...])


def _final_norm(h, g):
    bn, L, _ = h.shape
    S = L - META_END
    return pl.pallas_call(
        _final_kernel,
        grid=(bn, S // BLOCK),
        in_specs=[pl.BlockSpec((1, BLOCK, D_MODEL), lambda b, i: (b, i + META_END // BLOCK, 0)), _const_spec((1, D_MODEL))],
        out_specs=pl.BlockSpec((1, BLOCK, D_MODEL), lambda b, i: (b, i, 0)),
        out_shape=jax.ShapeDtypeStruct((bn, S, D_MODEL), F32),
        compiler_params=_params(("parallel", "parallel")),
        name="final_norm",
    )(h, g)


def _t5_bucket(rel):
    half = N_BUCKETS // 2
    exact = half // 2
    n = jnp.abs(rel)
    nf = jnp.maximum(n, exact).astype(F32)
    big = exact + (jnp.log(nf / exact) / math.log(MAX_DIST / exact) * (half - exact)).astype(jnp.int32)
    big = jnp.minimum(big, half - 1)
    return jnp.where(rel > 0, half, 0) + jnp.where(n < exact, n, big)


def _bias_tables(rel_bias):
    bias_a = rel_bias[:, :N_HEADS].astype(F32) * LOG2E
    bias_c = rel_bias[:, N_HEADS:].astype(F32) * LOG2E
    r = jnp.arange(BLOCK)[:, None]
    rel_band = (jnp.arange(3 * BLOCK)[None, :] - BLOCK) - r
    bband = jnp.moveaxis(bias_a[_t5_bucket(rel_band)], -1, 0)
    bband = jnp.where(jnp.abs(rel_band)[None] <= WINDOW, bband, NEG)
    lane = jnp.arange(BLOCK)[None, :]
    bmeta = []
    for i in range(3):
        rel_m = lane - (i * BLOCK + r)
        bm = jnp.moveaxis(bias_a[_t5_bucket(rel_m)], -1, 0)
        bmeta.append(jnp.where((lane >= FRONT)[None], bm, NEG))
    bmeta = jnp.stack(bmeta)
    rel5 = (jnp.arange(5)[:, None, None] - 2) * BLOCK + lane[None] - r[None]
    bias5 = jnp.moveaxis(bias_c[_t5_bucket(rel5)], -1, 1)
    return bband, bmeta, bias5


def _rope_layout():
    q16 = np.arange(HEAD_DIM // 4)
    first = np.concatenate([q16, 32 + q16])
    second = np.concatenate([16 + q16, 48 + q16])
    dims = np.concatenate([first, first, second, second])
    head = np.concatenate([np.zeros(32, int), np.ones(32, int)] * 2)
    return dims, head


def _rope_tables(L, S):
    rows = S // GRID_W
    row = jnp.concatenate([jnp.zeros((FRONT,), jnp.int32), -jnp.ones((N_META,), jnp.int32),
                           jnp.repeat(jnp.arange(rows, dtype=jnp.int32), GRID_W)]).astype(F32)
    col = jnp.concatenate([jnp.zeros((FRONT,), jnp.int32), jnp.arange(N_META, dtype=jnp.int32),
                           jnp.tile(jnp.arange(GRID_W, dtype=jnp.int32), rows)]).astype(F32)
    half = HEAD_DIM // 2
    inv = ROPE_THETA ** (-jnp.arange(0, half, 2, dtype=F32) / half)
    ang = jnp.concatenate([row[:, None] * inv[None, :], col[:, None] * inv[None, :]], axis=-1)
    cos_t = jnp.tile(jnp.cos(ang), (1, 4))
    sin_t = jnp.concatenate([-jnp.tile(jnp.sin(ang), (1, 2)), jnp.tile(jnp.sin(ang), (1, 2))], axis=-1)
    return cos_t, sin_t


def _column_layout():
    rdims, rhead = _rope_layout()
    cols = []
    nat = np.arange(HEAD_DIM)
    for base, rope in ((0, False), (512, True), (1024, False)):
        for g in range(2):
            heads = (g, 2 + g)
            if rope:
                cols.append(base + np.array([heads[h] for h in rhead]) * HEAD_DIM + rdims)
            else:
                cols.append(np.concatenate([base + h * HEAD_DIM + nat for h in heads]))
        kbase = base + 256
        if rope:
            cols.append(kbase + rhead * HEAD_DIM + rdims)
        else:
            cols.append(kbase + np.arange(LANES))
        cols.append(base + 384 + np.arange(LANES))
    return np.concatenate(cols)


def _branch_row_layout():
    nat = np.arange(HEAD_DIM)
    return np.concatenate([h * HEAD_DIM + nat for h in (0, 2, 1, 3)])


def kernel(x, meta_tokens, rel_bias, hgrn_lb_logits, ln_mix, w_in, attn_sink, qk_norm_q, qk_norm_k, diff_lambda,
           diff_subnorm, hgrn_out_norm, w_branch, w_out, ln_ffn, w_ffn_gate, w_ffn_up, w_ffn_down, ln_final):
    bn, S, _ = x.shape
    L = META_END + S
    depth = w_in.shape[0]
    assert L % ROW_TILE == 0 and S % GRID_W == 0
    h = jnp.concatenate([jnp.zeros((bn, FRONT, D_MODEL), x.dtype),
                         jnp.broadcast_to(meta_tokens.astype(x.dtype)[None], (bn, N_META, D_MODEL)), x], axis=1)
    bband, bmeta, bias5 = _bias_tables(rel_bias)
    cos_t, sin_t = _rope_tables(L, S)
    rdims, rhead = _rope_layout()
    lane = np.arange(LANES)
    hsel_rope = jnp.asarray(rhead[:, None] == rhead[None, :], BF16)
    hsel_half = jnp.asarray((lane[:, None] // HEAD_DIM) == (lane[None, :] // HEAD_DIM), BF16)
    w256 = np.arange(MIX_W)
    bd = jnp.asarray((w256[:, None] // HEAD_DIM) == (w256[None, :] // HEAD_DIM), BF16)
    qkv_cols = _column_layout()
    yrows = _branch_row_layout()
    cs = np.ones((1, QKV_W), np.float32)
    cs[0, 0:256] = HEAD_DIM ** -0.5 * LOG2E
    cs[0, 1024:1280] = C_DIM ** -0.5 * LOG2E
    cs = jnp.asarray(cs)
    lb_p = jax.nn.softmax(hgrn_lb_logits.astype(F32), axis=1)
    lb_all = jnp.cumsum(lb_p, axis=1) - lb_p[:, :1]

    for l in range(depth):
        wl = w_in[l]
        wqkv = wl[:, qkv_cols].astype(BF16)
        wd = wl[:, QKV_W:QKV_W + D_W].astype(BF16)
        wgz = wl[:, QKV_W + D_W:].astype(BF16)
        gn = jnp.stack([qk_norm_q[l][rdims] * (HEAD_DIM ** -0.5 * LOG2E)] * 2 + [qk_norm_k[l][rdims]]).astype(F32)
        qkv, dproj = _in_proj(h, ln_mix[l][None], wqkv, wd, cs, cos_t, sin_t, gn, hsel_rope)
        ya = _attn_a(qkv, attn_sink[l].astype(F32) * LOG2E, bband, bmeta)
        yb = _attn_b(qkv)
        lam_init = 0.8 - 0.6 * math.exp(-0.3 * l)
        lam_p = diff_lambda[l].astype(F32)
        lam = jnp.exp(jnp.sum(lam_p[0] * lam_p[1])) - jnp.exp(jnp.sum(lam_p[2] * lam_p[3])) + lam_init
        scal = jnp.stack([lam, jnp.asarray(1.0 - lam_init, F32)])
        gsub = jnp.tile(diff_subnorm[l].astype(F32), 2)[None]
        yc = _attn_c(qkv, scal, bias5, gsub, hsel_half)
        lb2 = lb_all[:, l][:, None, :]
        of, ob = _hgrn(dproj, lb2, bd)
        wb = w_branch[l].astype(BF16)
        wb = jnp.concatenate([wb[:3][:, yrows], wb[3:]], axis=0)
        gout = jnp.tile(hgrn_out_norm[l].astype(F32), N_HEADS)[None]
        h = _merge(h, ya, yb, yc, of, ob, dproj, ln_mix[l][None], gout, bd, wgz, wb, w_out[l].astype(BF16))
        h = _ffn(h, ln_ffn[l][None], w_ffn_gate[l].astype(BF16), w_ffn_up[l].astype(BF16), w_ffn_down[l].astype(BF16))
    return _final_norm(h, ln_final[None])
```

```python
import functools
import math

import jax
import jax.numpy as jnp
import numpy as np
from jax import lax
from jax.experimental import pallas as pl
from jax.experimental.pallas import tpu as pltpu

D_MODEL = 1024
N_BRANCH = 4
MIX_W = D_MODEL // N_BRANCH
HEAD_DIM = 64
BLOCK = 128
WINDOW = 128
ROPE_THETA = 10000.0
GRID_W = 64
N_HEADS = 4
N_KV = 2
C_DIM = 32
N_META = 16
FRONT = (-N_META) % BLOCK
META_END = FRONT + N_META
N_BUCKETS = 32
MAX_DIST = 128
D_FF = -(-(8 * D_MODEL) // (3 * 256)) * 256
RMS_EPS = 1e-6
NEG = -1e30
F_FLOOR = 1e-30
LOG2E = 1.4426950408889634
LANES = 128
CHUNK = 16
ROW_TILE = 384
C_UNROLL = 4
QKV_W = 3 * 512
D_W = 5 * MIX_W
VMEM_LIMIT = 56 * 1024 * 1024

F32 = jnp.float32
BF16 = jnp.bfloat16


def _dot(a, b):
    return jnp.dot(a, b, preferred_element_type=F32)


def _dot_nt(a, b):
    return lax.dot_general(a, b, (((1,), (1,)), ((), ())), preferred_element_type=F32)


def _split_dot(x, sel):
    hi = x.astype(BF16)
    lo = (x - hi.astype(F32)).astype(BF16)
    return _dot(hi, sel) + _dot(lo, sel)


def _rms(x, g):
    return x * lax.rsqrt(jnp.mean(x * x, axis=-1, keepdims=True) + RMS_EPS) * g


def _params(sem):
    return pltpu.CompilerParams(dimension_semantics=sem, vmem_limit_bytes=VMEM_LIMIT)


def _const_spec(shape):
    nd = len(shape)
    return pl.BlockSpec(shape, lambda *_: (0,) * nd)


def _in_proj_kernel(h_ref, g_ref, wqkv_ref, wd_ref, cs_ref, cos_ref, sin_ref, gn_ref, hsel_ref, qkv_ref, d_ref):
    u = _rms(h_ref[0], g_ref[...]).astype(BF16)
    acc = _dot(u, wqkv_ref[...]) * cs_ref[...]
    qkv_ref[0, :, 0:512] = acc[:, 0:512].astype(BF16)
    for t in range(3):
        c0 = 512 + LANES * t
        xt = acc[:, c0:c0 + LANES]
        ms = _split_dot(xt * xt, hsel_ref[...]) * (1.0 / HEAD_DIM)
        xt = xt * lax.rsqrt(ms + RMS_EPS) * gn_ref[t:t + 1, :]
        xt = xt * cos_ref[...] + pltpu.roll(xt, LANES // 2, 1) * sin_ref[...]
        qkv_ref[0, :, c0:c0 + LANES] = xt.astype(BF16)
    qkv_ref[0, :, 896:QKV_W] = acc[:, 896:QKV_W].astype(BF16)
    d_ref[0] = _dot(u, wd_ref[...])


def _in_proj(h, g, wqkv, wd, cs, cos_t, sin_t, gn, hsel):
    bn, L, _ = h.shape
    nt = L // ROW_TILE
    return pl.pallas_call(
        _in_proj_kernel,
        grid=(bn, nt),
        in_specs=[
            pl.BlockSpec((1, ROW_TILE, D_MODEL), lambda b, i: (b, i, 0)),
            _const_spec((1, D_MODEL)),
            _const_spec((D_MODEL, QKV_W)),
            _const_spec((D_MODEL, D_W)),
            _const_spec((1, QKV_W)),
            pl.BlockSpec((ROW_TILE, LANES), lambda b, i: (i, 0)),
            pl.BlockSpec((ROW_TILE, LANES), lambda b, i: (i, 0)),
            _const_spec((3, LANES)),
            _const_spec((LANES, LANES)),
        ],
        out_specs=[
            pl.BlockSpec((1, ROW_TILE, QKV_W), lambda b, i: (b, i, 0)),
            pl.BlockSpec((1, ROW_TILE, D_W), lambda b, i: (b, i, 0)),
        ],
        out_shape=[jax.ShapeDtypeStruct((bn, L, QKV_W), BF16), jax.ShapeDtypeStruct((bn, L, D_W), F32)],
        compiler_params=_params(("parallel", "parallel")),
        name="in_proj",
    )(h, g, wqkv, wd, cs, cos_t, sin_t, gn, hsel)


def _stack_q(q_ref, masks):
    parts = []
    for t in range(2):
        qt = q_ref[0, :, t * LANES:(t + 1) * LANES]
        for m in masks:
            parts.append(jnp.where(m, qt, jnp.zeros_like(qt)))
    return jnp.concatenate(parts, axis=0)


def _blk(ref, j):
    if isinstance(j, int):
        return ref[0, j * BLOCK:(j + 1) * BLOCK, :]
    return ref[0, pl.ds(pl.multiple_of(j * BLOCK, BLOCK), BLOCK), :]


def _attn_a_kernel(sink_ref, q_ref, k_ref, v_ref, bband_ref, bmeta_ref, o_ref):
    i = pl.program_id(1)
    nb = pl.num_programs(1)
    prev = jnp.maximum(i - 1, 0)
    nxt = jnp.minimum(i + 1, nb - 1)
    kband = jnp.concatenate([_blk(k_ref, prev), _blk(k_ref, i), _blk(k_ref, nxt)], axis=0)
    vband = jnp.concatenate([_blk(v_ref, prev), _blk(v_ref, i), _blk(v_ref, nxt)], axis=0)
    kmeta = k_ref[0, 0:BLOCK, :]
    vmeta = v_ref[0, 0:BLOCK, :]
    col = lax.broadcasted_iota(jnp.int32, (1, 3 * BLOCK), 1)
    lo = jnp.where(i == 0, 2 * BLOCK, jnp.where(i == 1, BLOCK, 0))
    hi = jnp.where(i == nb - 1, 2 * BLOCK, 3 * BLOCK)
    dead = (col < lo) | (col >= hi)
    lane = lax.broadcasted_iota(jnp.int32, (1, LANES), 1)
    outs = []
    for kv in range(N_KV):
        qs = _stack_q(q_ref, [(lane >= HEAD_DIM) == bool(kv)])
        bb = jnp.concatenate([bband_ref[2 * kv], bband_ref[2 * kv + 1]], axis=0)
        bm = jnp.concatenate([bmeta_ref[0, 2 * kv], bmeta_ref[0, 2 * kv + 1]], axis=0)
        sb = jnp.where(dead, NEG, _dot_nt(qs, kband) + bb)
        sm = _dot_nt(qs, kmeta) + bm
        sk = jnp.concatenate([jnp.full((BLOCK, 1), sink_ref[2 * kv], F32),
                              jnp.full((BLOCK, 1), sink_ref[2 * kv + 1], F32)], axis=0)
        m = jnp.maximum(jnp.maximum(jnp.max(sb, axis=-1, keepdims=True), jnp.max(sm, axis=-1, keepdims=True)), sk)
        pb = jnp.exp2(sb - m)
        pm = jnp.exp2(sm - m)
        den = jnp.sum(pb, axis=-1, keepdims=True) + jnp.sum(pm, axis=-1, keepdims=True) + jnp.exp2(sk - m)
        outs.append((_dot(pb.astype(BF16), vband) + _dot(pm.astype(BF16), vmeta)) / den)
    for t in range(2):
        y = jnp.where(lane < HEAD_DIM, outs[0][t * BLOCK:(t + 1) * BLOCK], outs[1][t * BLOCK:(t + 1) * BLOCK])
        o_ref[0, :, t * LANES:(t + 1) * LANES] = y.astype(BF16)


def _attn_a(qkv, sink, bband, bmeta):
    bn, L, _ = qkv.shape
    nb = L // BLOCK
    return pl.pallas_call(
        _attn_a_kernel,
        grid=(bn, nb),
        in_specs=[
            pl.BlockSpec(memory_space=pltpu.SMEM),
            pl.BlockSpec((1, BLOCK, 2 * LANES), lambda b, i: (b, i, 0)),
            pl.BlockSpec((1, L, LANES), lambda b, i: (b, 0, 2)),
            pl.BlockSpec((1, L, LANES), lambda b, i: (b, 0, 3)),
            _const_spec((N_HEADS, BLOCK, 3 * BLOCK)),
            pl.BlockSpec((1, N_HEADS, BLOCK, BLOCK), lambda b, i: (jnp.minimum(i, 2), 0, 0, 0)),
        ],
        out_specs=pl.BlockSpec((1, BLOCK, MIX_W), lambda b, i: (b, i, 0)),
        out_shape=jax.ShapeDtypeStruct((bn, L, MIX_W), BF16),
        compiler_params=_params(("parallel", "parallel")),
        name="attn_window",
    )(sink, qkv, qkv, qkv, bband, bmeta)


def _attn_b_kernel(q_ref, k_ref, v_ref, o_ref):
    L = k_ref.shape[1]
    lane = lax.broadcasted_iota(jnp.int32, (1, LANES), 1)
    head_b = (lane // (HEAD_DIM // 2)) % 2
    keycol = lax.broadcasted_iota(jnp.int32, (1, BLOCK), 1)
    k0, v0 = k_ref[0, 0:BLOCK, :], v_ref[0, 0:BLOCK, :]
    k1, v1 = k_ref[0, BLOCK:L, :], v_ref[0, BLOCK:L, :]
    outs = []
    for kv in range(N_KV):
        qs = _stack_q(q_ref, [head_b == kv])
        s0 = jnp.where(keycol >= FRONT, _dot_nt(qs, k0), NEG)
        s1 = _dot_nt(qs, k1)
        m = jnp.maximum(jnp.max(s0, axis=-1, keepdims=True), jnp.max(s1, axis=-1, keepdims=True))
        p0 = jnp.exp2(s0 - m)
        p1 = jnp.exp2(s1 - m)
        den = jnp.sum(p0, axis=-1, keepdims=True) + jnp.sum(p1, axis=-1, keepdims=True)
        outs.append((_dot(p0.astype(BF16), v0) + _dot(p1.astype(BF16), v1)) / den)
    for t in range(2):
        y = jnp.where(lane < HEAD_DIM, outs[0][t * BLOCK:(t + 1) * BLOCK], outs[1][t * BLOCK:(t + 1) * BLOCK])
        o_ref[0, :, t * LANES:(t + 1) * LANES] = y.astype(BF16)


def _attn_b(qkv):
    bn, L, _ = qkv.shape
    nb = L // BLOCK
    return pl.pallas_call(
        _attn_b_kernel,
        grid=(bn, nb),
        in_specs=[
            pl.BlockSpec((1, BLOCK, 2 * LANES), lambda b, i: (b, i, 2)),
            pl.BlockSpec((1, L, LANES), lambda b, i: (b, 0, 6)),
            pl.BlockSpec((1, L, LANES), lambda b, i: (b, 0, 7)),
        ],
        out_specs=pl.BlockSpec((1, BLOCK, MIX_W), lambda b, i: (b, i, 0)),
        out_shape=jax.ShapeDtypeStruct((bn, L, MIX_W), BF16),
        compiler_params=_params(("parallel", "parallel")),
        name="attn_rope",
    )(qkv, qkv, qkv)


def _attn_c_kernel(sc_ref, q_ref, k_ref, v_ref, bias_ref, gsub_ref, hsel_ref, o_ref, qs_ref, s_ref, m_ref, acc_ref):
    i = pl.program_id(1)
    nb = k_ref.shape[1] // BLOCK
    ngrp = (nb - 1) // C_UNROLL
    lam = sc_ref[0]
    lane = lax.broadcasted_iota(jnp.int32, (1, LANES), 1)
    keycol = lax.broadcasted_iota(jnp.int32, (1, BLOCK), 1)
    grp = lane // C_DIM
    outs = []
    for kv in range(N_KV):
        qs_ref[...] = _stack_q(q_ref, [grp == 2 * kv, grp == 2 * kv + 1])
        own = (lane >= HEAD_DIM) == bool(kv)

        def scores(j0, width):
            if isinstance(j0, int):
                kb = k_ref[0, j0 * BLOCK:(j0 + width) * BLOCK, :]
            else:
                kb = k_ref[0, pl.ds(pl.multiple_of(j0 * BLOCK, BLOCK), width * BLOCK), :]
            s_all = _dot_nt(qs_ref[...], kb)
            m = None
            for u in range(width):
                t = jnp.clip(j0 + u - i + 2, 0, 4)
                b0 = bias_ref[t, 2 * kv]
                b1 = bias_ref[t, 2 * kv + 1]
                s = s_all[:, u * BLOCK:(u + 1) * BLOCK] + jnp.concatenate([b0, b0, b1, b1], axis=0)
                if isinstance(j0, int) and j0 + u == 0:
                    s = jnp.where(keycol >= FRONT, s, NEG)
                s_ref[j0 + u] = s
                m = s if m is None else jnp.maximum(m, s)
            return m

        def vones(j):
            vb = _blk(v_ref, j)
            return jnp.where(own, vb, jnp.ones_like(vb))

        def blocks(n):
            return [1 + n * C_UNROLL + u for u in range(C_UNROLL)]

        rest = list(range(1 + ngrp * C_UNROLL, nb))

        def row_max(js):
            m = scores(js[0], 1)
            for j in js[1:]:
                m = jnp.maximum(m, scores(j, 1))
            return m

        m_ref[...] = row_max([0] + rest)

        def pass1(n, carry):
            j0 = 1 + n * C_UNROLL
            m = m_ref[...]
            for u in range(0, C_UNROLL, 2):
                m = jnp.maximum(m, scores(j0 + u, 2))
            m_ref[...] = m
            return carry

        lax.fori_loop(0, ngrp, pass1, 0)
        m_ref[...] = jnp.broadcast_to(jnp.max(m_ref[...], axis=-1, keepdims=True), m_ref.shape)

        def weighted(js):
            p = jnp.concatenate([jnp.exp2(s_ref[j] - m_ref[...]).astype(BF16) for j in js], axis=1)
            return _dot(p, jnp.concatenate([vones(j) for j in js], axis=0))

        acc_ref[...] = weighted([0] + rest)

        def pass2(n, carry):
            acc_ref[...] += weighted(blocks(n))
            return carry

        lax.fori_loop(0, ngrp, pass2, 0)
        acc = acc_ref[...]
        o = acc / pltpu.roll(acc, HEAD_DIM, 1)
        outs.append([o[(2 * g) * BLOCK:(2 * g + 1) * BLOCK] - lam * o[(2 * g + 1) * BLOCK:(2 * g + 2) * BLOCK]
                     for g in range(2)])
    for g in range(2):
        y = jnp.where(lane < HEAD_DIM, outs[0][g], outs[1][g])
        ms = _split_dot(y * y, hsel_ref[...]) * (1.0 / HEAD_DIM)
        y = y * lax.rsqrt(ms + RMS_EPS) * gsub_ref[...] * sc_ref[1]
        o_ref[0, :, g * LANES:(g + 1) * LANES] = y.astype(BF16)


def _attn_c(qkv, scal, bias5, gsub, hsel):
    bn, L, _ = qkv.shape
    nb = L // BLOCK
    return pl.pallas_call(
        _attn_c_kernel,
        grid=(bn, nb),
        in_specs=[
            pl.BlockSpec(memory_space=pltpu.SMEM),
            pl.BlockSpec((1, BLOCK, 2 * LANES), lambda b, i: (b, i, 4)),
            pl.BlockSpec((1, L, LANES), lambda b, i: (b, 0, 10)),
            pl.BlockSpec((1, L, LANES), lambda b, i: (b, 0, 11)),
            _const_spec((5, N_HEADS, BLOCK, BLOCK)),
            _const_spec((1, LANES)),
            _const_spec((LANES, LANES)),
        ],
        out_specs=pl.BlockSpec((1, BLOCK, MIX_W), lambda b, i: (b, i, 0)),
        out_shape=jax.ShapeDtypeStruct((bn, L, MIX_W), BF16),
        scratch_shapes=[
            pltpu.VMEM((4 * BLOCK, LANES), BF16),
            pltpu.VMEM((nb, 4 * BLOCK, BLOCK), F32),
            pltpu.VMEM((4 * BLOCK, BLOCK), F32),
            pltpu.VMEM((4 * BLOCK, LANES), F32),
        ],
        compiler_params=_params(("parallel", "parallel")),
        name="attn_diff",
    )(scal, qkv, qkv, qkv, bias5, gsub, hsel)


def _hgrn_direction(forward, first_tile, q_ref, z_ref, v_ref, lb_ref, bd_ref, o_ref, st_ref, u_ref, qd_ref, gam_ref):
    R = q_ref.shape[1]
    W = MIX_W
    nch = R // CHUNK
    z = z_ref[0]
    q = q_ref[0] * (HEAD_DIM ** -0.5)
    v = v_ref[0]
    lb = lb_ref[...]
    row = lax.broadcasted_iota(jnp.int32, (R, 1), 0)
    rin = row % CHUNK
    f = lb + (1.0 - lb) * jax.nn.sigmoid(z)
    g = jnp.log(jnp.maximum(f, F_FLOOR))
    kk = (1.0 - lb) * jax.nn.sigmoid(-z)
    kk = jnp.where(row < jnp.where(first_tile, FRONT, 0), 0.0, kk)
    b, c = g, g
    for sh in (1, 2, 4, 8):
        b = b + jnp.where(rin >= sh, pltpu.roll(b, sh, 0), 0.0)
        c = c + jnp.where(rin + sh < CHUNK, pltpu.roll(c, R - sh, 0), 0.0)
    x, e = (b, c - g) if forward else (c, b - g)
    qd_ref[...] = (q * jnp.exp(x)).astype(BF16)
    kd = kk * jnp.exp(e)
    gam_ref[...] = jnp.exp(b + c - g)
    o_intra = jnp.zeros((R, W), F32)
    for d in range(CHUNK):
        sh = d if forward else (R - d) % R
        ok = (rin >= d) if forward else (rin + d < CHUNK)
        xs, ks, vs = (x, kk, v) if d == 0 else (pltpu.roll(x, sh, 0), pltpu.roll(kk, sh, 0), pltpu.roll(v, sh, 0))
        e_d = jnp.exp(jnp.where(ok, x - xs, NEG)) * q * ks
        o_intra = o_intra + _dot(e_d.astype(BF16), bd_ref[...]) * vs
    lane = lax.broadcasted_iota(jnp.int32, (1, BLOCK), 1)
    vt = v.T.astype(BF16)
    kdb = kd.astype(BF16)
    for blk in range(R // BLOCK):
        vt_b = vt[:, blk * BLOCK:(blk + 1) * BLOCK]
        lhs = jnp.concatenate([jnp.where(lane // CHUNK == cidx, vt_b, jnp.zeros_like(vt_b))
                               for cidx in range(BLOCK // CHUNK)], axis=0)
        u = _dot(lhs, kdb[blk * BLOCK:(blk + 1) * BLOCK, :])
        for cidx in range(BLOCK // CHUNK):
            for hf in range(2):
                u_ref[blk * (BLOCK // CHUNK) + cidx, hf] = u[cidx * W + hf * LANES:cidx * W + (hf + 1) * LANES,
                                                             hf * LANES:(hf + 1) * LANES]
    o_ref[0] = o_intra
    hmask = (lax.broadcasted_iota(jnp.int32, (LANES, LANES), 0) // HEAD_DIM
             == lax.broadcasted_iota(jnp.int32, (LANES, LANES), 1) // HEAD_DIM)

    def step(n, carry):
        ci = n if forward else nch - 1 - n
        r0 = pl.multiple_of(ci * CHUNK, CHUNK)
        qd = qd_ref[pl.ds(r0, CHUNK), :]
        gam = gam_ref[pl.ds(r0, 1), :]
        for hf in range(2):
            st = st_ref[hf]
            o_ref[0, pl.ds(r0, CHUNK), hf * LANES:(hf + 1) * LANES] += _dot_nt(
                qd[:, hf * LANES:(hf + 1) * LANES], st.astype(BF16))
            st_ref[hf] = st * gam[:, hf * LANES:(hf + 1) * LANES] + jnp.where(hmask, u_ref[ci, hf], 0.0)
        return carry

    lax.fori_loop(0, nch, step, 0)


def _hgrn_kernel(qf_ref, zf_ref, vf_ref, qb_ref, zb_ref, vb_ref, lb_ref, bd_ref, of_ref, ob_ref,
                 st_ref, u_ref, qd_ref, gam_ref):
    j = pl.program_id(1)
    nt = pl.num_programs(1)

    @pl.when(j == 0)
    def _():
        st_ref[...] = jnp.zeros_like(st_ref)

    _hgrn_direction(True, j == 0, qf_ref, zf_ref, vf_ref, lb_ref.at[0], bd_ref, of_ref, st_ref.at[0], u_ref, qd_ref, gam_ref)
    _hgrn_direction(False, j == nt - 1, qb_ref, zb_ref, vb_ref, lb_ref.at[1], bd_ref, ob_ref, st_ref.at[1], u_ref, qd_ref, gam_ref)


def _hgrn(dproj, lb2, bd):
    bn, L, _ = dproj.shape
    nt = L // ROW_TILE
    fw = lambda c: pl.BlockSpec((1, ROW_TILE, MIX_W), lambda b, j: (b, j, c))
    bw = lambda c: pl.BlockSpec((1, ROW_TILE, MIX_W), lambda b, j: (b, nt - 1 - j, c))
    return pl.pallas_call(
        _hgrn_kernel,
        grid=(bn, nt),
        in_specs=[fw(0), fw(1), fw(3), bw(0), bw(2), bw(3), _const_spec((2, 1, MIX_W)), _const_spec((MIX_W, MIX_W))],
        out_specs=[pl.BlockSpec((1, ROW_TILE, MIX_W), lambda b, j: (b, j, 0)),
                   pl.BlockSpec((1, ROW_TILE, MIX_W), lambda b, j: (b, nt - 1 - j, 0))],
        out_shape=[jax.ShapeDtypeStruct((bn, L, MIX_W), F32)] * 2,
        scratch_shapes=[
            pltpu.VMEM((2, 2, LANES, LANES), F32),
            pltpu.VMEM((ROW_TILE // CHUNK, 2, LANES, LANES), F32),
            pltpu.VMEM((ROW_TILE, MIX_W), BF16),
            pltpu.VMEM((ROW_TILE, MIX_W), F32),
        ],
        compiler_params=_params(("parallel", "arbitrary")),
        name="hgrn",
    )(dproj, dproj, dproj, dproj, dproj, dproj, lb2, bd)


def _merge_kernel(h_ref, ya_ref, yb_ref, yc_ref, of_ref, ob_ref, dg_ref, g_ref, gout_ref, bd_ref,
                  wgz_ref, wb_ref, wo_ref, o_ref):
    h = h_ref[0]
    u = _rms(h, g_ref[...]).astype(BF16)
    od = of_ref[0] + ob_ref[0]
    ms = _split_dot(od * od, bd_ref[...]) * (1.0 / HEAD_DIM)
    yd = (od * lax.rsqrt(ms + RMS_EPS) * gout_ref[...] * jax.nn.silu(dg_ref[0])).astype(BF16)
    ys = (ya_ref[0], yb_ref[0], yc_ref[0], yd)
    merged = None
    for n in range(N_BRANCH):
        gate = jax.nn.sigmoid(_dot(u, wgz_ref[:, n * D_MODEL:(n + 1) * D_MODEL]))
        term = gate * _dot(ys[n], wb_ref[n])
        merged = term if merged is None else merged + term
    o_ref[0] = h + _dot(merged.astype(BF16), wo_ref[...])


def _merge(h, ya, yb, yc, of, ob, dproj, g, gout, bd, wgz, wb, wo):
    bn, L, _ = h.shape
    nt = L // ROW_TILE
    row = lambda w, c=0: pl.BlockSpec((1, ROW_TILE, w), lambda b, i: (b, i, c))
    return pl.pallas_call(
        _merge_kernel,
        grid=(bn, nt),
        in_specs=[row(D_MODEL), row(MIX_W), row(MIX_W), row(MIX_W), row(MIX_W), row(MIX_W), row(MIX_W, 4),
                  _const_spec((1, D_MODEL)), _const_spec((1, MIX_W)), _const_spec((MIX_W, MIX_W)),
                  _const_spec((D_MODEL, N_BRANCH * D_MODEL)), _const_spec((N_BRANCH, MIX_W, D_MODEL)),
                  _const_spec((D_MODEL, D_MODEL))],
        out_specs=row(D_MODEL),
        out_shape=jax.ShapeDtypeStruct((bn, L, D_MODEL), F32),
        compiler_params=_params(("parallel", "parallel")),
        name="merge",
    )(h, ya, yb, yc, of, ob, dproj, g, gout, bd, wgz, wb, wo)


def _ffn_kernel(h_ref, g_ref, wg_ref, wu_ref, wd_ref, o_ref):
    h = h_ref[0]
    u = _rms(h, g_ref[...]).astype(BF16)
    half = D_FF // 2
    out = h
    for c in range(2):
        a = _dot(u, wg_ref[:, c * half:(c + 1) * half])
        t = (jax.nn.silu(a) * _dot(u, wu_ref[:, c * half:(c + 1) * half])).astype(BF16)
        out = out + _dot(t, wd_ref[c * half:(c + 1) * half, :])
    o_ref[0] = out


def _ffn(h, g, wg, wu, wd):
    bn, L, _ = h.shape
    nt = L // ROW_TILE
    row = pl.BlockSpec((1, ROW_TILE, D_MODEL), lambda b, i: (b, i, 0))
    return pl.pallas_call(
        _ffn_kernel,
        grid=(bn, nt),
        in_specs=[row, _const_spec((1, D_MODEL)), _const_spec((D_MODEL, D_FF)), _const_spec((D_MODEL, D_FF)),
                  _const_spec((D_FF, D_MODEL))],
        out_specs=row,
        out_shape=jax.ShapeDtypeStruct((bn, L, D_MODEL), F32),
        compiler_params=_params(("parallel", "parallel")),
        name="ffn",
    )(h, g, wg, wu, wd)


def _final_kernel(h_ref, g_ref, o_ref):
    o_ref[0] = _rms(h_ref[0], g_ref[...])


def _final_norm(h, g):
    bn, L, _ = h.shape
    S = L - META_END
    return pl.pallas_call(
        _final_kernel,
        grid=(bn, S // BLOCK),
        in_specs=[pl.BlockSpec((1, BLOCK, D_MODEL), lambda b, i: (b, i + META_END // BLOCK, 0)), _const_spec((1, D_MODEL))],
        out_specs=pl.BlockSpec((1, BLOCK, D_MODEL), lambda b, i: (b, i, 0)),
        out_shape=jax.ShapeDtypeStruct((bn, S, D_MODEL), F32),
        compiler_params=_params(("parallel", "parallel")),
        name="final_norm",
    )(h, g)


def _t5_bucket(rel):
    half = N_BUCKETS // 2
    exact = half // 2
    n = jnp.abs(rel)
    nf = jnp.maximum(n, exact).astype(F32)
    big = exact + (jnp.log(nf / exact) / math.log(MAX_DIST / exact) * (half - exact)).astype(jnp.int32)
    big = jnp.minimum(big, half - 1)
    return jnp.where(rel > 0, half, 0) + jnp.where(n < exact, n, big)


def _lookup(table, bucket):
    hit = bucket[None, ..., None] == jnp.arange(N_BUCKETS)
    return jnp.sum(jnp.where(hit, table.T.reshape((N_HEADS,) + (1,) * bucket.ndim + (N_BUCKETS,)), 0.0), axis=-1)


def _bias_tables(rel_bias):
    bias_a = rel_bias[:, :N_HEADS].astype(F32) * LOG2E
    bias_c = rel_bias[:, N_HEADS:].astype(F32) * LOG2E
    r = jnp.arange(BLOCK)[:, None]
    rel_band = (jnp.arange(3 * BLOCK)[None, :] - BLOCK) - r
    bband = jnp.where(jnp.abs(rel_band)[None] <= WINDOW, _lookup(bias_a, _t5_bucket(rel_band)), NEG)
    lane = jnp.arange(BLOCK)[None, :]
    rel_m = lane[None] - (jnp.arange(3)[:, None, None] * BLOCK + r[None])
    bmeta = jnp.where((lane >= FRONT)[None, None], jnp.moveaxis(_lookup(bias_a, _t5_bucket(rel_m)), 0, 1), NEG)
    rel5 = (jnp.arange(5)[:, None, None] - 2) * BLOCK + lane[None] - r[None]
    bias5 = jnp.moveaxis(_lookup(bias_c, _t5_bucket(rel5)), 0, 1)
    return bband, bmeta, bias5


def _rope_layout():
    head = np.concatenate([np.zeros(32, int), np.ones(32, int)] * 2)
    return head


def _rope_cols(w, heads):
    lead = w.shape[:-1]
    w = w.reshape(lead + (heads // 2, 2, 2, 2, HEAD_DIM // 4))
    if heads == N_HEADS:
        w = jnp.moveaxis(w, (-5, -4, -3, -2), (-3, -5, -2, -4))
    else:
        w = jnp.moveaxis(w, (-4, -3, -2), (-3, -2, -4))
    return w.reshape(lead + (heads * HEAD_DIM,))


def _pair_cols(w):
    lead = w.shape[:-1]
    return jnp.swapaxes(w.reshape(lead + (N_KV, 2, HEAD_DIM)), -3, -2).reshape(lead + (N_HEADS * HEAD_DIM,))


def _rope_tables(L, S):
    rows = S // GRID_W
    row = jnp.concatenate([jnp.zeros((FRONT,), jnp.int32), -jnp.ones((N_META,), jnp.int32),
                           jnp.repeat(jnp.arange(rows, dtype=jnp.int32), GRID_W)]).astype(F32)
    col = jnp.concatenate([jnp.zeros((FRONT,), jnp.int32), jnp.arange(N_META, dtype=jnp.int32),
                           jnp.tile(jnp.arange(GRID_W, dtype=jnp.int32), rows)]).astype(F32)
    half = HEAD_DIM // 2
    inv = ROPE_THETA ** (-jnp.arange(0, half, 2, dtype=F32) / half)
    ang = jnp.concatenate([row[:, None] * inv[None, :], col[:, None] * inv[None, :]], axis=-1)
    cos_t = jnp.tile(jnp.cos(ang), (1, 4))
    sin_t = jnp.concatenate([-jnp.tile(jnp.sin(ang), (1, 2)), jnp.tile(jnp.sin(ang), (1, 2))], axis=-1)
    return cos_t, sin_t


def _rope_gain(g):
    g = jnp.swapaxes(g.astype(F32).reshape(2, 2, HEAD_DIM // 4), 0, 1).reshape(2, 1, HEAD_DIM // 2)
    return jnp.broadcast_to(g, (2, 2, HEAD_DIM // 2)).reshape(LANES)


def _qkv_weight(wl):
    a, b, c = wl[:, 0:512], wl[:, 512:1024], wl[:, 1024:1536]
    return jnp.concatenate([_pair_cols(a[:, :256]), a[:, 256:],
                            _rope_cols(b[:, :256], N_HEADS), _rope_cols(b[:, 256:384], N_KV), b[:, 384:],
                            _pair_cols(c[:, :256]), c[:, 256:]], axis=1)


def kernel(x, meta_tokens, rel_bias, hgrn_lb_logits, ln_mix, w_in, attn_sink, qk_norm_q, qk_norm_k, diff_lambda,
           diff_subnorm, hgrn_out_norm, w_branch, w_out, ln_ffn, w_ffn_gate, w_ffn_up, w_ffn_down, ln_final):
    bn, S, _ = x.shape
    L = META_END + S
    depth = w_in.shape[0]
    assert L % ROW_TILE == 0 and S % GRID_W == 0
    h = jnp.concatenate([jnp.zeros((bn, FRONT, D_MODEL), x.dtype),
                         jnp.broadcast_to(meta_tokens.astype(x.dtype)[None], (bn, N_META, D_MODEL)), x], axis=1)
    bband, bmeta, bias5 = _bias_tables(rel_bias)
    cos_t, sin_t = _rope_tables(L, S)
    rhead = _rope_layout()
    lane = np.arange(LANES)
    hsel_rope = jnp.asarray(rhead[:, None] == rhead[None, :], BF16)
    hsel_half = jnp.asarray((lane[:, None] // HEAD_DIM) == (lane[None, :] // HEAD_DIM), BF16)
    w256 = np.arange(MIX_W)
    bd = jnp.asarray((w256[:, None] // HEAD_DIM) == (w256[None, :] // HEAD_DIM), BF16)
    cs = np.ones((1, QKV_W), np.float32)
    cs[0, 0:256] = HEAD_DIM ** -0.5 * LOG2E
    cs[0, 1024:1280] = C_DIM ** -0.5 * LOG2E
    cs = jnp.asarray(cs)
    lb_p = jax.nn.softmax(hgrn_lb_logits.astype(F32), axis=1)
    lb_all = jnp.cumsum(lb_p, axis=1) - lb_p[:, :1]

    for l in range(depth):
        wl = w_in[l]
        wqkv = _qkv_weight(wl).astype(BF16)
        wd = wl[:, QKV_W:QKV_W + D_W].astype(BF16)
        wgz = wl[:, QKV_W + D_W:].astype(BF16)
        gn = jnp.stack([_rope_gain(qk_norm_q[l]) * (HEAD_DIM ** -0.5 * LOG2E)] * 2 + [_rope_gain(qk_norm_k[l])])
        qkv, dproj = _in_proj(h, ln_mix[l][None], wqkv, wd, cs, cos_t, sin_t, gn, hsel_rope)
        ya = _attn_a(qkv, attn_sink[l].astype(F32) * LOG2E, bband, bmeta)
        yb = _attn_b(qkv)
        lam_init = 0.8 - 0.6 * math.exp(-0.3 * l)
        lam_p = diff_lambda[l].astype(F32)
        lam = jnp.exp(jnp.sum(lam_p[0] * lam_p[1])) - jnp.exp(jnp.sum(lam_p[2] * lam_p[3])) + lam_init
        scal = jnp.stack([lam, jnp.asarray(1.0 - lam_init, F32)])
        gsub = jnp.tile(diff_subnorm[l].astype(F32), 2)[None]
        yc = _attn_c(qkv, scal, bias5, gsub, hsel_half)
        lb2 = lb_all[:, l][:, None, :]
        of, ob = _hgrn(dproj, lb2, bd)
        wb = w_branch[l].astype(BF16)
        wb_att = jnp.swapaxes(wb[:3].reshape(3, N_KV, 2, HEAD_DIM, D_MODEL), 1, 2).reshape(3, MIX_W, D_MODEL)
        wb = jnp.concatenate([wb_att, wb[3:]], axis=0)
        gout = jnp.tile(hgrn_out_norm[l].astype(F32), N_HEADS)[None]
        h = _merge(h, ya, yb, yc, of, ob, dproj, ln_mix[l][None], gout, bd, wgz, wb, w_out[l].astype(BF16))
        h = _ffn(h, ln_ffn[l][None], w_ffn_gate[l].astype(BF16), w_ffn_up[l].astype(BF16), w_ffn_down[l].astype(BF16))
    return _final_norm(h, ln_final[None])
```

```python
import math

import jax
import jax.numpy as jnp
import numpy as np
from jax import lax
from jax.experimental import pallas as pl
from jax.experimental.pallas import tpu as pltpu

D_MODEL = 1024
N_BRANCH = 4
MIX_W = D_MODEL // N_BRANCH
HEAD_DIM = 64
BLOCK = 128
WINDOW = 128
ROPE_THETA = 10000.0
GRID_W = 64
N_HEADS = 4
N_KV = 2
C_DIM = 32
N_META = 16
FRONT = (-N_META) % BLOCK
META_END = FRONT + N_META
N_BUCKETS = 32
MAX_DIST = 128
D_FF = -(-(8 * D_MODEL) // (3 * 256)) * 256
RMS_EPS = 1e-6
NEG = -1e30
F_FLOOR = 1e-30
LOG2E = 1.4426950408889634
LANES = 128
CHUNK = 16
ROW_TILE = 384
C_UNROLL = 4
QKV_W = 3 * 512
D_W = 5 * MIX_W
VMEM_LIMIT = 56 * 1024 * 1024

F32 = jnp.float32
BF16 = jnp.bfloat16


def _dot(a, b):
    return jnp.dot(a, b, preferred_element_type=F32)


def _dot_nt(a, b):
    return lax.dot_general(a, b, (((1,), (1,)), ((), ())), preferred_element_type=F32)


def _split_dot(x, sel):
    hi = x.astype(BF16)
    lo = (x - hi.astype(F32)).astype(BF16)
    return _dot(hi, sel) + _dot(lo, sel)


def _rms(x, g):
    return x * lax.rsqrt(jnp.mean(x * x, axis=-1, keepdims=True) + RMS_EPS) * g


def _params(sem):
    return pltpu.CompilerParams(dimension_semantics=sem, vmem_limit_bytes=VMEM_LIMIT)


def _const_spec(shape):
    nd = len(shape)
    return pl.BlockSpec(shape, lambda *_: (0,) * nd)


def _in_proj_kernel(h_ref, g_ref, wqkv_ref, wd_ref, cs_ref, cos_ref, sin_ref, gn_ref, hsel_ref,
                    qkv_ref, d_ref, vtb_ref, vtc_ref):
    u = _rms(h_ref[0], g_ref[...]).astype(BF16)
    acc = _dot(u, wqkv_ref[...]) * cs_ref[...]
    qkv_ref[0, :, 0:512] = acc[:, 0:512].astype(BF16)
    for t in range(3):
        c0 = 512 + LANES * t
        xt = acc[:, c0:c0 + LANES]
        ms = _split_dot(xt * xt, hsel_ref[...]) * (1.0 / HEAD_DIM)
        xt = xt * lax.rsqrt(ms + RMS_EPS) * gn_ref[t:t + 1, :]
        xt = xt * cos_ref[...] + pltpu.roll(xt, LANES // 2, 1) * sin_ref[...]
        qkv_ref[0, :, c0:c0 + LANES] = xt.astype(BF16)
    qkv_ref[0, :, 896:QKV_W] = acc[:, 896:QKV_W].astype(BF16)
    vtb_ref[0] = acc[:, 896:1024].T.astype(BF16)
    vtc_ref[0] = acc[:, 1408:QKV_W].T.astype(BF16)
    d_ref[0] = _dot(u, wd_ref[...])


def _in_proj(h, g, wqkv, wd, cs, cos_t, sin_t, gn, hsel):
    bn, L, _ = h.shape
    nt = L // ROW_TILE
    return pl.pallas_call(
        _in_proj_kernel,
        grid=(bn, nt),
        in_specs=[
            pl.BlockSpec((1, ROW_TILE, D_MODEL), lambda b, i: (b, i, 0)),
            _const_spec((1, D_MODEL)),
            _const_spec((D_MODEL, QKV_W)),
            _const_spec((D_MODEL, D_W)),
            _const_spec((1, QKV_W)),
            pl.BlockSpec((ROW_TILE, LANES), lambda b, i: (i, 0)),
            pl.BlockSpec((ROW_TILE, LANES), lambda b, i: (i, 0)),
            _const_spec((3, LANES)),
            _const_spec((LANES, LANES)),
        ],
        out_specs=[
            pl.BlockSpec((1, ROW_TILE, QKV_W), lambda b, i: (b, i, 0)),
            pl.BlockSpec((1, ROW_TILE, D_W), lambda b, i: (b, i, 0)),
            pl.BlockSpec((1, LANES, ROW_TILE), lambda b, i: (b, 0, i)),
            pl.BlockSpec((1, LANES, ROW_TILE), lambda b, i: (b, 0, i)),
        ],
        out_shape=[jax.ShapeDtypeStruct((bn, L, QKV_W), BF16), jax.ShapeDtypeStruct((bn, L, D_W), F32),
                   jax.ShapeDtypeStruct((bn, LANES, L), BF16), jax.ShapeDtypeStruct((bn, LANES, L), BF16)],
        compiler_params=_params(("parallel", "parallel")),
        name="in_proj",
    )(h, g, wqkv, wd, cs, cos_t, sin_t, gn, hsel)


def _stack_q(q_ref, masks):
    parts = []
    for t in range(2):
        qt = q_ref[0, :, t * LANES:(t + 1) * LANES]
        for m in masks:
            parts.append(jnp.where(m, qt, jnp.zeros_like(qt)))
    return jnp.concatenate(parts, axis=0)


def _blk(ref, j):
    if isinstance(j, int):
        return ref[0, j * BLOCK:(j + 1) * BLOCK, :]
    return ref[0, pl.ds(pl.multiple_of(j * BLOCK, BLOCK), BLOCK), :]


def _attn_a_kernel(sink_ref, q_ref, k_ref, v_ref, bband_ref, bmeta_ref, o_ref):
    i = pl.program_id(1)
    nb = pl.num_programs(1)
    prev = jnp.maximum(i - 1, 0)
    nxt = jnp.minimum(i + 1, nb - 1)
    kband = jnp.concatenate([_blk(k_ref, prev), _blk(k_ref, i), _blk(k_ref, nxt)], axis=0)
    vband = jnp.concatenate([_blk(v_ref, prev), _blk(v_ref, i), _blk(v_ref, nxt)], axis=0)
    kmeta = k_ref[0, 0:BLOCK, :]
    vmeta = v_ref[0, 0:BLOCK, :]
    col = lax.broadcasted_iota(jnp.int32, (1, 3 * BLOCK), 1)
    lo = jnp.where(i == 0, 2 * BLOCK, jnp.where(i == 1, BLOCK, 0))
    hi = jnp.where(i == nb - 1, 2 * BLOCK, 3 * BLOCK)
    dead = (col < lo) | (col >= hi)
    lane = lax.broadcasted_iota(jnp.int32, (1, LANES), 1)
    outs = []
    for kv in range(N_KV):
        qs = _stack_q(q_ref, [(lane >= HEAD_DIM) == bool(kv)])
        bb = jnp.concatenate([bband_ref[2 * kv], bband_ref[2 * kv + 1]], axis=0)
        bm = jnp.concatenate([bmeta_ref[0, 2 * kv], bmeta_ref[0, 2 * kv + 1]], axis=0)
        sb = jnp.where(dead, NEG, _dot_nt(qs, kband) + bb)
        sm = _dot_nt(qs, kmeta) + bm
        sk = jnp.concatenate([jnp.full((BLOCK, 1), sink_ref[2 * kv], F32),
                              jnp.full((BLOCK, 1), sink_ref[2 * kv + 1], F32)], axis=0)
        m = jnp.maximum(jnp.maximum(jnp.max(sb, axis=-1, keepdims=True), jnp.max(sm, axis=-1, keepdims=True)), sk)
        pb = jnp.exp2(sb - m)
        pm = jnp.exp2(sm - m)
        den = jnp.sum(pb, axis=-1, keepdims=True) + jnp.sum(pm, axis=-1, keepdims=True) + jnp.exp2(sk - m)
        outs.append((_dot(pb.astype(BF16), vband) + _dot(pm.astype(BF16), vmeta)) / den)
    for t in range(2):
        y = jnp.where(lane < HEAD_DIM, outs[0][t * BLOCK:(t + 1) * BLOCK], outs[1][t * BLOCK:(t + 1) * BLOCK])
        o_ref[0, :, t * LANES:(t + 1) * LANES] = y.astype(BF16)


def _attn_a(qkv, sink, bband, bmeta):
    bn, L, _ = qkv.shape
    nb = L // BLOCK
    return pl.pallas_call(
        _attn_a_kernel,
        grid=(bn, nb),
        in_specs=[
            pl.BlockSpec(memory_space=pltpu.SMEM),
            pl.BlockSpec((1, BLOCK, 2 * LANES), lambda b, i: (b, i, 0)),
            pl.BlockSpec((1, L, LANES), lambda b, i: (b, 0, 2)),
            pl.BlockSpec((1, L, LANES), lambda b, i: (b, 0, 3)),
            _const_spec((N_HEADS, BLOCK, 3 * BLOCK)),
            pl.BlockSpec((1, N_HEADS, BLOCK, BLOCK), lambda b, i: (jnp.minimum(i, 2), 0, 0, 0)),
        ],
        out_specs=pl.BlockSpec((1, BLOCK, MIX_W), lambda b, i: (b, i, 0)),
        out_shape=jax.ShapeDtypeStruct((bn, L, MIX_W), BF16),
        compiler_params=_params(("parallel", "parallel")),
        name="attn_window",
    )(sink, qkv, qkv, qkv, bband, bmeta)


def _stack_maps(q_ref, masks_per_kv):
    parts = []
    for masks in masks_per_kv:
        for t in range(2):
            qt = q_ref[0, :, t * LANES:(t + 1) * LANES]
            for m in masks:
                parts.append(jnp.where(m, qt, jnp.zeros_like(qt)))
    return jnp.concatenate(parts, axis=0)


def _attn_sweep(qs_ref, k_ref, vt_ref, bias_fn, sa_ref, sb_ref, m_ref, acc_ref):
    nb = k_ref.shape[1] // BLOCK
    W = qs_ref.shape[0]
    half = W // 2
    ngrp = (nb - 1) // C_UNROLL
    keyrow = lax.broadcasted_iota(jnp.int32, (BLOCK, 1), 0)

    def rows(ref, j0, width):
        if isinstance(j0, int):
            return ref[0, j0 * BLOCK:(j0 + width) * BLOCK, :]
        return ref[0, pl.ds(pl.multiple_of(j0 * BLOCK, BLOCK), width * BLOCK), :]

    def cols(ref, j0, width):
        if isinstance(j0, int):
            return ref[0, :, j0 * BLOCK:(j0 + width) * BLOCK]
        return ref[0, :, pl.ds(pl.multiple_of(j0 * BLOCK, BLOCK), width * BLOCK)]

    def scores(j0, width):
        s_all = _dot_nt(rows(k_ref, j0, width), qs_ref[...])
        parts = []
        for u in range(width):
            s = s_all[u * BLOCK:(u + 1) * BLOCK]
            b = bias_fn(j0 + u)
            if b is not None:
                s = s + b
            if isinstance(j0, int) and j0 + u == 0:
                s = jnp.where(keyrow >= FRONT, s, NEG)
            parts.append(s)
        return parts[0] if width == 1 else jnp.concatenate(parts, axis=0)

    def absorb(s, j0, width, first=False):
        mg = jnp.max(jnp.max(s.reshape(width * BLOCK // 8, 8, W), axis=0), axis=0, keepdims=True)
        m_new = mg if first else jnp.maximum(m_ref[...], mg)
        p = jnp.exp2(s - m_new).astype(BF16)
        vt = cols(vt_ref, j0, width)
        ones = jnp.ones((16, width * BLOCK), BF16)
        if not first:
            alpha = jnp.exp2(m_ref[...] - m_new)
        for kv in range(N_KV):
            w = _dot(jnp.concatenate([vt[kv * HEAD_DIM:(kv + 1) * HEAD_DIM], ones], axis=0),
                     p[:, kv * half:(kv + 1) * half])
            acc_ref[kv] = w if first else acc_ref[kv] * alpha[:, kv * half:(kv + 1) * half] + w
        m_ref[...] = m_new

    def group(n):
        return 1 + n * C_UNROLL

    absorb(scores(0, 1), 0, 1, first=True)
    if ngrp > 0:
        sa_ref[...] = scores(group(0), C_UNROLL)
        npair = (ngrp - 1) // 2

        def body(n, carry):
            g = 2 * n
            sb_ref[...] = scores(group(g + 1), C_UNROLL)
            absorb(sa_ref[...], group(g), C_UNROLL)
            sa_ref[...] = scores(group(g + 2), C_UNROLL)
            absorb(sb_ref[...], group(g + 1), C_UNROLL)
            return carry

        lax.fori_loop(0, npair, body, 0)
        g = 2 * npair
        if ngrp - g == 2:
            sb_ref[...] = scores(group(g + 1), C_UNROLL)
            absorb(sa_ref[...], group(g), C_UNROLL)
            absorb(sb_ref[...], group(g + 1), C_UNROLL)
        else:
            absorb(sa_ref[...], group(g), C_UNROLL)
    for j in range(1 + ngrp * C_UNROLL, nb):
        absorb(scores(j, 1), j, 1)


def _sweep_scratch(nmaps):
    W = nmaps * BLOCK
    return [pltpu.VMEM((W, LANES), BF16), pltpu.VMEM((C_UNROLL * BLOCK, W), F32), pltpu.VMEM((C_UNROLL * BLOCK, W), F32),
            pltpu.VMEM((1, W), F32), pltpu.VMEM((N_KV, HEAD_DIM + 16, W // 2), F32)]


def _attn_b_kernel(q_ref, k_ref, vt_ref, o_ref, qs_ref, sa_ref, sb_ref, m_ref, acc_ref):
    lane = lax.broadcasted_iota(jnp.int32, (1, LANES), 1)
    head_b = (lane // (HEAD_DIM // 2)) % 2
    qs_ref[...] = _stack_maps(q_ref, [[head_b == kv] for kv in range(N_KV)])
    _attn_sweep(qs_ref, k_ref, vt_ref, lambda j: None, sa_ref, sb_ref, m_ref, acc_ref)
    ot = [acc_ref[kv][0:HEAD_DIM] / acc_ref[kv][HEAD_DIM:HEAD_DIM + 1] for kv in range(N_KV)]
    for g in range(2):
        yt = jnp.concatenate([ot[kv][:, g * BLOCK:(g + 1) * BLOCK] for kv in range(N_KV)], axis=0)
        o_ref[0, :, g * LANES:(g + 1) * LANES] = yt.T.astype(BF16)


def _attn_b(qkv, vt):
    bn, L, _ = qkv.shape
    nb = L // BLOCK
    return pl.pallas_call(
        _attn_b_kernel,
        grid=(bn, nb),
        in_specs=[
            pl.BlockSpec((1, BLOCK, 2 * LANES), lambda b, i: (b, i, 2)),
            pl.BlockSpec((1, L, LANES), lambda b, i: (b, 0, 6)),
            pl.BlockSpec((1, LANES, L), lambda b, i: (b, 0, 0)),
        ],
        out_specs=pl.BlockSpec((1, BLOCK, MIX_W), lambda b, i: (b, i, 0)),
        out_shape=jax.ShapeDtypeStruct((bn, L, MIX_W), BF16),
        scratch_shapes=_sweep_scratch(4),
        compiler_params=_params(("parallel", "parallel")),
        name="attn_rope",
    )(qkv, qkv, vt)


def _attn_c_kernel(sc_ref, q_ref, k_ref, vt_ref, bias_ref, gsub_ref, hsel_ref, o_ref,
                   qs_ref, sa_ref, sb_ref, m_ref, acc_ref):
    i = pl.program_id(1)
    lam = sc_ref[0]
    lane = lax.broadcasted_iota(jnp.int32, (1, LANES), 1)
    grp = lane // C_DIM
    qs_ref[...] = _stack_maps(q_ref, [[grp == 2 * kv, grp == 2 * kv + 1] for kv in range(N_KV)])

    def bias(j):
        t = jnp.clip(j - i + 2, 0, 4)
        tiles = [bias_ref[t, h] for h in range(N_HEADS)]
        return jnp.concatenate([tiles[h] for h in range(N_HEADS) for _ in range(2)], axis=1)

    _attn_sweep(qs_ref, k_ref, vt_ref, bias, sa_ref, sb_ref, m_ref, acc_ref)
    ot = [acc_ref[kv][0:HEAD_DIM] / acc_ref[kv][HEAD_DIM:HEAD_DIM + 1] for kv in range(N_KV)]
    for g in range(2):
        dt = [ot[kv][:, (2 * g) * BLOCK:(2 * g + 1) * BLOCK] - lam * ot[kv][:, (2 * g + 1) * BLOCK:(2 * g + 2) * BLOCK]
              for kv in range(N_KV)]
        y = jnp.concatenate(dt, axis=0).T
        ms = _split_dot(y * y, hsel_ref[...]) * (1.0 / HEAD_DIM)
        y = y * lax.rsqrt(ms + RMS_EPS) * gsub_ref[...] * sc_ref[1]
        o_ref[0, :, g * LANES:(g + 1) * LANES] = y.astype(BF16)


def _attn_c(qkv, vt, scal, bias5t, gsub, hsel):
    bn, L, _ = qkv.shape
    nb = L // BLOCK
    return pl.pallas_call(
        _attn_c_kernel,
        grid=(bn, nb),
        in_specs=[
            pl.BlockSpec(memory_space=pltpu.SMEM),
            pl.BlockSpec((1, BLOCK, 2 * LANES), lambda b, i: (b, i, 4)),
            pl.BlockSpec((1, L, LANES), lambda b, i: (b, 0, 10)),
            pl.BlockSpec((1, LANES, L), lambda b, i: (b, 0, 0)),
            _const_spec((5, N_HEADS, BLOCK, BLOCK)),
            _const_spec((1, LANES)),
            _const_spec((LANES, LANES)),
        ],
        out_specs=pl.BlockSpec((1, BLOCK, MIX_W), lambda b, i: (b, i, 0)),
        out_shape=jax.ShapeDtypeStruct((bn, L, MIX_W), BF16),
        scratch_shapes=_sweep_scratch(8),
        compiler_params=_params(("parallel", "parallel")),
        name="attn_diff",
    )(scal, qkv, qkv, vt, bias5t, gsub, hsel)


def _hgrn_direction(forward, first_tile, q_ref, z_ref, v_ref, lb_ref, bd_ref, o_ref, st_ref, u_ref, qd_ref, gam_ref):
    R = q_ref.shape[1]
    W = MIX_W
    nch = R // CHUNK
    z = z_ref[0]
    q = q_ref[0] * (HEAD_DIM ** -0.5)
    v = v_ref[0]
    lb = lb_ref[...]
    row = lax.broadcasted_iota(jnp.int32, (R, 1), 0)
    rin = row % CHUNK
    f = lb + (1.0 - lb) * jax.nn.sigmoid(z)
    g = jnp.log(jnp.maximum(f, F_FLOOR))
    kk = (1.0 - lb) * jax.nn.sigmoid(-z)
    kk = jnp.where(row < jnp.where(first_tile, FRONT, 0), 0.0, kk)
    b, c = g, g
    for sh in (1, 2, 4, 8):
        b = b + jnp.where(rin >= sh, pltpu.roll(b, sh, 0), 0.0)
        c = c + jnp.where(rin + sh < CHUNK, pltpu.roll(c, R - sh, 0), 0.0)
    x, e = (b, c - g) if forward else (c, b - g)
    qd_ref[...] = (q * jnp.exp(x)).astype(BF16)
    kd = kk * jnp.exp(e)
    gam_ref[...] = jnp.exp(b + c - g)
    o_intra = jnp.zeros((R, W), F32)
    for d in range(CHUNK):
        sh = d if forward else (R - d) % R
        ok = (rin >= d) if forward else (rin + d < CHUNK)
        xs, ks, vs = (x, kk, v) if d == 0 else (pltpu.roll(x, sh, 0), pltpu.roll(kk, sh, 0), pltpu.roll(v, sh, 0))
        e_d = jnp.exp(jnp.where(ok, x - xs, NEG)) * q * ks
        o_intra = o_intra + _dot(e_d.astype(BF16), bd_ref[...]) * vs
    lane = lax.broadcasted_iota(jnp.int32, (1, BLOCK), 1)
    vt = v.T.astype(BF16)
    kdb = kd.astype(BF16)
    for blk in range(R // BLOCK):
        vt_b = vt[:, blk * BLOCK:(blk + 1) * BLOCK]
        lhs = jnp.concatenate([jnp.where(lane // CHUNK == cidx, vt_b, jnp.zeros_like(vt_b))
                               for cidx in range(BLOCK // CHUNK)], axis=0)
        u = _dot(lhs, kdb[blk * BLOCK:(blk + 1) * BLOCK, :])
        for cidx in range(BLOCK // CHUNK):
            for hf in range(2):
                u_ref[blk * (BLOCK // CHUNK) + cidx, hf] = u[cidx * W + hf * LANES:cidx * W + (hf + 1) * LANES,
                                                             hf * LANES:(hf + 1) * LANES]
    o_ref[0] = o_intra
    hmask = (lax.broadcasted_iota(jnp.int32, (LANES, LANES), 0) // HEAD_DIM
             == lax.broadcasted_iota(jnp.int32, (LANES, LANES), 1) // HEAD_DIM)

    def step(n, carry):
        ci = n if forward else nch - 1 - n
        r0 = pl.multiple_of(ci * CHUNK, CHUNK)
        qd = qd_ref[pl.ds(r0, CHUNK), :]
        gam = gam_ref[pl.ds(r0, 1), :]
        for hf in range(2):
            st = st_ref[hf]
            o_ref[0, pl.ds(r0, CHUNK), hf * LANES:(hf + 1) * LANES] += _dot_nt(
                qd[:, hf * LANES:(hf + 1) * LANES], st.astype(BF16))
            st_ref[hf] = st * gam[:, hf * LANES:(hf + 1) * LANES] + jnp.where(hmask, u_ref[ci, hf], 0.0)
        return carry

    lax.fori_loop(0, nch, step, 0)


def _hgrn_kernel(qf_ref, zf_ref, vf_ref, qb_ref, zb_ref, vb_ref, lb_ref, bd_ref, of_ref, ob_ref,
                 st_ref, u_ref, qd_ref, gam_ref):
    j = pl.program_id(1)
    nt = pl.num_programs(1)

    @pl.when(j == 0)
    def _():
        st_ref[...] = jnp.zeros_like(st_ref)

    _hgrn_direction(True, j == 0, qf_ref, zf_ref, vf_ref, lb_ref.at[0], bd_ref, of_ref, st_ref.at[0], u_ref, qd_ref, gam_ref)
    _hgrn_direction(False, j == nt - 1, qb_ref, zb_ref, vb_ref, lb_ref.at[1], bd_ref, ob_ref, st_ref.at[1], u_ref, qd_ref, gam_ref)


def _hgrn(dproj, lb2, bd):
    bn, L, _ = dproj.shape
    nt = L // ROW_TILE
    fw = lambda c: pl.BlockSpec((1, ROW_TILE, MIX_W), lambda b, j: (b, j, c))
    bw = lambda c: pl.BlockSpec((1, ROW_TILE, MIX_W), lambda b, j: (b, nt - 1 - j, c))
    return pl.pallas_call(
        _hgrn_kernel,
        grid=(bn, nt),
        in_specs=[fw(0), fw(1), fw(3), bw(0), bw(2), bw(3), _const_spec((2, 1, MIX_W)), _const_spec((MIX_W, MIX_W))],
        out_specs=[pl.BlockSpec((1, ROW_TILE, MIX_W), lambda b, j: (b, j, 0)),
                   pl.BlockSpec((1, ROW_TILE, MIX_W), lambda b, j: (b, nt - 1 - j, 0))],
        out_shape=[jax.ShapeDtypeStruct((bn, L, MIX_W), F32)] * 2,
        scratch_shapes=[
            pltpu.VMEM((2, 2, LANES, LANES), F32),
            pltpu.VMEM((ROW_TILE // CHUNK, 2, LANES, LANES), F32),
            pltpu.VMEM((ROW_TILE, MIX_W), BF16),
            pltpu.VMEM((ROW_TILE, MIX_W), F32),
        ],
        compiler_params=_params(("parallel", "arbitrary")),
        name="hgrn",
    )(dproj, dproj, dproj, dproj, dproj, dproj, lb2, bd)


def _merge_kernel(h_ref, ya_ref, yb_ref, yc_ref, of_ref, ob_ref, dg_ref, g_ref, gout_ref, bd_ref,
                  wgz_ref, wb_ref, wo_ref, o_ref):
    h = h_ref[0]
    u = _rms(h, g_ref[...]).astype(BF16)
    od = of_ref[0] + ob_ref[0]
    ms = _split_dot(od * od, bd_ref[...]) * (1.0 / HEAD_DIM)
    yd = (od * lax.rsqrt(ms + RMS_EPS) * gout_ref[...] * jax.nn.silu(dg_ref[0])).astype(BF16)
    ys = (ya_ref[0], yb_ref[0], yc_ref[0], yd)
    merged = None
    for n in range(N_BRANCH):
        gate = jax.nn.sigmoid(_dot(u, wgz_ref[:, n * D_MODEL:(n + 1) * D_MODEL]))
        term = gate * _dot(ys[n], wb_ref[n])
        merged = term if merged is None else merged + term
    o_ref[0] = h + _dot(merged.astype(BF16), wo_ref[...])


def _merge(h, ya, yb, yc, of, ob, dproj, g, gout, bd, wgz, wb, wo):
    bn, L, _ = h.shape
    nt = L // ROW_TILE
    row = lambda w, c=0: pl.BlockSpec((1, ROW_TILE, w), lambda b, i: (b, i, c))
    return pl.pallas_call(
        _merge_kernel,
        grid=(bn, nt),
        in_specs=[row(D_MODEL), row(MIX_W), row(MIX_W), row(MIX_W), row(MIX_W), row(MIX_W), row(MIX_W, 4),
                  _const_spec((1, D_MODEL)), _const_spec((1, MIX_W)), _const_spec((MIX_W, MIX_W)),
                  _const_spec((D_MODEL, N_BRANCH * D_MODEL)), _const_spec((N_BRANCH, MIX_W, D_MODEL)),
                  _const_spec((D_MODEL, D_MODEL))],
        out_specs=row(D_MODEL),
        out_shape=jax.ShapeDtypeStruct((bn, L, D_MODEL), F32),
        compiler_params=_params(("parallel", "parallel")),
        name="merge",
    )(h, ya, yb, yc, of, ob, dproj, g, gout, bd, wgz, wb, wo)


def _ffn_kernel(h_ref, g_ref, wg_ref, wu_ref, wd_ref, o_ref):
    h = h_ref[0]
    u = _rms(h, g_ref[...]).astype(BF16)
    half = D_FF // 2
    out = h
    for c in range(2):
        a = _dot(u, wg_ref[:, c * half:(c + 1) * half])
        t = (jax.nn.silu(a) * _dot(u, wu_ref[:, c * half:(c + 1) * half])).astype(BF16)
        out = out + _dot(t, wd_ref[c * half:(c + 1) * half, :])
    o_ref[0] = out


def _ffn(h, g, wg, wu, wd):
    bn, L, _ = h.shape
    nt = L // ROW_TILE
    row = pl.BlockSpec((1, ROW_TILE, D_MODEL), lambda b, i: (b, i, 0))
    return pl.pallas_call(
        _ffn_kernel,
        grid=(bn, nt),
        in_specs=[row, _const_spec((1, D_MODEL)), _const_spec((D_MODEL, D_FF)), _const_spec((D_MODEL, D_FF)),
                  _const_spec((D_FF, D_MODEL))],
        out_specs=row,
        out_shape=jax.ShapeDtypeStruct((bn, L, D_MODEL), F32),
        compiler_params=_params(("parallel", "parallel")),
        name="ffn",
    )(h, g, wg, wu, wd)


def _final_kernel(h_ref, g_ref, o_ref):
    o_ref[0] = _rms(h_ref[0], g_ref[...])


def _final_norm(h, g):
    bn, L, _ = h.shape
    S = L - META_END
    return pl.pallas_call(
        _final_kernel,
        grid=(bn, S // BLOCK),
        in_specs=[pl.BlockSpec((1, BLOCK, D_MODEL), lambda b, i: (b, i + META_END // BLOCK, 0)), _const_spec((1, D_MODEL))],
        out_specs=pl.BlockSpec((1, BLOCK, D_MODEL), lambda b, i: (b, i, 0)),
        out_shape=jax.ShapeDtypeStruct((bn, S, D_MODEL), F32),
        compiler_params=_params(("parallel", "parallel")),
        name="final_norm",
    )(h, g)


def _t5_bucket(rel):
    half = N_BUCKETS // 2
    exact = half // 2
    n = jnp.abs(rel)
    nf = jnp.maximum(n, exact).astype(F32)
    big = exact + (jnp.log(nf / exact) / math.log(MAX_DIST / exact) * (half - exact)).astype(jnp.int32)
    big = jnp.minimum(big, half - 1)
    return jnp.where(rel > 0, half, 0) + jnp.where(n < exact, n, big)


def _lookup(table, bucket):
    hit = bucket[None, ..., None] == jnp.arange(N_BUCKETS)
    return jnp.sum(jnp.where(hit, table.T.reshape((N_HEADS,) + (1,) * bucket.ndim + (N_BUCKETS,)), 0.0), axis=-1)


def _bias_tables(rel_bias):
    bias_a = rel_bias[:, :N_HEADS].astype(F32) * LOG2E
    bias_c = rel_bias[:, N_HEADS:].astype(F32) * LOG2E
    r = jnp.arange(BLOCK)[:, None]
    rel_band = (jnp.arange(3 * BLOCK)[None, :] - BLOCK) - r
    bband = jnp.where(jnp.abs(rel_band)[None] <= WINDOW, _lookup(bias_a, _t5_bucket(rel_band)), NEG)
    lane = jnp.arange(BLOCK)[None, :]
    rel_m = lane[None] - (jnp.arange(3)[:, None, None] * BLOCK + r[None])
    bmeta = jnp.where((lane >= FRONT)[None, None], jnp.moveaxis(_lookup(bias_a, _t5_bucket(rel_m)), 0, 1), NEG)
    rel5t = (jnp.arange(5)[:, None, None] - 2) * BLOCK + r[None] - lane[None]
    bias5t = jnp.moveaxis(_lookup(bias_c, _t5_bucket(rel5t)), 0, 1)
    return bband, bmeta, bias5t


def _rope_layout():
    return np.concatenate([np.zeros(32, int), np.ones(32, int)] * 2)


def _rope_cols(w, heads):
    lead = w.shape[:-1]
    w = w.reshape(lead + (heads // 2, 2, 2, 2, HEAD_DIM // 4))
    if heads == N_HEADS:
        w = jnp.moveaxis(w, (-5, -4, -3, -2), (-3, -5, -2, -4))
    else:
        w = jnp.moveaxis(w, (-4, -3, -2), (-3, -2, -4))
    return w.reshape(lead + (heads * HEAD_DIM,))


def _pair_cols(w):
    lead = w.shape[:-1]
    return jnp.swapaxes(w.reshape(lead + (N_KV, 2, HEAD_DIM)), -3, -2).reshape(lead + (N_HEADS * HEAD_DIM,))


def _rope_tables(L, S):
    rows = S // GRID_W
    row = jnp.concatenate([jnp.zeros((FRONT,), jnp.int32), -jnp.ones((N_META,), jnp.int32),
                           jnp.repeat(jnp.arange(rows, dtype=jnp.int32), GRID_W)]).astype(F32)
    col = jnp.concatenate([jnp.zeros((FRONT,), jnp.int32), jnp.arange(N_META, dtype=jnp.int32),
                           jnp.tile(jnp.arange(GRID_W, dtype=jnp.int32), rows)]).astype(F32)
    half = HEAD_DIM // 2
    inv = ROPE_THETA ** (-jnp.arange(0, half, 2, dtype=F32) / half)
    ang = jnp.concatenate([row[:, None] * inv[None, :], col[:, None] * inv[None, :]], axis=-1)
    cos_t = jnp.tile(jnp.cos(ang), (1, 4))
    sin_t = jnp.concatenate([-jnp.tile(jnp.sin(ang), (1, 2)), jnp.tile(jnp.sin(ang), (1, 2))], axis=-1)
    return cos_t, sin_t


def _rope_gain(g):
    g = jnp.swapaxes(g.astype(F32).reshape(2, 2, HEAD_DIM // 4), 0, 1).reshape(2, 1, HEAD_DIM // 2)
    return jnp.broadcast_to(g, (2, 2, HEAD_DIM // 2)).reshape(LANES)


def _qkv_weight(wl):
    a, b, c = wl[:, 0:512], wl[:, 512:1024], wl[:, 1024:1536]
    return jnp.concatenate([_pair_cols(a[:, :256]), a[:, 256:],
                            _rope_cols(b[:, :256], N_HEADS), _rope_cols(b[:, 256:384], N_KV), b[:, 384:],
                            _pair_cols(c[:, :256]), c[:, 256:]], axis=1)


def kernel(x, meta_tokens, rel_bias, hgrn_lb_logits, ln_mix, w_in, attn_sink, qk_norm_q, qk_norm_k, diff_lambda,
           diff_subnorm, hgrn_out_norm, w_branch, w_out, ln_ffn, w_ffn_gate, w_ffn_up, w_ffn_down, ln_final):
    bn, S, _ = x.shape
    L = META_END + S
    depth = w_in.shape[0]
    assert L % ROW_TILE == 0 and S % GRID_W == 0
    h = jnp.concatenate([jnp.zeros((bn, FRONT, D_MODEL), x.dtype),
                         jnp.broadcast_to(meta_tokens.astype(x.dtype)[None], (bn, N_META, D_MODEL)), x], axis=1)
    bband, bmeta, bias5t = _bias_tables(rel_bias)
    cos_t, sin_t = _rope_tables(L, S)
    rhead = _rope_layout()
    lane = np.arange(LANES)
    hsel_rope = jnp.asarray(rhead[:, None] == rhead[None, :], BF16)
    hsel_half = jnp.asarray((lane[:, None] // HEAD_DIM) == (lane[None, :] // HEAD_DIM), BF16)
    w256 = np.arange(MIX_W)
    bd = jnp.asarray((w256[:, None] // HEAD_DIM) == (w256[None, :] // HEAD_DIM), BF16)
    cs = np.ones((1, QKV_W), np.float32)
    cs[0, 0:256] = HEAD_DIM ** -0.5 * LOG2E
    cs[0, 1024:1280] = C_DIM ** -0.5 * LOG2E
    cs = jnp.asarray(cs)
    lb_p = jax.nn.softmax(hgrn_lb_logits.astype(F32), axis=1)
    lb_all = jnp.cumsum(lb_p, axis=1) - lb_p[:, :1]

    for l in range(depth):
        wl = w_in[l]
        wqkv = _qkv_weight(wl).astype(BF16)
        wd = wl[:, QKV_W:QKV_W + D_W].astype(BF16)
        wgz = wl[:, QKV_W + D_W:].astype(BF16)
        gn = jnp.stack([_rope_gain(qk_norm_q[l]) * (HEAD_DIM ** -0.5 * LOG2E)] * 2 + [_rope_gain(qk_norm_k[l])])
        qkv, dproj, vtb, vtc = _in_proj(h, ln_mix[l][None], wqkv, wd, cs, cos_t, sin_t, gn, hsel_rope)
        ya = _attn_a(qkv, attn_sink[l].astype(F32) * LOG2E, bband, bmeta)
        yb = _attn_b(qkv, vtb)
        lam_init = 0.8 - 0.6 * math.exp(-0.3 * l)
        lam_p = diff_lambda[l].astype(F32)
        lam = jnp.exp(jnp.sum(lam_p[0] * lam_p[1])) - jnp.exp(jnp.sum(lam_p[2] * lam_p[3])) + lam_init
        scal = jnp.stack([lam, jnp.asarray(1.0 - lam_init, F32)])
        gsub = jnp.tile(diff_subnorm[l].astype(F32), 2)[None]
        yc = _attn_c(qkv, vtc, scal, bias5t, gsub, hsel_half)
        lb2 = lb_all[:, l][:, None, :]
        of, ob = _hgrn(dproj, lb2, bd)
        wb = w_branch[l].astype(BF16)
        wb_att = jnp.swapaxes(wb[:3].reshape(3, N_KV, 2, HEAD_DIM, D_MODEL), 1, 2).reshape(3, MIX_W, D_MODEL)
        wb = jnp.concatenate([wb_att, wb[3:]], axis=0)
        gout = jnp.tile(hgrn_out_norm[l].astype(F32), N_HEADS)[None]
        h = _merge(h, ya, yb, yc, of, ob, dproj, ln_mix[l][None], gout, bd, wgz, wb, w_out[l].astype(BF16))
        h = _ffn(h, ln_ffn[l][None], w_ffn_gate[l].astype(BF16), w_ffn_up[l].astype(BF16), w_ffn_down[l].astype(BF16))
    return _final_norm(h, ln_final[None])
```

```python
import math

import jax
import jax.numpy as jnp
import numpy as np
from jax import lax
from jax.experimental import pallas as pl
from jax.experimental.pallas import tpu as pltpu

D_MODEL = 1024
N_BRANCH = 4
MIX_W = D_MODEL // N_BRANCH
HEAD_DIM = 64
BLOCK = 128
WINDOW = 128
ROPE_THETA = 10000.0
GRID_W = 64
N_HEADS = 4
N_KV = 2
C_DIM = 32
N_META = 16
FRONT = (-N_META) % BLOCK
META_END = FRONT + N_META
N_BUCKETS = 32
MAX_DIST = 128
D_FF = -(-(8 * D_MODEL) // (3 * 256)) * 256
RMS_EPS = 1e-6
NEG = -1e30
F_FLOOR = 1e-30
LOG2E = 1.4426950408889634
LANES = 128
ROW_TILE = 384
C_UNROLL = 4
QKV_W = 3 * 512
D_W = 5 * MIX_W
VMEM_LIMIT = 56 * 1024 * 1024

F32 = jnp.float32
BF16 = jnp.bfloat16


def _dot(a, b):
    return jnp.dot(a, b, preferred_element_type=F32)


def _dot_nt(a, b):
    return lax.dot_general(a, b, (((1,), (1,)), ((), ())), preferred_element_type=F32)


def _split_dot(x, sel):
    hi = x.astype(BF16)
    lo = (x - hi.astype(F32)).astype(BF16)
    return _dot(hi, sel) + _dot(lo, sel)


def _rms(x, g):
    return x * lax.rsqrt(jnp.mean(x * x, axis=-1, keepdims=True) + RMS_EPS) * g


def _params(sem):
    return pltpu.CompilerParams(dimension_semantics=sem, vmem_limit_bytes=VMEM_LIMIT)


def _const_spec(shape):
    nd = len(shape)
    return pl.BlockSpec(shape, lambda *_: (0,) * nd)


def _in_proj_kernel(h_ref, g_ref, wqkv_ref, wd_ref, cs_ref, cos_ref, sin_ref, gn_ref, hsel_ref,
                    qkv_ref, d_ref, vtb_ref, vtc_ref):
    u = _rms(h_ref[0], g_ref[...]).astype(BF16)
    acc = _dot(u, wqkv_ref[...]) * cs_ref[...]
    qkv_ref[0, :, 0:512] = acc[:, 0:512].astype(BF16)
    for t in range(3):
        c0 = 512 + LANES * t
        xt = acc[:, c0:c0 + LANES]
        ms = _split_dot(xt * xt, hsel_ref[...]) * (1.0 / HEAD_DIM)
        xt = xt * lax.rsqrt(ms + RMS_EPS) * gn_ref[t:t + 1, :]
        xt = xt * cos_ref[...] + pltpu.roll(xt, LANES // 2, 1) * sin_ref[...]
        qkv_ref[0, :, c0:c0 + LANES] = xt.astype(BF16)
    qkv_ref[0, :, 896:QKV_W] = acc[:, 896:QKV_W].astype(BF16)
    vtb_ref[0] = acc[:, 896:1024].T.astype(BF16)
    vtc_ref[0] = acc[:, 1408:QKV_W].T.astype(BF16)
    d_ref[0] = _dot(u, wd_ref[...])


def _in_proj(h, g, wqkv, wd, cs, cos_t, sin_t, gn, hsel):
    bn, L, _ = h.shape
    nt = L // ROW_TILE
    return pl.pallas_call(
        _in_proj_kernel,
        grid=(bn, nt),
        in_specs=[
            pl.BlockSpec((1, ROW_TILE, D_MODEL), lambda b, i: (b, i, 0)),
            _const_spec((1, D_MODEL)),
            _const_spec((D_MODEL, QKV_W)),
            _const_spec((D_MODEL, D_W)),
            _const_spec((1, QKV_W)),
            pl.BlockSpec((ROW_TILE, LANES), lambda b, i: (i, 0)),
            pl.BlockSpec((ROW_TILE, LANES), lambda b, i: (i, 0)),
            _const_spec((3, LANES)),
            _const_spec((LANES, LANES)),
        ],
        out_specs=[
            pl.BlockSpec((1, ROW_TILE, QKV_W), lambda b, i: (b, i, 0)),
            pl.BlockSpec((1, ROW_TILE, D_W), lambda b, i: (b, i, 0)),
            pl.BlockSpec((1, LANES, ROW_TILE), lambda b, i: (b, 0, i)),
            pl.BlockSpec((1, LANES, ROW_TILE), lambda b, i: (b, 0, i)),
        ],
        out_shape=[jax.ShapeDtypeStruct((bn, L, QKV_W), BF16), jax.ShapeDtypeStruct((bn, L, D_W), F32),
                   jax.ShapeDtypeStruct((bn, LANES, L), BF16), jax.ShapeDtypeStruct((bn, LANES, L), BF16)],
        compiler_params=_params(("parallel", "parallel")),
        name="in_proj",
    )(h, g, wqkv, wd, cs, cos_t, sin_t, gn, hsel)


def _stack_q(q_ref, masks):
    parts = []
    for t in range(2):
        qt = q_ref[0, :, t * LANES:(t + 1) * LANES]
        for m in masks:
            parts.append(jnp.where(m, qt, jnp.zeros_like(qt)))
    return jnp.concatenate(parts, axis=0)


def _blk(ref, j):
    if isinstance(j, int):
        return ref[0, j * BLOCK:(j + 1) * BLOCK, :]
    return ref[0, pl.ds(pl.multiple_of(j * BLOCK, BLOCK), BLOCK), :]


def _attn_a_kernel(sink_ref, q_ref, k_ref, v_ref, bband_ref, bmeta_ref, o_ref):
    i = pl.program_id(1)
    nb = pl.num_programs(1)
    prev = jnp.maximum(i - 1, 0)
    nxt = jnp.minimum(i + 1, nb - 1)
    kband = jnp.concatenate([_blk(k_ref, prev), _blk(k_ref, i), _blk(k_ref, nxt)], axis=0)
    vband = jnp.concatenate([_blk(v_ref, prev), _blk(v_ref, i), _blk(v_ref, nxt)], axis=0)
    kmeta = k_ref[0, 0:BLOCK, :]
    vmeta = v_ref[0, 0:BLOCK, :]
    col = lax.broadcasted_iota(jnp.int32, (1, 3 * BLOCK), 1)
    lo = jnp.where(i == 0, 2 * BLOCK, jnp.where(i == 1, BLOCK, 0))
    hi = jnp.where(i == nb - 1, 2 * BLOCK, 3 * BLOCK)
    dead = (col < lo) | (col >= hi)
    lane = lax.broadcasted_iota(jnp.int32, (1, LANES), 1)
    outs = []
    for kv in range(N_KV):
        qs = _stack_q(q_ref, [(lane >= HEAD_DIM) == bool(kv)])
        bb = jnp.concatenate([bband_ref[2 * kv], bband_ref[2 * kv + 1]], axis=0)
        bm = jnp.concatenate([bmeta_ref[0, 2 * kv], bmeta_ref[0, 2 * kv + 1]], axis=0)
        sb = jnp.where(dead, NEG, _dot_nt(qs, kband) + bb)
        sm = _dot_nt(qs, kmeta) + bm
        sk = jnp.concatenate([jnp.full((BLOCK, 1), sink_ref[2 * kv], F32),
                              jnp.full((BLOCK, 1), sink_ref[2 * kv + 1], F32)], axis=0)
        m = jnp.maximum(jnp.maximum(jnp.max(sb, axis=-1, keepdims=True), jnp.max(sm, axis=-1, keepdims=True)), sk)
        pb = jnp.exp2(sb - m)
        pm = jnp.exp2(sm - m)
        den = jnp.sum(pb, axis=-1, keepdims=True) + jnp.sum(pm, axis=-1, keepdims=True) + jnp.exp2(sk - m)
        outs.append((_dot(pb.astype(BF16), vband) + _dot(pm.astype(BF16), vmeta)) / den)
    for t in range(2):
        y = jnp.where(lane < HEAD_DIM, outs[0][t * BLOCK:(t + 1) * BLOCK], outs[1][t * BLOCK:(t + 1) * BLOCK])
        o_ref[0, :, t * LANES:(t + 1) * LANES] = y.astype(BF16)


def _attn_a(qkv, sink, bband, bmeta):
    bn, L, _ = qkv.shape
    nb = L // BLOCK
    return pl.pallas_call(
        _attn_a_kernel,
        grid=(bn, nb),
        in_specs=[
            pl.BlockSpec(memory_space=pltpu.SMEM),
            pl.BlockSpec((1, BLOCK, 2 * LANES), lambda b, i: (b, i, 0)),
            pl.BlockSpec((1, L, LANES), lambda b, i: (b, 0, 2)),
            pl.BlockSpec((1, L, LANES), lambda b, i: (b, 0, 3)),
            _const_spec((N_HEADS, BLOCK, 3 * BLOCK)),
            pl.BlockSpec((1, N_HEADS, BLOCK, BLOCK), lambda b, i: (jnp.minimum(i, 2), 0, 0, 0)),
        ],
        out_specs=pl.BlockSpec((1, BLOCK, MIX_W), lambda b, i: (b, i, 0)),
        out_shape=jax.ShapeDtypeStruct((bn, L, MIX_W), BF16),
        compiler_params=_params(("parallel", "parallel")),
        name="attn_window",
    )(sink, qkv, qkv, qkv, bband, bmeta)


def _stack_maps(q_ref, masks_per_kv):
    parts = []
    for masks in masks_per_kv:
        for t in range(2):
            qt = q_ref[0, :, t * LANES:(t + 1) * LANES]
            for m in masks:
                parts.append(jnp.where(m, qt, jnp.zeros_like(qt)))
    return jnp.concatenate(parts, axis=0)


def _attn_sweep(qs_ref, k_ref, vt_ref, bias_fn, sa_ref, sb_ref, m_ref, acc_ref):
    nb = k_ref.shape[1] // BLOCK
    W = qs_ref.shape[0]
    half = W // 2
    ngrp = (nb - 1) // C_UNROLL
    keyrow = lax.broadcasted_iota(jnp.int32, (BLOCK, 1), 0)

    def rows(ref, j0, width):
        if isinstance(j0, int):
            return ref[0, j0 * BLOCK:(j0 + width) * BLOCK, :]
        return ref[0, pl.ds(pl.multiple_of(j0 * BLOCK, BLOCK), width * BLOCK), :]

    def cols(ref, j0, width):
        if isinstance(j0, int):
            return ref[0, :, j0 * BLOCK:(j0 + width) * BLOCK]
        return ref[0, :, pl.ds(pl.multiple_of(j0 * BLOCK, BLOCK), width * BLOCK)]

    def scores(j0, width):
        s_all = _dot_nt(rows(k_ref, j0, width), qs_ref[...])
        parts = []
        for u in range(width):
            s = s_all[u * BLOCK:(u + 1) * BLOCK]
            b = bias_fn(j0 + u)
            if b is not None:
                s = s + b
            if isinstance(j0, int) and j0 + u == 0:
                s = jnp.where(keyrow >= FRONT, s, NEG)
            parts.append(s)
        return parts[0] if width == 1 else jnp.concatenate(parts, axis=0)

    def absorb(s, j0, width, first=False):
        mg = jnp.max(jnp.max(s.reshape(width * BLOCK // 8, 8, W), axis=0), axis=0, keepdims=True)
        m_new = mg if first else jnp.maximum(m_ref[...], mg)
        p = jnp.exp2(s - m_new).astype(BF16)
        vt = cols(vt_ref, j0, width)
        ones = jnp.ones((16, width * BLOCK), BF16)
        if not first:
            alpha = jnp.exp2(m_ref[...] - m_new)
        for kv in range(N_KV):
            w = _dot(jnp.concatenate([vt[kv * HEAD_DIM:(kv + 1) * HEAD_DIM], ones], axis=0),
                     p[:, kv * half:(kv + 1) * half])
            acc_ref[kv] = w if first else acc_ref[kv] * alpha[:, kv * half:(kv + 1) * half] + w
        m_ref[...] = m_new

    def group(n):
        return 1 + n * C_UNROLL

    absorb(scores(0, 1), 0, 1, first=True)
    if ngrp > 0:
        sa_ref[...] = scores(group(0), C_UNROLL)
        npair = (ngrp - 1) // 2

        def body(n, carry):
            g = 2 * n
            sb_ref[...] = scores(group(g + 1), C_UNROLL)
            absorb(sa_ref[...], group(g), C_UNROLL)
            sa_ref[...] = scores(group(g + 2), C_UNROLL)
            absorb(sb_ref[...], group(g + 1), C_UNROLL)
            return carry

        lax.fori_loop(0, npair, body, 0)
        g = 2 * npair
        if ngrp - g == 2:
            sb_ref[...] = scores(group(g + 1), C_UNROLL)
            absorb(sa_ref[...], group(g), C_UNROLL)
            absorb(sb_ref[...], group(g + 1), C_UNROLL)
        else:
            absorb(sa_ref[...], group(g), C_UNROLL)
    for j in range(1 + ngrp * C_UNROLL, nb):
        absorb(scores(j, 1), j, 1)


def _sweep_scratch(nmaps):
    W = nmaps * BLOCK
    return [pltpu.VMEM((W, LANES), BF16), pltpu.VMEM((C_UNROLL * BLOCK, W), F32), pltpu.VMEM((C_UNROLL * BLOCK, W), F32),
            pltpu.VMEM((1, W), F32), pltpu.VMEM((N_KV, HEAD_DIM + 16, W // 2), F32)]


def _attn_b_kernel(q_ref, k_ref, vt_ref, o_ref, qs_ref, sa_ref, sb_ref, m_ref, acc_ref):
    lane = lax.broadcasted_iota(jnp.int32, (1, LANES), 1)
    head_b = (lane // (HEAD_DIM // 2)) % 2
    qs_ref[...] = _stack_maps(q_ref, [[head_b == kv] for kv in range(N_KV)])
    _attn_sweep(qs_ref, k_ref, vt_ref, lambda j: None, sa_ref, sb_ref, m_ref, acc_ref)
    ot = [acc_ref[kv][0:HEAD_DIM] / acc_ref[kv][HEAD_DIM:HEAD_DIM + 1] for kv in range(N_KV)]
    for g in range(2):
        yt = jnp.concatenate([ot[kv][:, g * BLOCK:(g + 1) * BLOCK] for kv in range(N_KV)], axis=0)
        o_ref[0, :, g * LANES:(g + 1) * LANES] = yt.T.astype(BF16)


def _attn_b(qkv, vt):
    bn, L, _ = qkv.shape
    nb = L // BLOCK
    return pl.pallas_call(
        _attn_b_kernel,
        grid=(bn, nb),
        in_specs=[
            pl.BlockSpec((1, BLOCK, 2 * LANES), lambda b, i: (b, i, 2)),
            pl.BlockSpec((1, L, LANES), lambda b, i: (b, 0, 6)),
            pl.BlockSpec((1, LANES, L), lambda b, i: (b, 0, 0)),
        ],
        out_specs=pl.BlockSpec((1, BLOCK, MIX_W), lambda b, i: (b, i, 0)),
        out_shape=jax.ShapeDtypeStruct((bn, L, MIX_W), BF16),
        scratch_shapes=_sweep_scratch(4),
        compiler_params=_params(("parallel", "parallel")),
        name="attn_rope",
    )(qkv, qkv, vt)


def _attn_c_kernel(sc_ref, q_ref, k_ref, vt_ref, bias_ref, gsub_ref, hsel_ref, o_ref,
                   qs_ref, sa_ref, sb_ref, m_ref, acc_ref):
    i = pl.program_id(1)
    lam = sc_ref[0]
    lane = lax.broadcasted_iota(jnp.int32, (1, LANES), 1)
    grp = lane // C_DIM
    qs_ref[...] = _stack_maps(q_ref, [[grp == 2 * kv, grp == 2 * kv + 1] for kv in range(N_KV)])

    def bias(j):
        t = jnp.clip(j - i + 2, 0, 4)
        tiles = [bias_ref[t, h] for h in range(N_HEADS)]
        return jnp.concatenate([tiles[h] for h in range(N_HEADS) for _ in range(2)], axis=1)

    _attn_sweep(qs_ref, k_ref, vt_ref, bias, sa_ref, sb_ref, m_ref, acc_ref)
    ot = [acc_ref[kv][0:HEAD_DIM] / acc_ref[kv][HEAD_DIM:HEAD_DIM + 1] for kv in range(N_KV)]
    for g in range(2):
        dt = [ot[kv][:, (2 * g) * BLOCK:(2 * g + 1) * BLOCK] - lam * ot[kv][:, (2 * g + 1) * BLOCK:(2 * g + 2) * BLOCK]
              for kv in range(N_KV)]
        y = jnp.concatenate(dt, axis=0).T
        ms = _split_dot(y * y, hsel_ref[...]) * (1.0 / HEAD_DIM)
        y = y * lax.rsqrt(ms + RMS_EPS) * gsub_ref[...] * sc_ref[1]
        o_ref[0, :, g * LANES:(g + 1) * LANES] = y.astype(BF16)


def _attn_c(qkv, vt, scal, bias5t, gsub, hsel):
    bn, L, _ = qkv.shape
    nb = L // BLOCK
    return pl.pallas_call(
        _attn_c_kernel,
        grid=(bn, nb),
        in_specs=[
            pl.BlockSpec(memory_space=pltpu.SMEM),
            pl.BlockSpec((1, BLOCK, 2 * LANES), lambda b, i: (b, i, 4)),
            pl.BlockSpec((1, L, LANES), lambda b, i: (b, 0, 10)),
            pl.BlockSpec((1, LANES, L), lambda b, i: (b, 0, 0)),
            _const_spec((5, N_HEADS, BLOCK, BLOCK)),
            _const_spec((1, LANES)),
            _const_spec((LANES, LANES)),
        ],
        out_specs=pl.BlockSpec((1, BLOCK, MIX_W), lambda b, i: (b, i, 0)),
        out_shape=jax.ShapeDtypeStruct((bn, L, MIX_W), BF16),
        scratch_shapes=_sweep_scratch(8),
        compiler_params=_params(("parallel", "parallel")),
        name="attn_diff",
    )(scal, qkv, qkv, vt, bias5t, gsub, hsel)


def _split3_dot(sel, x):
    x1 = x.astype(BF16)
    r1 = x - x1.astype(F32)
    x2 = r1.astype(BF16)
    x3 = (r1 - x2.astype(F32)).astype(BF16)
    return _dot(sel, x1) + _dot(sel, x2) + _dot(sel, x3)


def _hgrn_chunk(forward, q, kk, g, v, tri_ref, bd_ref, st_ref):
    W = MIX_W
    b = _split3_dot(tri_ref[...], g)
    tot = b[BLOCK - 1:BLOCK, :]
    if forward:
        x, e = b, tot - b
    else:
        x, e = tot - b + g, b - g
    sub = lax.broadcasted_iota(jnp.int32, (1, 8, 1), 1)
    x3, q3, k3, v3 = (a.reshape(BLOCK // 8, 8, W) for a in (x, q, kk, v))
    o = _dot((q * kk).astype(BF16), bd_ref[...]) * v
    for d in range(1, 8):
        sh = d if forward else 8 - d
        ok = (sub >= d) if forward else (sub + d < 8)
        e_d = jnp.exp2(jnp.where(ok, x3 - pltpu.roll(x3, sh, 1), NEG)) * q3 * pltpu.roll(k3, sh, 1)
        o = o + _dot(e_d.reshape(BLOCK, W).astype(BF16), bd_ref[...]) * pltpu.roll(v3, sh, 1).reshape(BLOCK, W)
    row = lax.broadcasted_iota(jnp.int32, (BLOCK, 1), 0)
    qcol = lax.broadcasted_iota(jnp.int32, (1, N_HEADS * BLOCK), 1) % BLOCK
    lane_head = lax.broadcasted_iota(jnp.int32, (1, W), 1) // HEAD_DIM
    at = jnp.zeros((BLOCK, N_HEADS * BLOCK), F32)
    for m in (8, 16, 32, 64):
        mid = m - 1 if forward else m
        xm = jnp.concatenate([jnp.broadcast_to(x[i + mid:i + mid + 1, :], (2 * m, W))
                              for i in range(0, BLOCK, 2 * m)], axis=0)
        second = (row // m) % 2 == 1
        qside = second if forward else jnp.logical_not(second)
        qt = (q * jnp.exp2(jnp.where(qside, x - xm, NEG))).astype(BF16)
        kt = (kk * jnp.exp2(jnp.where(qside, NEG, xm - x))).astype(BF16)
        qstack = jnp.concatenate([jnp.where(lane_head == h, qt, jnp.zeros_like(qt)) for h in range(N_HEADS)], axis=0)
        a = _dot_nt(kt, qstack)
        at = at + jnp.where(row // (2 * m) == qcol // (2 * m), a, 0.0)
    vt = v.T.astype(BF16)
    ot = _dot(vt, at.astype(BF16))
    o = o + jnp.concatenate([ot[h * HEAD_DIM:(h + 1) * HEAD_DIM, h * BLOCK:(h + 1) * BLOCK]
                             for h in range(N_HEADS)], axis=0).T
    qd = (q * jnp.exp2(x)).astype(BF16)
    kd = (kk * jnp.exp2(e)).astype(BF16)
    gam = jnp.exp2(tot)
    hmask = (lax.broadcasted_iota(jnp.int32, (LANES, LANES), 0) // HEAD_DIM
             == lax.broadcasted_iota(jnp.int32, (LANES, LANES), 1) // HEAD_DIM)
    inter = []
    for hf in range(2):
        lanes = slice(hf * LANES, (hf + 1) * LANES)
        st = st_ref[hf]
        inter.append(_dot_nt(qd[:, lanes], st.astype(BF16)))
        st_ref[hf] = st * gam[:, lanes] + jnp.where(hmask, _dot(vt[lanes, :], kd[:, lanes]), 0.0)
    return o + jnp.concatenate(inter, axis=1)


def _hgrn_direction(forward, first_tile, q_ref, z_ref, v_ref, lb_ref, tri_ref, bd_ref, o_ref, st_ref):
    R = q_ref.shape[1]
    z = z_ref[0]
    q = q_ref[0] * (HEAD_DIM ** -0.5)
    v = v_ref[0]
    lb = lb_ref[...]
    row = lax.broadcasted_iota(jnp.int32, (R, 1), 0)
    f = lb + (1.0 - lb) * jax.nn.sigmoid(z)
    g = jnp.log(jnp.maximum(f, F_FLOOR)) * LOG2E
    kk = (1.0 - lb) * jax.nn.sigmoid(-z)
    kk = jnp.where(row < jnp.where(first_tile, FRONT, 0), 0.0, kk)
    nblk = R // BLOCK
    for n in range(nblk):
        r0 = (n if forward else nblk - 1 - n) * BLOCK
        rows = slice(r0, r0 + BLOCK)
        o_ref[0, rows, :] = _hgrn_chunk(forward, q[rows], kk[rows], g[rows], v[rows], tri_ref, bd_ref, st_ref)


def _hgrn_kernel(qf_ref, zf_ref, vf_ref, qb_ref, zb_ref, vb_ref, lb_ref, tri_ref, bd_ref, of_ref, ob_ref, st_ref):
    j = pl.program_id(1)
    nt = pl.num_programs(1)

    @pl.when(j == 0)
    def _():
        st_ref[...] = jnp.zeros_like(st_ref)

    _hgrn_direction(True, j == 0, qf_ref, zf_ref, vf_ref, lb_ref.at[0], tri_ref, bd_ref, of_ref, st_ref.at[0])
    _hgrn_direction(False, j == nt - 1, qb_ref, zb_ref, vb_ref, lb_ref.at[1], tri_ref, bd_ref, ob_ref, st_ref.at[1])


def _hgrn(dproj, lb2, tri, bd):
    bn, L, _ = dproj.shape
    nt = L // ROW_TILE
    fw = lambda c: pl.BlockSpec((1, ROW_TILE, MIX_W), lambda b, j: (b, j, c))
    bw = lambda c: pl.BlockSpec((1, ROW_TILE, MIX_W), lambda b, j: (b, nt - 1 - j, c))
    return pl.pallas_call(
        _hgrn_kernel,
        grid=(bn, nt),
        in_specs=[fw(0), fw(1), fw(3), bw(0), bw(2), bw(3), _const_spec((2, 1, MIX_W)), _const_spec((BLOCK, BLOCK)),
                  _const_spec((MIX_W, MIX_W))],
        out_specs=[pl.BlockSpec((1, ROW_TILE, MIX_W), lambda b, j: (b, j, 0)),
                   pl.BlockSpec((1, ROW_TILE, MIX_W), lambda b, j: (b, nt - 1 - j, 0))],
        out_shape=[jax.ShapeDtypeStruct((bn, L, MIX_W), F32)] * 2,
        scratch_shapes=[pltpu.VMEM((2, 2, LANES, LANES), F32)],
        compiler_params=_params(("parallel", "arbitrary")),
        name="hgrn",
    )(dproj, dproj, dproj, dproj, dproj, dproj, lb2, tri, bd)


def _merge_kernel(h_ref, ya_ref, yb_ref, yc_ref, of_ref, ob_ref, dg_ref, g_ref, gout_ref, bd_ref,
                  wgz_ref, wb_ref, wo_ref, o_ref):
    h = h_ref[0]
    u = _rms(h, g_ref[...]).astype(BF16)
    od = of_ref[0] + ob_ref[0]
    ms = _split_dot(od * od, bd_ref[...]) * (1.0 / HEAD_DIM)
    yd = (od * lax.rsqrt(ms + RMS_EPS) * gout_ref[...] * jax.nn.silu(dg_ref[0])).astype(BF16)
    ys = (ya_ref[0], yb_ref[0], yc_ref[0], yd)
    merged = None
    for n in range(N_BRANCH):
        gate = jax.nn.sigmoid(_dot(u, wgz_ref[:, n * D_MODEL:(n + 1) * D_MODEL]))
        term = gate * _dot(ys[n], wb_ref[n])
        merged = term if merged is None else merged + term
    o_ref[0] = h + _dot(merged.astype(BF16), wo_ref[...])


def _merge(h, ya, yb, yc, of, ob, dproj, g, gout, bd, wgz, wb, wo):
    bn, L, _ = h.shape
    nt = L // ROW_TILE
    row = lambda w, c=0: pl.BlockSpec((1, ROW_TILE, w), lambda b, i: (b, i, c))
    return pl.pallas_call(
        _merge_kernel,
        grid=(bn, nt),
        in_specs=[row(D_MODEL), row(MIX_W), row(MIX_W), row(MIX_W), row(MIX_W), row(MIX_W), row(MIX_W, 4),
                  _const_spec((1, D_MODEL)), _const_spec((1, MIX_W)), _const_spec((MIX_W, MIX_W)),
                  _const_spec((D_MODEL, N_BRANCH * D_MODEL)), _const_spec((N_BRANCH, MIX_W, D_MODEL)),
                  _const_spec((D_MODEL, D_MODEL))],
        out_specs=row(D_MODEL),
        out_shape=jax.ShapeDtypeStruct((bn, L, D_MODEL), F32),
        compiler_params=_params(("parallel", "parallel")),
        name="merge",
    )(h, ya, yb, yc, of, ob, dproj, g, gout, bd, wgz, wb, wo)


def _ffn_kernel(h_ref, g_ref, wg_ref, wu_ref, wd_ref, o_ref):
    h = h_ref[0]
    u = _rms(h, g_ref[...]).astype(BF16)
    half = D_FF // 2
    out = h
    for c in range(2):
        a = _dot(u, wg_ref[:, c * half:(c + 1) * half])
        t = (jax.nn.silu(a) * _dot(u, wu_ref[:, c * half:(c + 1) * half])).astype(BF16)
        out = out + _dot(t, wd_ref[c * half:(c + 1) * half, :])
    o_ref[0] = out


def _ffn(h, g, wg, wu, wd):
    bn, L, _ = h.shape
    nt = L // ROW_TILE
    row = pl.BlockSpec((1, ROW_TILE, D_MODEL), lambda b, i: (b, i, 0))
    return pl.pallas_call(
        _ffn_kernel,
        grid=(bn, nt),
        in_specs=[row, _const_spec((1, D_MODEL)), _const_spec((D_MODEL, D_FF)), _const_spec((D_MODEL, D_FF)),
                  _const_spec((D_FF, D_MODEL))],
        out_specs=row,
        out_shape=jax.ShapeDtypeStruct((bn, L, D_MODEL), F32),
        compiler_params=_params(("parallel", "parallel")),
        name="ffn",
    )(h, g, wg, wu, wd)


def _final_kernel(h_ref, g_ref, o_ref):
    o_ref[0] = _rms(h_ref[0], g_ref[...])


def _final_norm(h, g):
    bn, L, _ = h.shape
    S = L - META_END
    return pl.pallas_call(
        _final_kernel,
        grid=(bn, S // BLOCK),
        in_specs=[pl.BlockSpec((1, BLOCK, D_MODEL), lambda b, i: (b, i + META_END // BLOCK, 0)), _const_spec((1, D_MODEL))],
        out_specs=pl.BlockSpec((1, BLOCK, D_MODEL), lambda b, i: (b, i, 0)),
        out_shape=jax.ShapeDtypeStruct((bn, S, D_MODEL), F32),
        compiler_params=_params(("parallel", "parallel")),
        name="final_norm",
    )(h, g)


def _t5_bucket(rel):
    half = N_BUCKETS // 2
    exact = half // 2
    n = jnp.abs(rel)
    nf = jnp.maximum(n, exact).astype(F32)
    big = exact + (jnp.log(nf / exact) / math.log(MAX_DIST / exact) * (half - exact)).astype(jnp.int32)
    big = jnp.minimum(big, half - 1)
    return jnp.where(rel > 0, half, 0) + jnp.where(n < exact, n, big)


def _lookup(table, bucket):
    hit = bucket[None, ..., None] == jnp.arange(N_BUCKETS)
    return jnp.sum(jnp.where(hit, table.T.reshape((N_HEADS,) + (1,) * bucket.ndim + (N_BUCKETS,)), 0.0), axis=-1)


def _bias_tables(rel_bias):
    bias_a = rel_bias[:, :N_HEADS].astype(F32) * LOG2E
    bias_c = rel_bias[:, N_HEADS:].astype(F32) * LOG2E
    r = jnp.arange(BLOCK)[:, None]
    rel_band = (jnp.arange(3 * BLOCK)[None, :] - BLOCK) - r
    bband = jnp.where(jnp.abs(rel_band)[None] <= WINDOW, _lookup(bias_a, _t5_bucket(rel_band)), NEG)
    lane = jnp.arange(BLOCK)[None, :]
    rel_m = lane[None] - (jnp.arange(3)[:, None, None] * BLOCK + r[None])
    bmeta = jnp.where((lane >= FRONT)[None, None], jnp.moveaxis(_lookup(bias_a, _t5_bucket(rel_m)), 0, 1), NEG)
    rel5t = (jnp.arange(5)[:, None, None] - 2) * BLOCK + r[None] - lane[None]
    bias5t = jnp.moveaxis(_lookup(bias_c, _t5_bucket(rel5t)), 0, 1)
    return bband, bmeta, bias5t


def _rope_layout():
    return np.concatenate([np.zeros(32, int), np.ones(32, int)] * 2)


def _rope_cols(w, heads):
    lead = w.shape[:-1]
    w = w.reshape(lead + (heads // 2, 2, 2, 2, HEAD_DIM // 4))
    if heads == N_HEADS:
        w = jnp.moveaxis(w, (-5, -4, -3, -2), (-3, -5, -2, -4))
    else:
        w = jnp.moveaxis(w, (-4, -3, -2), (-3, -2, -4))
    return w.reshape(lead + (heads * HEAD_DIM,))


def _pair_cols(w):
    lead = w.shape[:-1]
    return jnp.swapaxes(w.reshape(lead + (N_KV, 2, HEAD_DIM)), -3, -2).reshape(lead + (N_HEADS * HEAD_DIM,))


def _rope_tables(L, S):
    rows = S // GRID_W
    row = jnp.concatenate([jnp.zeros((FRONT,), jnp.int32), -jnp.ones((N_META,), jnp.int32),
                           jnp.repeat(jnp.arange(rows, dtype=jnp.int32), GRID_W)]).astype(F32)
    col = jnp.concatenate([jnp.zeros((FRONT,), jnp.int32), jnp.arange(N_META, dtype=jnp.int32),
                           jnp.tile(jnp.arange(GRID_W, dtype=jnp.int32), rows)]).astype(F32)
    half = HEAD_DIM // 2
    inv = ROPE_THETA ** (-jnp.arange(0, half, 2, dtype=F32) / half)
    ang = jnp.concatenate([row[:, None] * inv[None, :], col[:, None] * inv[None, :]], axis=-1)
    cos_t = jnp.tile(jnp.cos(ang), (1, 4))
    sin_t = jnp.concatenate([-jnp.tile(jnp.sin(ang), (1, 2)), jnp.tile(jnp.sin(ang), (1, 2))], axis=-1)
    return cos_t, sin_t


def _rope_gain(g):
    g = jnp.swapaxes(g.astype(F32).reshape(2, 2, HEAD_DIM // 4), 0, 1).reshape(2, 1, HEAD_DIM // 2)
    return jnp.broadcast_to(g, (2, 2, HEAD_DIM // 2)).reshape(LANES)


def _qkv_weight(wl):
    a, b, c = wl[:, 0:512], wl[:, 512:1024], wl[:, 1024:1536]
    return jnp.concatenate([_pair_cols(a[:, :256]), a[:, 256:],
                            _rope_cols(b[:, :256], N_HEADS), _rope_cols(b[:, 256:384], N_KV), b[:, 384:],
                            _pair_cols(c[:, :256]), c[:, 256:]], axis=1)


def kernel(x, meta_tokens, rel_bias, hgrn_lb_logits, ln_mix, w_in, attn_sink, qk_norm_q, qk_norm_k, diff_lambda,
           diff_subnorm, hgrn_out_norm, w_branch, w_out, ln_ffn, w_ffn_gate, w_ffn_up, w_ffn_down, ln_final):
    bn, S, _ = x.shape
    L = META_END + S
    depth = w_in.shape[0]
    assert L % ROW_TILE == 0 and S % GRID_W == 0
    h = jnp.concatenate([jnp.zeros((bn, FRONT, D_MODEL), x.dtype),
                         jnp.broadcast_to(meta_tokens.astype(x.dtype)[None], (bn, N_META, D_MODEL)), x], axis=1)
    bband, bmeta, bias5t = _bias_tables(rel_bias)
    cos_t, sin_t = _rope_tables(L, S)
    rhead = _rope_layout()
    lane = np.arange(LANES)
    hsel_rope = jnp.asarray(rhead[:, None] == rhead[None, :], BF16)
    hsel_half = jnp.asarray((lane[:, None] // HEAD_DIM) == (lane[None, :] // HEAD_DIM), BF16)
    w256 = np.arange(MIX_W)
    bd = jnp.asarray((w256[:, None] // HEAD_DIM) == (w256[None, :] // HEAD_DIM), BF16)
    tri = jnp.asarray(np.tril(np.ones((BLOCK, BLOCK))), BF16)
    cs = np.ones((1, QKV_W), np.float32)
    cs[0, 0:256] = HEAD_DIM ** -0.5 * LOG2E
    cs[0, 1024:1280] = C_DIM ** -0.5 * LOG2E
    cs = jnp.asarray(cs)
    lb_p = jax.nn.softmax(hgrn_lb_logits.astype(F32), axis=1)
    lb_all = jnp.cumsum(lb_p, axis=1) - lb_p[:, :1]

    for l in range(depth):
        wl = w_in[l]
        wqkv = _qkv_weight(wl).astype(BF16)
        wd = wl[:, QKV_W:QKV_W + D_W].astype(BF16)
        wgz = wl[:, QKV_W + D_W:].astype(BF16)
        gn = jnp.stack([_rope_gain(qk_norm_q[l]) * (HEAD_DIM ** -0.5 * LOG2E)] * 2 + [_rope_gain(qk_norm_k[l])])
        qkv, dproj, vtb, vtc = _in_proj(h, ln_mix[l][None], wqkv, wd, cs, cos_t, sin_t, gn, hsel_rope)
        ya = _attn_a(qkv, attn_sink[l].astype(F32) * LOG2E, bband, bmeta)
        yb = _attn_b(qkv, vtb)
        lam_init = 0.8 - 0.6 * math.exp(-0.3 * l)
        lam_p = diff_lambda[l].astype(F32)
        lam = jnp.exp(jnp.sum(lam_p[0] * lam_p[1])) - jnp.exp(jnp.sum(lam_p[2] * lam_p[3])) + lam_init
        scal = jnp.stack([lam, jnp.asarray(1.0 - lam_init, F32)])
        gsub = jnp.tile(diff_subnorm[l].astype(F32), 2)[None]
        yc = _attn_c(qkv, vtc, scal, bias5t, gsub, hsel_half)
        lb2 = lb_all[:, l][:, None, :]
        of, ob = _hgrn(dproj, lb2, tri, bd)
        wb = w_branch[l].astype(BF16)
        wb_att = jnp.swapaxes(wb[:3].reshape(3, N_KV, 2, HEAD_DIM, D_MODEL), 1, 2).reshape(3, MIX_W, D_MODEL)
        wb = jnp.concatenate([wb_att, wb[3:]], axis=0)
        gout = jnp.tile(hgrn_out_norm[l].astype(F32), N_HEADS)[None]
        h = _merge(h, ya, yb, yc, of, ob, dproj, ln_mix[l][None], gout, bd, wgz, wb, w_out[l].astype(BF16))
        h = _ffn(h, ln_ffn[l][None], w_ffn_gate[l].astype(BF16), w_ffn_up[l].astype(BF16), w_ffn_down[l].astype(BF16))
    return _final_norm(h, ln_final[None])
```

```python
import math

import jax
import jax.numpy as jnp
import numpy as np
from jax import lax
from jax.experimental import pallas as pl
from jax.experimental.pallas import tpu as pltpu

D_MODEL = 1024
N_BRANCH = 4
MIX_W = D_MODEL // N_BRANCH
HEAD_DIM = 64
BLOCK = 128
WINDOW = 128
ROPE_THETA = 10000.0
GRID_W = 64
N_HEADS = 4
N_KV = 2
C_DIM = 32
N_META = 16
FRONT = (-N_META) % BLOCK
META_END = FRONT + N_META
N_BUCKETS = 32
MAX_DIST = 128
D_FF = -(-(8 * D_MODEL) // (3 * 256)) * 256
RMS_EPS = 1e-6
NEG = -1e30
F_FLOOR = 1e-30
LOG2E = 1.4426950408889634
LANES = 128
ROW_TILE = 384
DENSE_TILES = (768, 384, 128)
C_UNROLL = 4
QKV_W = 3 * 512
D_W = 5 * MIX_W
VMEM_LIMIT = 56 * 1024 * 1024

F32 = jnp.float32
BF16 = jnp.bfloat16


def _dot(a, b):
    return jnp.dot(a, b, preferred_element_type=F32)


def _dot_nt(a, b):
    return lax.dot_general(a, b, (((1,), (1,)), ((), ())), preferred_element_type=F32)


def _split_dot(x, sel):
    hi = x.astype(BF16)
    lo = (x - hi.astype(F32)).astype(BF16)
    return _dot(hi, sel) + _dot(lo, sel)


def _rms(x, g):
    return x * lax.rsqrt(jnp.mean(x * x, axis=-1, keepdims=True) + RMS_EPS) * g


def _params(sem):
    return pltpu.CompilerParams(dimension_semantics=sem, vmem_limit_bytes=VMEM_LIMIT)


def _const_spec(shape):
    nd = len(shape)
    return pl.BlockSpec(shape, lambda *_: (0,) * nd, pipeline_mode=pl.Buffered(1))


def _dense_tile(rows):
    return next(t for t in DENSE_TILES if rows % t == 0)


def _in_proj_kernel(h_ref, g_ref, wqkv_ref, wd_ref, cs_ref, cos_ref, sin_ref, gn_ref, hsel_ref,
                    qkv_ref, d_ref, vtb_ref, vtc_ref):
    u = _rms(h_ref[0], g_ref[...]).astype(BF16)
    acc = _dot(u, wqkv_ref[...]) * cs_ref[...]
    qkv_ref[0, :, 0:512] = acc[:, 0:512].astype(BF16)
    for t in range(3):
        c0 = 512 + LANES * t
        xt = acc[:, c0:c0 + LANES]
        ms = _split_dot(xt * xt, hsel_ref[...]) * (1.0 / HEAD_DIM)
        xt = xt * lax.rsqrt(ms + RMS_EPS) * gn_ref[t:t + 1, :]
        xt = xt * cos_ref[...] + pltpu.roll(xt, LANES // 2, 1) * sin_ref[...]
        qkv_ref[0, :, c0:c0 + LANES] = xt.astype(BF16)
    qkv_ref[0, :, 896:QKV_W] = acc[:, 896:QKV_W].astype(BF16)
    vtb_ref[0] = acc[:, 896:1024].T.astype(BF16)
    vtc_ref[0] = acc[:, 1408:QKV_W].T.astype(BF16)
    d_ref[0] = _dot(u, wd_ref[...])


def _in_proj(h, g, wqkv, wd, cs, cos_t, sin_t, gn, hsel):
    bn, L, _ = h.shape
    nt = L // ROW_TILE
    return pl.pallas_call(
        _in_proj_kernel,
        grid=(bn, nt),
        in_specs=[
            pl.BlockSpec((1, ROW_TILE, D_MODEL), lambda b, i: (b, i, 0)),
            _const_spec((1, D_MODEL)),
            _const_spec((D_MODEL, QKV_W)),
            _const_spec((D_MODEL, D_W)),
            _const_spec((1, QKV_W)),
            pl.BlockSpec((ROW_TILE, LANES), lambda b, i: (i, 0)),
            pl.BlockSpec((ROW_TILE, LANES), lambda b, i: (i, 0)),
            _const_spec((3, LANES)),
            _const_spec((LANES, LANES)),
        ],
        out_specs=[
            pl.BlockSpec((1, ROW_TILE, QKV_W), lambda b, i: (b, i, 0)),
            pl.BlockSpec((1, ROW_TILE, D_W), lambda b, i: (b, i, 0)),
            pl.BlockSpec((1, LANES, ROW_TILE), lambda b, i: (b, 0, i)),
            pl.BlockSpec((1, LANES, ROW_TILE), lambda b, i: (b, 0, i)),
        ],
        out_shape=[jax.ShapeDtypeStruct((bn, L, QKV_W), BF16), jax.ShapeDtypeStruct((bn, L, D_W), F32),
                   jax.ShapeDtypeStruct((bn, LANES, L), BF16), jax.ShapeDtypeStruct((bn, LANES, L), BF16)],
        compiler_params=_params(("parallel", "parallel")),
        name="in_proj",
    )(h, g, wqkv, wd, cs, cos_t, sin_t, gn, hsel)


def _stack_q(q_ref, masks):
    parts = []
    for t in range(2):
        qt = q_ref[0, :, t * LANES:(t + 1) * LANES]
        for m in masks:
            parts.append(jnp.where(m, qt, jnp.zeros_like(qt)))
    return jnp.concatenate(parts, axis=0)


def _blk(ref, j):
    if isinstance(j, int):
        return ref[0, j * BLOCK:(j + 1) * BLOCK, :]
    return ref[0, pl.ds(pl.multiple_of(j * BLOCK, BLOCK), BLOCK), :]


def _attn_a_kernel(sink_ref, q_ref, k_ref, v_ref, bband_ref, bmeta_ref, o_ref):
    i = pl.program_id(1)
    nb = pl.num_programs(1)
    prev = jnp.maximum(i - 1, 0)
    nxt = jnp.minimum(i + 1, nb - 1)
    kband = jnp.concatenate([_blk(k_ref, prev), _blk(k_ref, i), _blk(k_ref, nxt)], axis=0)
    vband = jnp.concatenate([_blk(v_ref, prev), _blk(v_ref, i), _blk(v_ref, nxt)], axis=0)
    kmeta = k_ref[0, 0:BLOCK, :]
    vmeta = v_ref[0, 0:BLOCK, :]
    col = lax.broadcasted_iota(jnp.int32, (1, 3 * BLOCK), 1)
    lo = jnp.where(i == 0, 2 * BLOCK, jnp.where(i == 1, BLOCK, 0))
    hi = jnp.where(i == nb - 1, 2 * BLOCK, 3 * BLOCK)
    dead = (col < lo) | (col >= hi)
    lane = lax.broadcasted_iota(jnp.int32, (1, LANES), 1)
    outs = []
    for kv in range(N_KV):
        qs = _stack_q(q_ref, [(lane >= HEAD_DIM) == bool(kv)])
        bb = jnp.concatenate([bband_ref[2 * kv], bband_ref[2 * kv + 1]], axis=0)
        bm = jnp.concatenate([bmeta_ref[0, 2 * kv], bmeta_ref[0, 2 * kv + 1]], axis=0)
        sb = jnp.where(dead, NEG, _dot_nt(qs, kband) + bb)
        sm = _dot_nt(qs, kmeta) + bm
        sk = jnp.concatenate([jnp.full((BLOCK, 1), sink_ref[2 * kv], F32),
                              jnp.full((BLOCK, 1), sink_ref[2 * kv + 1], F32)], axis=0)
        m = jnp.maximum(jnp.maximum(jnp.max(sb, axis=-1, keepdims=True), jnp.max(sm, axis=-1, keepdims=True)), sk)
        pb = jnp.exp2(sb - m)
        pm = jnp.exp2(sm - m)
        den = jnp.sum(pb, axis=-1, keepdims=True) + jnp.sum(pm, axis=-1, keepdims=True) + jnp.exp2(sk - m)
        outs.append((_dot(pb.astype(BF16), vband) + _dot(pm.astype(BF16), vmeta)) / den)
    for t in range(2):
        y = jnp.where(lane < HEAD_DIM, outs[0][t * BLOCK:(t + 1) * BLOCK], outs[1][t * BLOCK:(t + 1) * BLOCK])
        o_ref[0, :, t * LANES:(t + 1) * LANES] = y.astype(BF16)


def _attn_a(qkv, sink, bband, bmeta):
    bn, L, _ = qkv.shape
    nb = L // BLOCK
    return pl.pallas_call(
        _attn_a_kernel,
        grid=(bn, nb),
        in_specs=[
            pl.BlockSpec(memory_space=pltpu.SMEM),
            pl.BlockSpec((1, BLOCK, 2 * LANES), lambda b, i: (b, i, 0)),
            pl.BlockSpec((1, L, LANES), lambda b, i: (b, 0, 2)),
            pl.BlockSpec((1, L, LANES), lambda b, i: (b, 0, 3)),
            _const_spec((N_HEADS, BLOCK, 3 * BLOCK)),
            pl.BlockSpec((1, N_HEADS, BLOCK, BLOCK), lambda b, i: (jnp.minimum(i, 2), 0, 0, 0)),
        ],
        out_specs=pl.BlockSpec((1, BLOCK, MIX_W), lambda b, i: (b, i, 0)),
        out_shape=jax.ShapeDtypeStruct((bn, L, MIX_W), BF16),
        compiler_params=_params(("parallel", "parallel")),
        name="attn_window",
    )(sink, qkv, qkv, qkv, bband, bmeta)


def _stack_maps(q_ref, masks_per_kv):
    parts = []
    for masks in masks_per_kv:
        for t in range(2):
            qt = q_ref[0, :, t * LANES:(t + 1) * LANES].astype(F32)
            for m in masks:
                parts.append(jnp.where(m, qt, 0.0).T.astype(BF16))
    return jnp.concatenate(parts, axis=1)


def _attn_sweep(qs_ref, k_ref, vt_ref, bias_fn, sa_ref, sb_ref, m_ref, acc_ref):
    nb = k_ref.shape[1] // BLOCK
    W = qs_ref.shape[1]
    half = W // 2
    ngrp = (nb - 1) // C_UNROLL
    keyrow = lax.broadcasted_iota(jnp.int32, (BLOCK, 1), 0)

    def rows(ref, j0, width):
        if isinstance(j0, int):
            return ref[0, j0 * BLOCK:(j0 + width) * BLOCK, :]
        return ref[0, pl.ds(pl.multiple_of(j0 * BLOCK, BLOCK), width * BLOCK), :]

    def cols(ref, j0, width):
        if isinstance(j0, int):
            return ref[0, :, j0 * BLOCK:(j0 + width) * BLOCK]
        return ref[0, :, pl.ds(pl.multiple_of(j0 * BLOCK, BLOCK), width * BLOCK)]

    def scores(j0, width):
        s_all = _dot(rows(k_ref, j0, width), qs_ref[...])
        parts = []
        for u in range(width):
            s = s_all[u * BLOCK:(u + 1) * BLOCK]
            b = bias_fn(j0 + u)
            if b is not None:
                s = s + b
            if isinstance(j0, int) and j0 + u == 0:
                s = jnp.where(keyrow >= FRONT, s, NEG)
            parts.append(s)
        return parts[0] if width == 1 else jnp.concatenate(parts, axis=0)

    def absorb(s, j0, width, first=False):
        mg = jnp.max(jnp.max(s.reshape(width * BLOCK // 8, 8, W), axis=0), axis=0, keepdims=True)
        m_new = mg if first else jnp.maximum(m_ref[...], mg)
        p = jnp.exp2(s - m_new).astype(BF16)
        vt = cols(vt_ref, j0, width)
        ones = jnp.ones((16, width * BLOCK), BF16)
        if not first:
            alpha = jnp.exp2(m_ref[...] - m_new)
        for kv in range(N_KV):
            w = _dot(jnp.concatenate([vt[kv * HEAD_DIM:(kv + 1) * HEAD_DIM], ones], axis=0),
                     p[:, kv * half:(kv + 1) * half])
            acc_ref[kv] = w if first else acc_ref[kv] * alpha[:, kv * half:(kv + 1) * half] + w
        m_ref[...] = m_new

    def group(n):
        return 1 + n * C_UNROLL

    absorb(scores(0, 1), 0, 1, first=True)
    if ngrp > 0:
        sa_ref[...] = scores(group(0), C_UNROLL)
        npair = (ngrp - 1) // 2

        def body(n, carry):
            g = 2 * n
            sb_ref[...] = scores(group(g + 1), C_UNROLL)
            absorb(sa_ref[...], group(g), C_UNROLL)
            sa_ref[...] = scores(group(g + 2), C_UNROLL)
            absorb(sb_ref[...], group(g + 1), C_UNROLL)
            return carry

        lax.fori_loop(0, npair, body, 0)
        g = 2 * npair
        if ngrp - g == 2:
            sb_ref[...] = scores(group(g + 1), C_UNROLL)
            absorb(sa_ref[...], group(g), C_UNROLL)
            absorb(sb_ref[...], group(g + 1), C_UNROLL)
        else:
            absorb(sa_ref[...], group(g), C_UNROLL)
    for j in range(1 + ngrp * C_UNROLL, nb):
        absorb(scores(j, 1), j, 1)


def _sweep_scratch(nmaps):
    W = nmaps * BLOCK
    return [pltpu.VMEM((LANES, W), BF16), pltpu.VMEM((C_UNROLL * BLOCK, W), F32), pltpu.VMEM((C_UNROLL * BLOCK, W), F32),
            pltpu.VMEM((1, W), F32), pltpu.VMEM((N_KV, HEAD_DIM + 16, W // 2), F32)]


def _attn_b_kernel(q_ref, k_ref, vt_ref, o_ref, qs_ref, sa_ref, sb_ref, m_ref, acc_ref):
    lane = lax.broadcasted_iota(jnp.int32, (1, LANES), 1)
    head_b = (lane // (HEAD_DIM // 2)) % 2
    qs_ref[...] = _stack_maps(q_ref, [[head_b == kv] for kv in range(N_KV)])
    _attn_sweep(qs_ref, k_ref, vt_ref, lambda j: None, sa_ref, sb_ref, m_ref, acc_ref)
    ot = [acc_ref[kv][0:HEAD_DIM] / acc_ref[kv][HEAD_DIM:HEAD_DIM + 1] for kv in range(N_KV)]
    for g in range(2):
        yt = jnp.concatenate([ot[kv][:, g * BLOCK:(g + 1) * BLOCK] for kv in range(N_KV)], axis=0)
        o_ref[0, :, g * LANES:(g + 1) * LANES] = yt.T.astype(BF16)


def _attn_b(qkv, vt):
    bn, L, _ = qkv.shape
    nb = L // BLOCK
    return pl.pallas_call(
        _attn_b_kernel,
        grid=(bn, nb),
        in_specs=[
            pl.BlockSpec((1, BLOCK, 2 * LANES), lambda b, i: (b, i, 2)),
            pl.BlockSpec((1, L, LANES), lambda b, i: (b, 0, 6)),
            pl.BlockSpec((1, LANES, L), lambda b, i: (b, 0, 0)),
        ],
        out_specs=pl.BlockSpec((1, BLOCK, MIX_W), lambda b, i: (b, i, 0)),
        out_shape=jax.ShapeDtypeStruct((bn, L, MIX_W), BF16),
        scratch_shapes=_sweep_scratch(4),
        compiler_params=_params(("parallel", "parallel")),
        name="attn_rope",
    )(qkv, qkv, vt)


def _attn_c_kernel(sc_ref, q_ref, k_ref, vt_ref, bias_ref, gsub_ref, hsel_ref, o_ref,
                   qs_ref, sa_ref, sb_ref, m_ref, acc_ref):
    i = pl.program_id(1)
    lam = sc_ref[0]
    lane = lax.broadcasted_iota(jnp.int32, (1, LANES), 1)
    grp = lane // C_DIM
    qs_ref[...] = _stack_maps(q_ref, [[grp == 2 * kv, grp == 2 * kv + 1] for kv in range(N_KV)])

    def bias(j):
        t = jnp.clip(j - i + 2, 0, 4)
        tiles = [bias_ref[t, h] for h in range(N_HEADS)]
        return jnp.concatenate([tiles[h] for h in range(N_HEADS) for _ in range(2)], axis=1)

    _attn_sweep(qs_ref, k_ref, vt_ref, bias, sa_ref, sb_ref, m_ref, acc_ref)
    ot = [acc_ref[kv][0:HEAD_DIM] / acc_ref[kv][HEAD_DIM:HEAD_DIM + 1] for kv in range(N_KV)]
    for g in range(2):
        dt = [ot[kv][:, (2 * g) * BLOCK:(2 * g + 1) * BLOCK] - lam * ot[kv][:, (2 * g + 1) * BLOCK:(2 * g + 2) * BLOCK]
              for kv in range(N_KV)]
        y = jnp.concatenate(dt, axis=0).T
        ms = _split_dot(y * y, hsel_ref[...]) * (1.0 / HEAD_DIM)
        y = y * lax.rsqrt(ms + RMS_EPS) * gsub_ref[...] * sc_ref[1]
        o_ref[0, :, g * LANES:(g + 1) * LANES] = y.astype(BF16)


def _attn_c(qkv, vt, scal, bias5t, gsub, hsel):
    bn, L, _ = qkv.shape
    nb = L // BLOCK
    return pl.pallas_call(
        _attn_c_kernel,
        grid=(bn, nb),
        in_specs=[
            pl.BlockSpec(memory_space=pltpu.SMEM),
            pl.BlockSpec((1, BLOCK, 2 * LANES), lambda b, i: (b, i, 4)),
            pl.BlockSpec((1, L, LANES), lambda b, i: (b, 0, 10)),
            pl.BlockSpec((1, LANES, L), lambda b, i: (b, 0, 0)),
            _const_spec((5, N_HEADS, BLOCK, BLOCK)),
            _const_spec((1, LANES)),
            _const_spec((LANES, LANES)),
        ],
        out_specs=pl.BlockSpec((1, BLOCK, MIX_W), lambda b, i: (b, i, 0)),
        out_shape=jax.ShapeDtypeStruct((bn, L, MIX_W), BF16),
        scratch_shapes=_sweep_scratch(8),
        compiler_params=_params(("parallel", "parallel")),
        name="attn_diff",
    )(scal, qkv, qkv, vt, bias5t, gsub, hsel)


def _split3_dot(sel, x):
    x1 = x.astype(BF16)
    r1 = x - x1.astype(F32)
    x2 = r1.astype(BF16)
    x3 = (r1 - x2.astype(F32)).astype(BF16)
    return _dot(sel, x1) + _dot(sel, x2) + _dot(sel, x3)


def _hgrn_chunk(forward, q, kk, g, v, tri_ref, bd_ref, st_ref):
    W = MIX_W
    b = _split3_dot(tri_ref[...], g)
    tot = b[BLOCK - 1:BLOCK, :]
    if forward:
        x, e = b, tot - b
    else:
        x, e = tot - b + g, b - g
    sub = lax.broadcasted_iota(jnp.int32, (1, 8, 1), 1)
    x3, q3, k3, v3 = (a.reshape(BLOCK // 8, 8, W) for a in (x, q, kk, v))
    o = _dot((q * kk).astype(BF16), bd_ref[...]) * v
    for d in range(1, 8):
        sh = d if forward else 8 - d
        ok = (sub >= d) if forward else (sub + d < 8)
        e_d = jnp.exp2(jnp.where(ok, x3 - pltpu.roll(x3, sh, 1), NEG)) * q3 * pltpu.roll(k3, sh, 1)
        o = o + _dot(e_d.reshape(BLOCK, W).astype(BF16), bd_ref[...]) * pltpu.roll(v3, sh, 1).reshape(BLOCK, W)
    row = lax.broadcasted_iota(jnp.int32, (BLOCK, 1), 0)
    qcol = lax.broadcasted_iota(jnp.int32, (1, N_HEADS * BLOCK), 1) % BLOCK
    lane_head = lax.broadcasted_iota(jnp.int32, (1, W), 1) // HEAD_DIM
    at = jnp.zeros((BLOCK, N_HEADS * BLOCK), F32)
    for m in (8, 16, 32, 64):
        mid = m - 1 if forward else m
        xm = jnp.concatenate([jnp.broadcast_to(x[i + mid:i + mid + 1, :], (2 * m, W))
                              for i in range(0, BLOCK, 2 * m)], axis=0)
        second = (row // m) % 2 == 1
        qside = second if forward else jnp.logical_not(second)
        qt = (q * jnp.exp2(jnp.where(qside, x - xm, NEG))).astype(BF16)
        kt = (kk * jnp.exp2(jnp.where(qside, NEG, xm - x))).astype(BF16)
        qstack = jnp.concatenate([jnp.where(lane_head == h, qt, jnp.zeros_like(qt)) for h in range(N_HEADS)], axis=0)
        a = _dot_nt(kt, qstack)
        at = at + jnp.where(row // (2 * m) == qcol // (2 * m), a, 0.0)
    vt = v.T.astype(BF16)
    ot = _dot(vt, at.astype(BF16))
    o = o + jnp.concatenate([ot[h * HEAD_DIM:(h + 1) * HEAD_DIM, h * BLOCK:(h + 1) * BLOCK]
                             for h in range(N_HEADS)], axis=0).T
    qd = (q * jnp.exp2(x)).astype(BF16)
    kd = (kk * jnp.exp2(e)).astype(BF16)
    gam = jnp.exp2(tot)
    hmask = (lax.broadcasted_iota(jnp.int32, (LANES, LANES), 0) // HEAD_DIM
             == lax.broadcasted_iota(jnp.int32, (LANES, LANES), 1) // HEAD_DIM)
    inter = []
    for hf in range(2):
        lanes = slice(hf * LANES, (hf + 1) * LANES)
        st = st_ref[hf]
        inter.append(_dot_nt(qd[:, lanes], st.astype(BF16)))
        st_ref[hf] = st * gam[:, lanes] + jnp.where(hmask, _dot(vt[lanes, :], kd[:, lanes]), 0.0)
    return o + jnp.concatenate(inter, axis=1)


def _hgrn_direction(forward, first_tile, q_ref, z_ref, v_ref, lb_ref, tri_ref, bd_ref, o_ref, st_ref):
    R = q_ref.shape[1]
    z = z_ref[0]
    q = q_ref[0] * (HEAD_DIM ** -0.5)
    v = v_ref[0]
    lb = lb_ref[...]
    row = lax.broadcasted_iota(jnp.int32, (R, 1), 0)
    f = lb + (1.0 - lb) * jax.nn.sigmoid(z)
    g = jnp.log(jnp.maximum(f, F_FLOOR)) * LOG2E
    kk = (1.0 - lb) * jax.nn.sigmoid(-z)
    kk = jnp.where(row < jnp.where(first_tile, FRONT, 0), 0.0, kk)
    nblk = R // BLOCK
    for n in range(nblk):
        r0 = (n if forward else nblk - 1 - n) * BLOCK
        rows = slice(r0, r0 + BLOCK)
        o_ref[0, rows, :] = _hgrn_chunk(forward, q[rows], kk[rows], g[rows], v[rows], tri_ref, bd_ref, st_ref)


def _hgrn_kernel(qf_ref, zf_ref, vf_ref, qb_ref, zb_ref, vb_ref, lb_ref, tri_ref, bd_ref, of_ref, ob_ref, st_ref):
    j = pl.program_id(1)
    nt = pl.num_programs(1)

    @pl.when(j == 0)
    def _():
        st_ref[...] = jnp.zeros_like(st_ref)

    _hgrn_direction(True, j == 0, qf_ref, zf_ref, vf_ref, lb_ref.at[0], tri_ref, bd_ref, of_ref, st_ref.at[0])
    _hgrn_direction(False, j == nt - 1, qb_ref, zb_ref, vb_ref, lb_ref.at[1], tri_ref, bd_ref, ob_ref, st_ref.at[1])


def _hgrn(dproj, lb2, tri, bd):
    bn, L, _ = dproj.shape
    nt = L // ROW_TILE
    fw = lambda c: pl.BlockSpec((1, ROW_TILE, MIX_W), lambda b, j: (b, j, c))
    bw = lambda c: pl.BlockSpec((1, ROW_TILE, MIX_W), lambda b, j: (b, nt - 1 - j, c))
    return pl.pallas_call(
        _hgrn_kernel,
        grid=(bn, nt),
        in_specs=[fw(0), fw(1), fw(3), bw(0), bw(2), bw(3), _const_spec((2, 1, MIX_W)), _const_spec((BLOCK, BLOCK)),
                  _const_spec((MIX_W, MIX_W))],
        out_specs=[pl.BlockSpec((1, ROW_TILE, MIX_W), lambda b, j: (b, j, 0)),
                   pl.BlockSpec((1, ROW_TILE, MIX_W), lambda b, j: (b, nt - 1 - j, 0))],
        out_shape=[jax.ShapeDtypeStruct((bn, L, MIX_W), F32)] * 2,
        scratch_shapes=[pltpu.VMEM((2, 2, LANES, LANES), F32)],
        compiler_params=_params(("parallel", "arbitrary")),
        name="hgrn",
    )(dproj, dproj, dproj, dproj, dproj, dproj, lb2, tri, bd)


def _merge_kernel(h_ref, ya_ref, yb_ref, yc_ref, of_ref, ob_ref, dg_ref, g_ref, gout_ref, bd_ref,
                  wgz_ref, wb_ref, wo_ref, o_ref):
    h = h_ref[0]
    u = _rms(h, g_ref[...]).astype(BF16)
    od = of_ref[0] + ob_ref[0]
    ms = _split_dot(od * od, bd_ref[...]) * (1.0 / HEAD_DIM)
    yd = (od * lax.rsqrt(ms + RMS_EPS) * gout_ref[...] * jax.nn.silu(dg_ref[0])).astype(BF16)
    ys = (ya_ref[0], yb_ref[0], yc_ref[0], yd)
    merged = None
    for n in range(N_BRANCH):
        gate = jax.nn.sigmoid(_dot(u, wgz_ref[:, n * D_MODEL:(n + 1) * D_MODEL]))
        term = gate * _dot(ys[n], wb_ref[n])
        merged = term if merged is None else merged + term
    o_ref[0] = h + _dot(merged.astype(BF16), wo_ref[...])


def _merge(h, ya, yb, yc, of, ob, dproj, g, gout, bd, wgz, wb, wo):
    bn, L, _ = h.shape
    rows = bn * L
    tm = _dense_tile(rows)
    flat = lambda a: a.reshape(1, rows, a.shape[-1])
    h, ya, yb, yc, of, ob, dproj = (flat(a) for a in (h, ya, yb, yc, of, ob, dproj))
    row = lambda w, c=0: pl.BlockSpec((1, tm, w), lambda i: (0, i, c))
    out = pl.pallas_call(
        _merge_kernel,
        grid=(rows // tm,),
        in_specs=[row(D_MODEL), row(MIX_W), row(MIX_W), row(MIX_W), row(MIX_W), row(MIX_W), row(MIX_W, 4),
                  _const_spec((1, D_MODEL)), _const_spec((1, MIX_W)), _const_spec((MIX_W, MIX_W)),
                  _const_spec((D_MODEL, N_BRANCH * D_MODEL)), _const_spec((N_BRANCH, MIX_W, D_MODEL)),
                  _const_spec((D_MODEL, D_MODEL))],
        out_specs=row(D_MODEL),
        out_shape=jax.ShapeDtypeStruct((1, rows, D_MODEL), F32),
        compiler_params=_params(("parallel",)),
        name="merge",
    )(h, ya, yb, yc, of, ob, dproj, g, gout, bd, wgz, wb, wo)
    return out.reshape(bn, L, D_MODEL)


def _ffn_kernel(h_ref, g_ref, wg_ref, wu_ref, wd_ref, o_ref):
    h = h_ref[0]
    u = _rms(h, g_ref[...]).astype(BF16)
    half = D_FF // 2
    out = h
    for c in range(2):
        a = _dot(u, wg_ref[:, c * half:(c + 1) * half])
        t = (jax.nn.silu(a) * _dot(u, wu_ref[:, c * half:(c + 1) * half])).astype(BF16)
        out = out + _dot(t, wd_ref[c * half:(c + 1) * half, :])
    o_ref[0] = out


def _ffn(h, g, wg, wu, wd):
    bn, L, _ = h.shape
    rows = bn * L
    tm = _dense_tile(rows)
    row = pl.BlockSpec((1, tm, D_MODEL), lambda i: (0, i, 0))
    out = pl.pallas_call(
        _ffn_kernel,
        grid=(rows // tm,),
        in_specs=[row, _const_spec((1, D_MODEL)), _const_spec((D_MODEL, D_FF)), _const_spec((D_MODEL, D_FF)),
                  _const_spec((D_FF, D_MODEL))],
        out_specs=row,
        out_shape=jax.ShapeDtypeStruct((1, rows, D_MODEL), F32),
        compiler_params=_params(("parallel",)),
        name="ffn",
    )(h.reshape(1, rows, D_MODEL), g, wg, wu, wd)
    return out.reshape(bn, L, D_MODEL)


def _final_kernel(h_ref, g_ref, o_ref):
    o_ref[0] = _rms(h_ref[0], g_ref[...])


def _final_norm(h, g):
    bn, L, _ = h.shape
    S = L - META_END
    return pl.pallas_call(
        _final_kernel,
        grid=(bn, S // BLOCK),
        in_specs=[pl.BlockSpec((1, BLOCK, D_MODEL), lambda b, i: (b, i + META_END // BLOCK, 0)), _const_spec((1, D_MODEL))],
        out_specs=pl.BlockSpec((1, BLOCK, D_MODEL), lambda b, i: (b, i, 0)),
        out_shape=jax.ShapeDtypeStruct((bn, S, D_MODEL), F32),
        compiler_params=_params(("parallel", "parallel")),
        name="final_norm",
    )(h, g)


def _t5_bucket(rel):
    half = N_BUCKETS // 2
    exact = half // 2
    n = jnp.abs(rel)
    nf = jnp.maximum(n, exact).astype(F32)
    big = exact + (jnp.log(nf / exact) / math.log(MAX_DIST / exact) * (half - exact)).astype(jnp.int32)
    big = jnp.minimum(big, half - 1)
    return jnp.where(rel > 0, half, 0) + jnp.where(n < exact, n, big)


def _lookup(table, bucket):
    hit = bucket[None, ..., None] == jnp.arange(N_BUCKETS)
    return jnp.sum(jnp.where(hit, table.T.reshape((N_HEADS,) + (1,) * bucket.ndim + (N_BUCKETS,)), 0.0), axis=-1)


def _bias_tables(rel_bias):
    bias_a = rel_bias[:, :N_HEADS].astype(F32) * LOG2E
    bias_c = rel_bias[:, N_HEADS:].astype(F32) * LOG2E
    r = jnp.arange(BLOCK)[:, None]
    rel_band = (jnp.arange(3 * BLOCK)[None, :] - BLOCK) - r
    bband = jnp.where(jnp.abs(rel_band)[None] <= WINDOW, _lookup(bias_a, _t5_bucket(rel_band)), NEG)
    lane = jnp.arange(BLOCK)[None, :]
    rel_m = lane[None] - (jnp.arange(3)[:, None, None] * BLOCK + r[None])
    bmeta = jnp.where((lane >= FRONT)[None, None], jnp.moveaxis(_lookup(bias_a, _t5_bucket(rel_m)), 0, 1), NEG)
    rel5t = (jnp.arange(5)[:, None, None] - 2) * BLOCK + r[None] - lane[None]
    bias5t = jnp.moveaxis(_lookup(bias_c, _t5_bucket(rel5t)), 0, 1)
    return bband, bmeta, bias5t


def _rope_layout():
    return np.concatenate([np.zeros(32, int), np.ones(32, int)] * 2)


def _rope_cols(w, heads):
    lead = w.shape[:-1]
    w = w.reshape(lead + (heads // 2, 2, 2, 2, HEAD_DIM // 4))
    if heads == N_HEADS:
        w = jnp.moveaxis(w, (-5, -4, -3, -2), (-3, -5, -2, -4))
    else:
        w = jnp.moveaxis(w, (-4, -3, -2), (-3, -2, -4))
    return w.reshape(lead + (heads * HEAD_DIM,))


def _pair_cols(w):
    lead = w.shape[:-1]
    return jnp.swapaxes(w.reshape(lead + (N_KV, 2, HEAD_DIM)), -3, -2).reshape(lead + (N_HEADS * HEAD_DIM,))


def _rope_tables(L, S):
    rows = S // GRID_W
    row = jnp.concatenate([jnp.zeros((FRONT,), jnp.int32), -jnp.ones((N_META,), jnp.int32),
                           jnp.repeat(jnp.arange(rows, dtype=jnp.int32), GRID_W)]).astype(F32)
    col = jnp.concatenate([jnp.zeros((FRONT,), jnp.int32), jnp.arange(N_META, dtype=jnp.int32),
                           jnp.tile(jnp.arange(GRID_W, dtype=jnp.int32), rows)]).astype(F32)
    half = HEAD_DIM // 2
    inv = ROPE_THETA ** (-jnp.arange(0, half, 2, dtype=F32) / half)
    ang = jnp.concatenate([row[:, None] * inv[None, :], col[:, None] * inv[None, :]], axis=-1)
    cos_t = jnp.tile(jnp.cos(ang), (1, 4))
    sin_t = jnp.concatenate([-jnp.tile(jnp.sin(ang), (1, 2)), jnp.tile(jnp.sin(ang), (1, 2))], axis=-1)
    return cos_t, sin_t


def _rope_gain(g):
    g = jnp.swapaxes(g.astype(F32).reshape(2, 2, HEAD_DIM // 4), 0, 1).reshape(2, 1, HEAD_DIM // 2)
    return jnp.broadcast_to(g, (2, 2, HEAD_DIM // 2)).reshape(LANES)


def _qkv_weight(wl):
    a, b, c = wl[:, 0:512], wl[:, 512:1024], wl[:, 1024:1536]
    return jnp.concatenate([_pair_cols(a[:, :256]), a[:, 256:],
                            _rope_cols(b[:, :256], N_HEADS), _rope_cols(b[:, 256:384], N_KV), b[:, 384:],
                            _pair_cols(c[:, :256]), c[:, 256:]], axis=1)


def kernel(x, meta_tokens, rel_bias, hgrn_lb_logits, ln_mix, w_in, attn_sink, qk_norm_q, qk_norm_k, diff_lambda,
           diff_subnorm, hgrn_out_norm, w_branch, w_out, ln_ffn, w_ffn_gate, w_ffn_up, w_ffn_down, ln_final):
    bn, S, _ = x.shape
    L = META_END + S
    depth = w_in.shape[0]
    assert L % ROW_TILE == 0 and S % GRID_W == 0
    h = jnp.concatenate([jnp.zeros((bn, FRONT, D_MODEL), x.dtype),
                         jnp.broadcast_to(meta_tokens.astype(x.dtype)[None], (bn, N_META, D_MODEL)), x], axis=1)
    bband, bmeta, bias5t = _bias_tables(rel_bias)
    cos_t, sin_t = _rope_tables(L, S)
    rhead = _rope_layout()
    lane = np.arange(LANES)
    hsel_rope = jnp.asarray(rhead[:, None] == rhead[None, :], BF16)
    hsel_half = jnp.asarray((lane[:, None] // HEAD_DIM) == (lane[None, :] // HEAD_DIM), BF16)
    w256 = np.arange(MIX_W)
    bd = jnp.asarray((w256[:, None] // HEAD_DIM) == (w256[None, :] // HEAD_DIM), BF16)
    tri = jnp.asarray(np.tril(np.ones((BLOCK, BLOCK))), BF16)
    cs = np.ones((1, QKV_W), np.float32)
    cs[0, 0:256] = HEAD_DIM ** -0.5 * LOG2E
    cs[0, 1024:1280] = C_DIM ** -0.5 * LOG2E
    cs = jnp.asarray(cs)
    lb_p = jax.nn.softmax(hgrn_lb_logits.astype(F32), axis=1)
    lb_all = jnp.cumsum(lb_p, axis=1) - lb_p[:, :1]

    for l in range(depth):
        wl = w_in[l]
        wqkv = _qkv_weight(wl).astype(BF16)
        wd = wl[:, QKV_W:QKV_W + D_W].astype(BF16)
        wgz = wl[:, QKV_W + D_W:].astype(BF16)
        gn = jnp.stack([_rope_gain(qk_norm_q[l]) * (HEAD_DIM ** -0.5 * LOG2E)] * 2 + [_rope_gain(qk_norm_k[l])])
        qkv, dproj, vtb, vtc = _in_proj(h, ln_mix[l][None], wqkv, wd, cs, cos_t, sin_t, gn, hsel_rope)
        ya = _attn_a(qkv, attn_sink[l].astype(F32) * LOG2E, bband, bmeta)
        yb = _attn_b(qkv, vtb)
        lam_init = 0.8 - 0.6 * math.exp(-0.3 * l)
        lam_p = diff_lambda[l].astype(F32)
        lam = jnp.exp(jnp.sum(lam_p[0] * lam_p[1])) - jnp.exp(jnp.sum(lam_p[2] * lam_p[3])) + lam_init
        scal = jnp.stack([lam, jnp.asarray(1.0 - lam_init, F32)])
        gsub = jnp.tile(diff_subnorm[l].astype(F32), 2)[None]
        yc = _attn_c(qkv, vtc, scal, bias5t, gsub, hsel_half)
        lb2 = lb_all[:, l][:, None, :]
        of, ob = _hgrn(dproj, lb2, tri, bd)
        wb = w_branch[l].astype(BF16)
        wb_att = jnp.swapaxes(wb[:3].reshape(3, N_KV, 2, HEAD_DIM, D_MODEL), 1, 2).reshape(3, MIX_W, D_MODEL)
        wb = jnp.concatenate([wb_att, wb[3:]], axis=0)
        gout = jnp.tile(hgrn_out_norm[l].astype(F32), N_HEADS)[None]
        h = _merge(h, ya, yb, yc, of, ob, dproj, ln_mix[l][None], gout, bd, wgz, wb, w_out[l].astype(BF16))
        h = _ffn(h, ln_ffn[l][None], w_ffn_gate[l].astype(BF16), w_ffn_up[l].astype(BF16), w_ffn_down[l].astype(BF16))
    return _final_norm(h, ln_final[None])
```

```python
import math

import jax
import jax.numpy as jnp
import numpy as np
from jax import lax
from jax.experimental import pallas as pl
from jax.experimental.pallas import tpu as pltpu

D_MODEL = 1024
N_BRANCH = 4
MIX_W = D_MODEL // N_BRANCH
HEAD_DIM = 64
BLOCK = 128
WINDOW = 128
ROPE_THETA = 10000.0
GRID_W = 64
N_HEADS = 4
N_KV = 2
C_DIM = 32
N_META = 16
FRONT = (-N_META) % BLOCK
META_END = FRONT + N_META
N_BUCKETS = 32
MAX_DIST = 128
D_FF = -(-(8 * D_MODEL) // (3 * 256)) * 256
RMS_EPS = 1e-6
NEG = -1e30
F_FLOOR = 1e-30
LOG2E = 1.4426950408889634
LANES = 128
ROW_TILE = 384
DENSE_TILES = (768, 384, 128)
C_UNROLL = 4
Q_TILE = 384
QKV_W = 3 * 512
D_W = 5 * MIX_W
VMEM_LIMIT = 56 * 1024 * 1024

F32 = jnp.float32
BF16 = jnp.bfloat16


def _dot(a, b):
    return jnp.dot(a, b, preferred_element_type=F32)


def _dot_nt(a, b):
    return lax.dot_general(a, b, (((1,), (1,)), ((), ())), preferred_element_type=F32)


def _split_dot(x, sel):
    hi = x.astype(BF16)
    lo = (x - hi.astype(F32)).astype(BF16)
    return _dot(hi, sel) + _dot(lo, sel)


def _rms(x, g):
    return x * lax.rsqrt(jnp.mean(x * x, axis=-1, keepdims=True) + RMS_EPS) * g


def _params(sem):
    return pltpu.CompilerParams(dimension_semantics=sem, vmem_limit_bytes=VMEM_LIMIT)


def _const_spec(shape):
    nd = len(shape)
    return pl.BlockSpec(shape, lambda *_: (0,) * nd, pipeline_mode=pl.Buffered(1))


def _dense_tile(rows):
    return next(t for t in DENSE_TILES if rows % t == 0)


def _in_proj_kernel(h_ref, g_ref, wqkv_ref, wd_ref, cs_ref, cos_ref, sin_ref, gn_ref, hsel_ref,
                    qkv_ref, d_ref, vtb_ref, vtc_ref):
    u = _rms(h_ref[0], g_ref[...]).astype(BF16)
    acc = _dot(u, wqkv_ref[...]) * cs_ref[...]
    qkv_ref[0, :, 0:512] = acc[:, 0:512].astype(BF16)
    for t in range(3):
        c0 = 512 + LANES * t
        xt = acc[:, c0:c0 + LANES]
        ms = _split_dot(xt * xt, hsel_ref[...]) * (1.0 / HEAD_DIM)
        xt = xt * lax.rsqrt(ms + RMS_EPS) * gn_ref[t:t + 1, :]
        xt = xt * cos_ref[...] + pltpu.roll(xt, LANES // 2, 1) * sin_ref[...]
        qkv_ref[0, :, c0:c0 + LANES] = xt.astype(BF16)
    qkv_ref[0, :, 896:QKV_W] = acc[:, 896:QKV_W].astype(BF16)
    vtb_ref[0] = acc[:, 896:1024].T.astype(BF16)
    vtc_ref[0] = acc[:, 1408:QKV_W].T.astype(BF16)
    d_ref[0] = _dot(u, wd_ref[...])


def _in_proj(h, g, wqkv, wd, cs, cos_t, sin_t, gn, hsel):
    bn, L, _ = h.shape
    nt = L // ROW_TILE
    return pl.pallas_call(
        _in_proj_kernel,
        grid=(bn, nt),
        in_specs=[
            pl.BlockSpec((1, ROW_TILE, D_MODEL), lambda b, i: (b, i, 0)),
            _const_spec((1, D_MODEL)),
            _const_spec((D_MODEL, QKV_W)),
            _const_spec((D_MODEL, D_W)),
            _const_spec((1, QKV_W)),
            pl.BlockSpec((ROW_TILE, LANES), lambda b, i: (i, 0)),
            pl.BlockSpec((ROW_TILE, LANES), lambda b, i: (i, 0)),
            _const_spec((3, LANES)),
            _const_spec((LANES, LANES)),
        ],
        out_specs=[
            pl.BlockSpec((1, ROW_TILE, QKV_W), lambda b, i: (b, i, 0)),
            pl.BlockSpec((1, ROW_TILE, D_W), lambda b, i: (b, i, 0)),
            pl.BlockSpec((1, LANES, ROW_TILE), lambda b, i: (b, 0, i)),
            pl.BlockSpec((1, LANES, ROW_TILE), lambda b, i: (b, 0, i)),
        ],
        out_shape=[jax.ShapeDtypeStruct((bn, L, QKV_W), BF16), jax.ShapeDtypeStruct((bn, L, D_W), F32),
                   jax.ShapeDtypeStruct((bn, LANES, L), BF16), jax.ShapeDtypeStruct((bn, LANES, L), BF16)],
        compiler_params=_params(("parallel", "parallel")),
        name="in_proj",
    )(h, g, wqkv, wd, cs, cos_t, sin_t, gn, hsel)


def _stack_q(q_ref, masks):
    parts = []
    for t in range(2):
        qt = q_ref[0, :, t * LANES:(t + 1) * LANES]
        for m in masks:
            parts.append(jnp.where(m, qt, jnp.zeros_like(qt)))
    return jnp.concatenate(parts, axis=0)


def _blk(ref, j):
    if isinstance(j, int):
        return ref[0, j * BLOCK:(j + 1) * BLOCK, :]
    return ref[0, pl.ds(pl.multiple_of(j * BLOCK, BLOCK), BLOCK), :]


def _attn_a_kernel(sink_ref, q_ref, k_ref, v_ref, bband_ref, bmeta_ref, o_ref):
    i = pl.program_id(1)
    nb = pl.num_programs(1)
    prev = jnp.maximum(i - 1, 0)
    nxt = jnp.minimum(i + 1, nb - 1)
    kband = jnp.concatenate([_blk(k_ref, prev), _blk(k_ref, i), _blk(k_ref, nxt)], axis=0)
    vband = jnp.concatenate([_blk(v_ref, prev), _blk(v_ref, i), _blk(v_ref, nxt)], axis=0)
    kmeta = k_ref[0, 0:BLOCK, :]
    vmeta = v_ref[0, 0:BLOCK, :]
    col = lax.broadcasted_iota(jnp.int32, (1, 3 * BLOCK), 1)
    lo = jnp.where(i == 0, 2 * BLOCK, jnp.where(i == 1, BLOCK, 0))
    hi = jnp.where(i == nb - 1, 2 * BLOCK, 3 * BLOCK)
    dead = (col < lo) | (col >= hi)
    lane = lax.broadcasted_iota(jnp.int32, (1, LANES), 1)
    outs = []
    for kv in range(N_KV):
        qs = _stack_q(q_ref, [(lane >= HEAD_DIM) == bool(kv)])
        bb = jnp.concatenate([bband_ref[2 * kv], bband_ref[2 * kv + 1]], axis=0)
        bm = jnp.concatenate([bmeta_ref[0, 2 * kv], bmeta_ref[0, 2 * kv + 1]], axis=0)
        sb = jnp.where(dead, NEG, _dot_nt(qs, kband) + bb)
        sm = _dot_nt(qs, kmeta) + bm
        sk = jnp.concatenate([jnp.full((BLOCK, 1), sink_ref[2 * kv], F32),
                              jnp.full((BLOCK, 1), sink_ref[2 * kv + 1], F32)], axis=0)
        m = jnp.maximum(jnp.maximum(jnp.max(sb, axis=-1, keepdims=True), jnp.max(sm, axis=-1, keepdims=True)), sk)
        pb = jnp.exp2(sb - m)
        pm = jnp.exp2(sm - m)
        den = jnp.sum(pb, axis=-1, keepdims=True) + jnp.sum(pm, axis=-1, keepdims=True) + jnp.exp2(sk - m)
        outs.append((_dot(pb.astype(BF16), vband) + _dot(pm.astype(BF16), vmeta)) / den)
    for t in range(2):
        y = jnp.where(lane < HEAD_DIM, outs[0][t * BLOCK:(t + 1) * BLOCK], outs[1][t * BLOCK:(t + 1) * BLOCK])
        o_ref[0, :, t * LANES:(t + 1) * LANES] = y.astype(BF16)


def _attn_a(qkv, sink, bband, bmeta):
    bn, L, _ = qkv.shape
    nb = L // BLOCK
    return pl.pallas_call(
        _attn_a_kernel,
        grid=(bn, nb),
        in_specs=[
            pl.BlockSpec(memory_space=pltpu.SMEM),
            pl.BlockSpec((1, BLOCK, 2 * LANES), lambda b, i: (b, i, 0)),
            pl.BlockSpec((1, L, LANES), lambda b, i: (b, 0, 2)),
            pl.BlockSpec((1, L, LANES), lambda b, i: (b, 0, 3)),
            _const_spec((N_HEADS, BLOCK, 3 * BLOCK)),
            pl.BlockSpec((1, N_HEADS, BLOCK, BLOCK), lambda b, i: (jnp.minimum(i, 2), 0, 0, 0)),
        ],
        out_specs=pl.BlockSpec((1, BLOCK, MIX_W), lambda b, i: (b, i, 0)),
        out_shape=jax.ShapeDtypeStruct((bn, L, MIX_W), BF16),
        compiler_params=_params(("parallel", "parallel")),
        name="attn_window",
    )(sink, qkv, qkv, qkv, bband, bmeta)


def _stack_maps(q_ref, masks_per_kv):
    parts = []
    for masks in masks_per_kv:
        for t in range(2):
            qt = q_ref[0, :, t * LANES:(t + 1) * LANES].astype(F32)
            for m in masks:
                parts.append(jnp.where(m, qt, 0.0).T.astype(BF16))
    return jnp.concatenate(parts, axis=1)


def _attn_sweep(qs_ref, k_ref, vt_ref, bias_fn, sa_ref, sb_ref, m_ref, acc_ref):
    nb = k_ref.shape[1] // BLOCK
    W = qs_ref.shape[1]
    half = W // 2
    ngrp = (nb - 1) // C_UNROLL
    keyrow = lax.broadcasted_iota(jnp.int32, (BLOCK, 1), 0)

    def rows(ref, j0, width):
        if isinstance(j0, int):
            return ref[0, j0 * BLOCK:(j0 + width) * BLOCK, :]
        return ref[0, pl.ds(pl.multiple_of(j0 * BLOCK, BLOCK), width * BLOCK), :]

    def cols(ref, j0, width):
        if isinstance(j0, int):
            return ref[0, :, j0 * BLOCK:(j0 + width) * BLOCK]
        return ref[0, :, pl.ds(pl.multiple_of(j0 * BLOCK, BLOCK), width * BLOCK)]

    def scores(j0, width):
        s_all = _dot(rows(k_ref, j0, width), qs_ref[...])
        parts = []
        for u in range(width):
            s = s_all[u * BLOCK:(u + 1) * BLOCK]
            b = bias_fn(j0 + u)
            if b is not None:
                s = s + b
            if isinstance(j0, int) and j0 + u == 0:
                s = jnp.where(keyrow >= FRONT, s, NEG)
            parts.append(s)
        return parts[0] if width == 1 else jnp.concatenate(parts, axis=0)

    def absorb(s, j0, width, first=False):
        mg = jnp.max(jnp.max(s.reshape(width * BLOCK // 8, 8, W), axis=0), axis=0, keepdims=True)
        m_new = mg if first else jnp.maximum(m_ref[...], mg)
        p = jnp.exp2(s - m_new).astype(BF16)
        vt = cols(vt_ref, j0, width)
        ones = jnp.ones((16, width * BLOCK), BF16)
        if not first:
            alpha = jnp.exp2(m_ref[...] - m_new)
        for kv in range(N_KV):
            w = _dot(jnp.concatenate([vt[kv * HEAD_DIM:(kv + 1) * HEAD_DIM], ones], axis=0),
                     p[:, kv * half:(kv + 1) * half])
            acc_ref[kv] = w if first else acc_ref[kv] * alpha[:, kv * half:(kv + 1) * half] + w
        m_ref[...] = m_new

    def group(n):
        return 1 + n * C_UNROLL

    absorb(scores(0, 1), 0, 1, first=True)
    if ngrp > 0:
        sa_ref[...] = scores(group(0), C_UNROLL)
        npair = (ngrp - 1) // 2

        def body(n, carry):
            g = 2 * n
            sb_ref[...] = scores(group(g + 1), C_UNROLL)
            absorb(sa_ref[...], group(g), C_UNROLL)
            sa_ref[...] = scores(group(g + 2), C_UNROLL)
            absorb(sb_ref[...], group(g + 1), C_UNROLL)
            return carry

        lax.fori_loop(0, npair, body, 0)
        g = 2 * npair
        if ngrp - g == 2:
            sb_ref[...] = scores(group(g + 1), C_UNROLL)
            absorb(sa_ref[...], group(g), C_UNROLL)
            absorb(sb_ref[...], group(g + 1), C_UNROLL)
        else:
            absorb(sa_ref[...], group(g), C_UNROLL)
    for j in range(1 + ngrp * C_UNROLL, nb):
        absorb(scores(j, 1), j, 1)


def _sweep_scratch(nmaps):
    W = nmaps * Q_TILE
    return [pltpu.VMEM((LANES, W), BF16), pltpu.VMEM((C_UNROLL * BLOCK, W), F32), pltpu.VMEM((C_UNROLL * BLOCK, W), F32),
            pltpu.VMEM((1, W), F32), pltpu.VMEM((N_KV, HEAD_DIM + 16, W // 2), F32)]


def _attn_b_kernel(q_ref, k_ref, vt_ref, o_ref, qs_ref, sa_ref, sb_ref, m_ref, acc_ref):
    lane = lax.broadcasted_iota(jnp.int32, (1, LANES), 1)
    head_b = (lane // (HEAD_DIM // 2)) % 2
    qs_ref[...] = _stack_maps(q_ref, [[head_b == kv] for kv in range(N_KV)])
    _attn_sweep(qs_ref, k_ref, vt_ref, lambda j: None, sa_ref, sb_ref, m_ref, acc_ref)
    ot = [acc_ref[kv][0:HEAD_DIM] / acc_ref[kv][HEAD_DIM:HEAD_DIM + 1] for kv in range(N_KV)]
    for g in range(2):
        yt = jnp.concatenate([ot[kv][:, g * Q_TILE:(g + 1) * Q_TILE] for kv in range(N_KV)], axis=0)
        o_ref[0, :, g * LANES:(g + 1) * LANES] = yt.T.astype(BF16)


def _attn_b(qkv, vt):
    bn, L, _ = qkv.shape
    return pl.pallas_call(
        _attn_b_kernel,
        grid=(bn, L // Q_TILE),
        in_specs=[
            pl.BlockSpec((1, Q_TILE, 2 * LANES), lambda b, i: (b, i, 2)),
            pl.BlockSpec((1, L, LANES), lambda b, i: (b, 0, 6)),
            pl.BlockSpec((1, LANES, L), lambda b, i: (b, 0, 0)),
        ],
        out_specs=pl.BlockSpec((1, Q_TILE, MIX_W), lambda b, i: (b, i, 0)),
        out_shape=jax.ShapeDtypeStruct((bn, L, MIX_W), BF16),
        scratch_shapes=_sweep_scratch(4),
        compiler_params=_params(("parallel", "parallel")),
        name="attn_rope",
    )(qkv, qkv, vt)


def _attn_c_kernel(sc_ref, q_ref, k_ref, vt_ref, bias_ref, gsub_ref, hsel_ref, o_ref,
                   qs_ref, sa_ref, sb_ref, m_ref, acc_ref):
    i = pl.program_id(1)
    lam = sc_ref[0]
    lane = lax.broadcasted_iota(jnp.int32, (1, LANES), 1)
    grp = lane // C_DIM
    qs_ref[...] = _stack_maps(q_ref, [[grp == 2 * kv, grp == 2 * kv + 1] for kv in range(N_KV)])

    def bias(j):
        nsub = Q_TILE // BLOCK
        ts = [jnp.clip(j - (i * nsub + sb) + 2, 0, 4) for sb in range(nsub)]
        per_head = [jnp.concatenate([bias_ref[t, h] for t in ts], axis=1) for h in range(N_HEADS)]
        return jnp.concatenate([per_head[h] for h in range(N_HEADS) for _ in range(2)], axis=1)

    _attn_sweep(qs_ref, k_ref, vt_ref, bias, sa_ref, sb_ref, m_ref, acc_ref)
    ot = [acc_ref[kv][0:HEAD_DIM] / acc_ref[kv][HEAD_DIM:HEAD_DIM + 1] for kv in range(N_KV)]
    for g in range(2):
        dt = [ot[kv][:, (2 * g) * Q_TILE:(2 * g + 1) * Q_TILE] - lam * ot[kv][:, (2 * g + 1) * Q_TILE:(2 * g + 2) * Q_TILE]
              for kv in range(N_KV)]
        y = jnp.concatenate(dt, axis=0).T
        ms = _split_dot(y * y, hsel_ref[...]) * (1.0 / HEAD_DIM)
        y = y * lax.rsqrt(ms + RMS_EPS) * gsub_ref[...] * sc_ref[1]
        o_ref[0, :, g * LANES:(g + 1) * LANES] = y.astype(BF16)


def _attn_c(qkv, vt, scal, bias5t, gsub, hsel):
    bn, L, _ = qkv.shape
    return pl.pallas_call(
        _attn_c_kernel,
        grid=(bn, L // Q_TILE),
        in_specs=[
            pl.BlockSpec(memory_space=pltpu.SMEM),
            pl.BlockSpec((1, Q_TILE, 2 * LANES), lambda b, i: (b, i, 4)),
            pl.BlockSpec((1, L, LANES), lambda b, i: (b, 0, 10)),
            pl.BlockSpec((1, LANES, L), lambda b, i: (b, 0, 0)),
            _const_spec((5, N_HEADS, BLOCK, BLOCK)),
            _const_spec((1, LANES)),
            _const_spec((LANES, LANES)),
        ],
        out_specs=pl.BlockSpec((1, Q_TILE, MIX_W), lambda b, i: (b, i, 0)),
        out_shape=jax.ShapeDtypeStruct((bn, L, MIX_W), BF16),
        scratch_shapes=_sweep_scratch(8),
        compiler_params=_params(("parallel", "parallel")),
        name="attn_diff",
    )(scal, qkv, qkv, vt, bias5t, gsub, hsel)


def _split3_dot(sel, x):
    x1 = x.astype(BF16)
    r1 = x - x1.astype(F32)
    x2 = r1.astype(BF16)
    x3 = (r1 - x2.astype(F32)).astype(BF16)
    return _dot(sel, x1) + _dot(sel, x2) + _dot(sel, x3)


def _hgrn_chunk(forward, q, kk, g, v, tri_ref, bd_ref, st_ref):
    W = MIX_W
    b = _split3_dot(tri_ref[...], g)
    tot = b[BLOCK - 1:BLOCK, :]
    if forward:
        x, e = b, tot - b
    else:
        x, e = tot - b + g, b - g
    sub = lax.broadcasted_iota(jnp.int32, (1, 8, 1), 1)
    x3, q3, k3, v3 = (a.reshape(BLOCK // 8, 8, W) for a in (x, q, kk, v))
    o = _dot((q * kk).astype(BF16), bd_ref[...]) * v
    for d in range(1, 8):
        sh = d if forward else 8 - d
        ok = (sub >= d) if forward else (sub + d < 8)
        e_d = jnp.exp2(jnp.where(ok, x3 - pltpu.roll(x3, sh, 1), NEG)) * q3 * pltpu.roll(k3, sh, 1)
        o = o + _dot(e_d.reshape(BLOCK, W).astype(BF16), bd_ref[...]) * pltpu.roll(v3, sh, 1).reshape(BLOCK, W)
    row = lax.broadcasted_iota(jnp.int32, (BLOCK, 1), 0)
    qcol = lax.broadcasted_iota(jnp.int32, (1, N_HEADS * BLOCK), 1) % BLOCK
    lane_head = lax.broadcasted_iota(jnp.int32, (1, W), 1) // HEAD_DIM
    at = jnp.zeros((BLOCK, N_HEADS * BLOCK), F32)
    for m in (8, 16, 32, 64):
        mid = m - 1 if forward else m
        xm = jnp.concatenate([jnp.broadcast_to(x[i + mid:i + mid + 1, :], (2 * m, W))
                              for i in range(0, BLOCK, 2 * m)], axis=0)
        second = (row // m) % 2 == 1
        qside = second if forward else jnp.logical_not(second)
        qt = (q * jnp.exp2(jnp.where(qside, x - xm, NEG))).astype(BF16)
        kt = (kk * jnp.exp2(jnp.where(qside, NEG, xm - x))).astype(BF16)
        qstack = jnp.concatenate([jnp.where(lane_head == h, qt, jnp.zeros_like(qt)) for h in range(N_HEADS)], axis=0)
        a = _dot_nt(kt, qstack)
        at = at + jnp.where(row // (2 * m) == qcol // (2 * m), a, 0.0)
    vt = v.T.astype(BF16)
    ot = _dot(vt, at.astype(BF16))
    o = o + jnp.concatenate([ot[h * HEAD_DIM:(h + 1) * HEAD_DIM, h * BLOCK:(h + 1) * BLOCK]
                             for h in range(N_HEADS)], axis=0).T
    qd = (q * jnp.exp2(x)).astype(BF16)
    kd = (kk * jnp.exp2(e)).astype(BF16)
    gam = jnp.exp2(tot)
    hmask = (lax.broadcasted_iota(jnp.int32, (LANES, LANES), 0) // HEAD_DIM
             == lax.broadcasted_iota(jnp.int32, (LANES, LANES), 1) // HEAD_DIM)
    inter = []
    for hf in range(2):
        lanes = slice(hf * LANES, (hf + 1) * LANES)
        st = st_ref[hf]
        inter.append(_dot_nt(qd[:, lanes], st.astype(BF16)))
        st_ref[hf] = st * gam[:, lanes] + jnp.where(hmask, _dot(vt[lanes, :], kd[:, lanes]), 0.0)
    return o + jnp.concatenate(inter, axis=1)


def _hgrn_direction(forward, first_tile, q_ref, z_ref, v_ref, lb_ref, tri_ref, bd_ref, o_ref, st_ref):
    R = q_ref.shape[1]
    z = z_ref[0]
    q = q_ref[0] * (HEAD_DIM ** -0.5)
    v = v_ref[0]
    lb = lb_ref[...]
    row = lax.broadcasted_iota(jnp.int32, (R, 1), 0)
    f = lb + (1.0 - lb) * jax.nn.sigmoid(z)
    g = jnp.log(jnp.maximum(f, F_FLOOR)) * LOG2E
    kk = (1.0 - lb) * jax.nn.sigmoid(-z)
    kk = jnp.where(row < jnp.where(first_tile, FRONT, 0), 0.0, kk)
    nblk = R // BLOCK
    for n in range(nblk):
        r0 = (n if forward else nblk - 1 - n) * BLOCK
        rows = slice(r0, r0 + BLOCK)
        o_ref[0, rows, :] = _hgrn_chunk(forward, q[rows], kk[rows], g[rows], v[rows], tri_ref, bd_ref, st_ref)


def _hgrn_kernel(qf_ref, zf_ref, vf_ref, qb_ref, zb_ref, vb_ref, lb_ref, tri_ref, bd_ref, of_ref, ob_ref, st_ref):
    j = pl.program_id(1)
    nt = pl.num_programs(1)

    @pl.when(j == 0)
    def _():
        st_ref[...] = jnp.zeros_like(st_ref)

    _hgrn_direction(True, j == 0, qf_ref, zf_ref, vf_ref, lb_ref.at[0], tri_ref, bd_ref, of_ref, st_ref.at[0])
    _hgrn_direction(False, j == nt - 1, qb_ref, zb_ref, vb_ref, lb_ref.at[1], tri_ref, bd_ref, ob_ref, st_ref.at[1])


def _hgrn(dproj, lb2, tri, bd):
    bn, L, _ = dproj.shape
    nt = L // ROW_TILE
    fw = lambda c: pl.BlockSpec((1, ROW_TILE, MIX_W), lambda b, j: (b, j, c))
    bw = lambda c: pl.BlockSpec((1, ROW_TILE, MIX_W), lambda b, j: (b, nt - 1 - j, c))
    return pl.pallas_call(
        _hgrn_kernel,
        grid=(bn, nt),
        in_specs=[fw(0), fw(1), fw(3), bw(0), bw(2), bw(3), _const_spec((2, 1, MIX_W)), _const_spec((BLOCK, BLOCK)),
                  _const_spec((MIX_W, MIX_W))],
        out_specs=[pl.BlockSpec((1, ROW_TILE, MIX_W), lambda b, j: (b, j, 0)),
                   pl.BlockSpec((1, ROW_TILE, MIX_W), lambda b, j: (b, nt - 1 - j, 0))],
        out_shape=[jax.ShapeDtypeStruct((bn, L, MIX_W), F32)] * 2,
        scratch_shapes=[pltpu.VMEM((2, 2, LANES, LANES), F32)],
        compiler_params=_params(("parallel", "arbitrary")),
        name="hgrn",
    )(dproj, dproj, dproj, dproj, dproj, dproj, lb2, tri, bd)


def _merge_kernel(h_ref, ya_ref, yb_ref, yc_ref, of_ref, ob_ref, dg_ref, g_ref, gout_ref, bd_ref,
                  wgz_ref, wb_ref, wo_ref, o_ref):
    h = h_ref[0]
    u = _rms(h, g_ref[...]).astype(BF16)
    od = of_ref[0] + ob_ref[0]
    ms = _split_dot(od * od, bd_ref[...]) * (1.0 / HEAD_DIM)
    yd = (od * lax.rsqrt(ms + RMS_EPS) * gout_ref[...] * jax.nn.silu(dg_ref[0])).astype(BF16)
    ys = (ya_ref[0], yb_ref[0], yc_ref[0], yd)
    merged = None
    for n in range(N_BRANCH):
        gate = jax.nn.sigmoid(_dot(u, wgz_ref[:, n * D_MODEL:(n + 1) * D_MODEL]))
        term = gate * _dot(ys[n], wb_ref[n])
        merged = term if merged is None else merged + term
    o_ref[0] = h + _dot(merged.astype(BF16), wo_ref[...])


def _merge(h, ya, yb, yc, of, ob, dproj, g, gout, bd, wgz, wb, wo):
    bn, L, _ = h.shape
    rows = bn * L
    tm = _dense_tile(rows)
    flat = lambda a: a.reshape(1, rows, a.shape[-1])
    h, ya, yb, yc, of, ob, dproj = (flat(a) for a in (h, ya, yb, yc, of, ob, dproj))
    row = lambda w, c=0: pl.BlockSpec((1, tm, w), lambda i: (0, i, c))
    out = pl.pallas_call(
        _merge_kernel,
        grid=(rows // tm,),
        in_specs=[row(D_MODEL), row(MIX_W), row(MIX_W), row(MIX_W), row(MIX_W), row(MIX_W), row(MIX_W, 4),
                  _const_spec((1, D_MODEL)), _const_spec((1, MIX_W)), _const_spec((MIX_W, MIX_W)),
                  _const_spec((D_MODEL, N_BRANCH * D_MODEL)), _const_spec((N_BRANCH, MIX_W, D_MODEL)),
                  _const_spec((D_MODEL, D_MODEL))],
        out_specs=row(D_MODEL),
        out_shape=jax.ShapeDtypeStruct((1, rows, D_MODEL), F32),
        compiler_params=_params(("parallel",)),
        name="merge",
    )(h, ya, yb, yc, of, ob, dproj, g, gout, bd, wgz, wb, wo)
    return out.reshape(bn, L, D_MODEL)


def _ffn_kernel(h_ref, g_ref, wg_ref, wu_ref, wd_ref, o_ref):
    h = h_ref[0]
    u = _rms(h, g_ref[...]).astype(BF16)
    half = D_FF // 2
    out = h
    for c in range(2):
        a = _dot(u, wg_ref[:, c * half:(c + 1) * half])
        t = (jax.nn.silu(a) * _dot(u, wu_ref[:, c * half:(c + 1) * half])).astype(BF16)
        out = out + _dot(t, wd_ref[c * half:(c + 1) * half, :])
    o_ref[0] = out


def _ffn(h, g, wg, wu, wd):
    bn, L, _ = h.shape
    rows = bn * L
    tm = _dense_tile(rows)
    row = pl.BlockSpec((1, tm, D_MODEL), lambda i: (0, i, 0))
    out = pl.pallas_call(
        _ffn_kernel,
        grid=(rows // tm,),
        in_specs=[row, _const_spec((1, D_MODEL)), _const_spec((D_MODEL, D_FF)), _const_spec((D_MODEL, D_FF)),
                  _const_spec((D_FF, D_MODEL))],
        out_specs=row,
        out_shape=jax.ShapeDtypeStruct((1, rows, D_MODEL), F32),
        compiler_params=_params(("parallel",)),
        name="ffn",
    )(h.reshape(1, rows, D_MODEL), g, wg, wu, wd)
    return out.reshape(bn, L, D_MODEL)


def _final_kernel(h_ref, g_ref, o_ref):
    o_ref[0] = _rms(h_ref[0], g_ref[...])


def _final_norm(h, g):
    bn, L, _ = h.shape
    S = L - META_END
    return pl.pallas_call(
        _final_kernel,
        grid=(bn, S // BLOCK),
        in_specs=[pl.BlockSpec((1, BLOCK, D_MODEL), lambda b, i: (b, i + META_END // BLOCK, 0)), _const_spec((1, D_MODEL))],
        out_specs=pl.BlockSpec((1, BLOCK, D_MODEL), lambda b, i: (b, i, 0)),
        out_shape=jax.ShapeDtypeStruct((bn, S, D_MODEL), F32),
        compiler_params=_params(("parallel", "parallel")),
        name="final_norm",
    )(h, g)


def _t5_bucket(rel):
    half = N_BUCKETS // 2
    exact = half // 2
    n = jnp.abs(rel)
    nf = jnp.maximum(n, exact).astype(F32)
    big = exact + (jnp.log(nf / exact) / math.log(MAX_DIST / exact) * (half - exact)).astype(jnp.int32)
    big = jnp.minimum(big, half - 1)
    return jnp.where(rel > 0, half, 0) + jnp.where(n < exact, n, big)


def _lookup(table, bucket):
    hit = bucket[None, ..., None] == jnp.arange(N_BUCKETS)
    return jnp.sum(jnp.where(hit, table.T.reshape((N_HEADS,) + (1,) * bucket.ndim + (N_BUCKETS,)), 0.0), axis=-1)


def _bias_tables(rel_bias):
    bias_a = rel_bias[:, :N_HEADS].astype(F32) * LOG2E
    bias_c = rel_bias[:, N_HEADS:].astype(F32) * LOG2E
    r = jnp.arange(BLOCK)[:, None]
    rel_band = (jnp.arange(3 * BLOCK)[None, :] - BLOCK) - r
    bband = jnp.where(jnp.abs(rel_band)[None] <= WINDOW, _lookup(bias_a, _t5_bucket(rel_band)), NEG)
    lane = jnp.arange(BLOCK)[None, :]
    rel_m = lane[None] - (jnp.arange(3)[:, None, None] * BLOCK + r[None])
    bmeta = jnp.where((lane >= FRONT)[None, None], jnp.moveaxis(_lookup(bias_a, _t5_bucket(rel_m)), 0, 1), NEG)
    rel5t = (jnp.arange(5)[:, None, None] - 2) * BLOCK + r[None] - lane[None]
    bias5t = jnp.moveaxis(_lookup(bias_c, _t5_bucket(rel5t)), 0, 1)
    return bband, bmeta, bias5t


def _rope_layout():
    return np.concatenate([np.zeros(32, int), np.ones(32, int)] * 2)


def _rope_cols(w, heads):
    lead = w.shape[:-1]
    w = w.reshape(lead + (heads // 2, 2, 2, 2, HEAD_DIM // 4))
    if heads == N_HEADS:
        w = jnp.moveaxis(w, (-5, -4, -3, -2), (-3, -5, -2, -4))
    else:
        w = jnp.moveaxis(w, (-4, -3, -2), (-3, -2, -4))
    return w.reshape(lead + (heads * HEAD_DIM,))


def _pair_cols(w):
    lead = w.shape[:-1]
    return jnp.swapaxes(w.reshape(lead + (N_KV, 2, HEAD_DIM)), -3, -2).reshape(lead + (N_HEADS * HEAD_DIM,))


def _rope_tables(L, S):
    rows = S // GRID_W
    row = jnp.concatenate([jnp.zeros((FRONT,), jnp.int32), -jnp.ones((N_META,), jnp.int32),
                           jnp.repeat(jnp.arange(rows, dtype=jnp.int32), GRID_W)]).astype(F32)
    col = jnp.concatenate([jnp.zeros((FRONT,), jnp.int32), jnp.arange(N_META, dtype=jnp.int32),
                           jnp.tile(jnp.arange(GRID_W, dtype=jnp.int32), rows)]).astype(F32)
    half = HEAD_DIM // 2
    inv = ROPE_THETA ** (-jnp.arange(0, half, 2, dtype=F32) / half)
    ang = jnp.concatenate([row[:, None] * inv[None, :], col[:, None] * inv[None, :]], axis=-1)
    cos_t = jnp.tile(jnp.cos(ang), (1, 4))
    sin_t = jnp.concatenate([-jnp.tile(jnp.sin(ang), (1, 2)), jnp.tile(jnp.sin(ang), (1, 2))], axis=-1)
    return cos_t, sin_t


def _rope_gain(g):
    g = jnp.swapaxes(g.astype(F32).reshape(2, 2, HEAD_DIM // 4), 0, 1).reshape(2, 1, HEAD_DIM // 2)
    return jnp.broadcast_to(g, (2, 2, HEAD_DIM // 2)).reshape(LANES)


def _qkv_weight(wl):
    a, b, c = wl[:, 0:512], wl[:, 512:1024], wl[:, 1024:1536]
    return jnp.concatenate([_pair_cols(a[:, :256]), a[:, 256:],
                            _rope_cols(b[:, :256], N_HEADS), _rope_cols(b[:, 256:384], N_KV), b[:, 384:],
                            _pair_cols(c[:, :256]), c[:, 256:]], axis=1)


def kernel(x, meta_tokens, rel_bias, hgrn_lb_logits, ln_mix, w_in, attn_sink, qk_norm_q, qk_norm_k, diff_lambda,
           diff_subnorm, hgrn_out_norm, w_branch, w_out, ln_ffn, w_ffn_gate, w_ffn_up, w_ffn_down, ln_final):
    bn, S, _ = x.shape
    L = META_END + S
    depth = w_in.shape[0]
    assert L % ROW_TILE == 0 and L % Q_TILE == 0 and S % GRID_W == 0
    h = jnp.concatenate([jnp.zeros((bn, FRONT, D_MODEL), x.dtype),
                         jnp.broadcast_to(meta_tokens.astype(x.dtype)[None], (bn, N_META, D_MODEL)), x], axis=1)
    bband, bmeta, bias5t = _bias_tables(rel_bias)
    cos_t, sin_t = _rope_tables(L, S)
    rhead = _rope_layout()
    lane = np.arange(LANES)
    hsel_rope = jnp.asarray(rhead[:, None] == rhead[None, :], BF16)
    hsel_half = jnp.asarray((lane[:, None] // HEAD_DIM) == (lane[None, :] // HEAD_DIM), BF16)
    w256 = np.arange(MIX_W)
    bd = jnp.asarray((w256[:, None] // HEAD_DIM) == (w256[None, :] // HEAD_DIM), BF16)
    tri = jnp.asarray(np.tril(np.ones((BLOCK, BLOCK))), BF16)
    cs = np.ones((1, QKV_W), np.float32)
    cs[0, 0:256] = HEAD_DIM ** -0.5 * LOG2E
    cs[0, 1024:1280] = C_DIM ** -0.5 * LOG2E
    cs = jnp.asarray(cs)
    lb_p = jax.nn.softmax(hgrn_lb_logits.astype(F32), axis=1)
    lb_all = jnp.cumsum(lb_p, axis=1) - lb_p[:, :1]

    for l in range(depth):
        wl = w_in[l]
        wqkv = _qkv_weight(wl).astype(BF16)
        wd = wl[:, QKV_W:QKV_W + D_W].astype(BF16)
        wgz = wl[:, QKV_W + D_W:].astype(BF16)
        gn = jnp.stack([_rope_gain(qk_norm_q[l]) * (HEAD_DIM ** -0.5 * LOG2E)] * 2 + [_rope_gain(qk_norm_k[l])])
        qkv, dproj, vtb, vtc = _in_proj(h, ln_mix[l][None], wqkv, wd, cs, cos_t, sin_t, gn, hsel_rope)
        ya = _attn_a(qkv, attn_sink[l].astype(F32) * LOG2E, bband, bmeta)
        yb = _attn_b(qkv, vtb)
        lam_init = 0.8 - 0.6 * math.exp(-0.3 * l)
        lam_p = diff_lambda[l].astype(F32)
        lam = jnp.exp(jnp.sum(lam_p[0] * lam_p[1])) - jnp.exp(jnp.sum(lam_p[2] * lam_p[3])) + lam_init
        scal = jnp.stack([lam, jnp.asarray(1.0 - lam_init, F32)])
        gsub = jnp.tile(diff_subnorm[l].astype(F32), 2)[None]
        yc = _attn_c(qkv, vtc, scal, bias5t, gsub, hsel_half)
        lb2 = lb_all[:, l][:, None, :]
        of, ob = _hgrn(dproj, lb2, tri, bd)
        wb = w_branch[l].astype(BF16)
        wb_att = jnp.swapaxes(wb[:3].reshape(3, N_KV, 2, HEAD_DIM, D_MODEL), 1, 2).reshape(3, MIX_W, D_MODEL)
        wb = jnp.concatenate([wb_att, wb[3:]], axis=0)
        gout = jnp.tile(hgrn_out_norm[l].astype(F32), N_HEADS)[None]
        h = _merge(h, ya, yb, yc, of, ob, dproj, ln_mix[l][None], gout, bd, wgz, wb, w_out[l].astype(BF16))
        h = _ffn(h, ln_ffn[l][None], w_ffn_gate[l].astype(BF16), w_ffn_up[l].astype(BF16), w_ffn_down[l].astype(BF16))
    return _final_norm(h, ln_final[None])
```

```python
import functools
import math

import jax
import jax.numpy as jnp
import numpy as np
from jax import lax
from jax.experimental import pallas as pl
from jax.experimental.pallas import tpu as pltpu

D_MODEL = 1024
N_BRANCH = 4
MIX_W = D_MODEL // N_BRANCH
HEAD_DIM = 64
BLOCK = 128
WINDOW = 128
ROPE_THETA = 10000.0
GRID_W = 64
N_HEADS = 4
N_KV = 2
C_DIM = 32
N_META = 16
FRONT = (-N_META) % BLOCK
META_END = FRONT + N_META
N_BUCKETS = 32
MAX_DIST = 128
D_FF = -(-(8 * D_MODEL) // (3 * 256)) * 256
RMS_EPS = 1e-6
NEG = -1e30
F_FLOOR = 1e-30
LOG2E = 1.4426950408889634
LANES = 128
ROW_TILE = 384
DENSE_TILES = (768, 384, 128)
C_UNROLL = 4
Q_TILE = 384
QKV_W = 3 * 512
D_W = 5 * MIX_W
VMEM_LIMIT = 56 * 1024 * 1024

F32 = jnp.float32
BF16 = jnp.bfloat16


def _dot(a, b):
    return jnp.dot(a, b, preferred_element_type=F32)


def _dot_nt(a, b):
    return lax.dot_general(a, b, (((1,), (1,)), ((), ())), preferred_element_type=F32)


def _split_dot(x, sel):
    hi = x.astype(BF16)
    lo = (x - hi.astype(F32)).astype(BF16)
    return _dot(hi, sel) + _dot(lo, sel)


def _rms(x, g):
    return x * lax.rsqrt(jnp.mean(x * x, axis=-1, keepdims=True) + RMS_EPS) * g


def _params(sem):
    return pltpu.CompilerParams(dimension_semantics=sem, vmem_limit_bytes=VMEM_LIMIT)


def _const_spec(shape):
    nd = len(shape)
    return pl.BlockSpec(shape, lambda *_: (0,) * nd, pipeline_mode=pl.Buffered(1))


def _dense_tile(rows):
    return next(t for t in DENSE_TILES if rows % t == 0)


def _in_proj_kernel(h_ref, g_ref, wqkv_ref, wd_ref, cs_ref, cos_ref, sin_ref, gn_ref, hsel_ref,
                    qkv_ref, d_ref, vtb_ref, vtc_ref):
    u = _rms(h_ref[0], g_ref[...]).astype(BF16)
    acc = _dot(u, wqkv_ref[...]) * cs_ref[...]
    qkv_ref[0, :, 0:512] = acc[:, 0:512].astype(BF16)
    for t in range(3):
        c0 = 512 + LANES * t
        xt = acc[:, c0:c0 + LANES]
        ms = _split_dot(xt * xt, hsel_ref[...]) * (1.0 / HEAD_DIM)
        xt = xt * lax.rsqrt(ms + RMS_EPS) * gn_ref[t:t + 1, :]
        xt = xt * cos_ref[...] + pltpu.roll(xt, LANES // 2, 1) * sin_ref[...]
        qkv_ref[0, :, c0:c0 + LANES] = xt.astype(BF16)
    qkv_ref[0, :, 896:QKV_W] = acc[:, 896:QKV_W].astype(BF16)
    vtb_ref[0] = acc[:, 896:1024].T.astype(BF16)
    vtc_ref[0] = acc[:, 1408:QKV_W].T.astype(BF16)
    d_ref[0] = _dot(u, wd_ref[...])


def _in_proj(h, g, wqkv, wd, cs, cos_t, sin_t, gn, hsel):
    bn, L, _ = h.shape
    nt = L // ROW_TILE
    return pl.pallas_call(
        _in_proj_kernel,
        grid=(bn, nt),
        in_specs=[
            pl.BlockSpec((1, ROW_TILE, D_MODEL), lambda b, i: (b, i, 0)),
            _const_spec((1, D_MODEL)),
            _const_spec((D_MODEL, QKV_W)),
            _const_spec((D_MODEL, D_W)),
            _const_spec((1, QKV_W)),
            pl.BlockSpec((ROW_TILE, LANES), lambda b, i: (i, 0)),
            pl.BlockSpec((ROW_TILE, LANES), lambda b, i: (i, 0)),
            _const_spec((3, LANES)),
            _const_spec((LANES, LANES)),
        ],
        out_specs=[
            pl.BlockSpec((1, ROW_TILE, QKV_W), lambda b, i: (b, i, 0)),
            pl.BlockSpec((1, ROW_TILE, D_W), lambda b, i: (b, i, 0)),
            pl.BlockSpec((1, LANES, ROW_TILE), lambda b, i: (b, 0, i)),
            pl.BlockSpec((1, LANES, ROW_TILE), lambda b, i: (b, 0, i)),
        ],
        out_shape=[jax.ShapeDtypeStruct((bn, L, QKV_W), BF16), jax.ShapeDtypeStruct((bn, L, D_W), F32),
                   jax.ShapeDtypeStruct((bn, LANES, L), BF16), jax.ShapeDtypeStruct((bn, LANES, L), BF16)],
        compiler_params=_params(("parallel", "parallel")),
        name="in_proj",
    )(h, g, wqkv, wd, cs, cos_t, sin_t, gn, hsel)


def _stack_q(q_ref, rows, masks):
    parts = []
    for t in range(2):
        qt = q_ref[0, rows, t * LANES:(t + 1) * LANES]
        for m in masks:
            parts.append(jnp.where(m, qt, jnp.zeros_like(qt)))
    return jnp.concatenate(parts, axis=0)


def _blk(ref, j):
    if isinstance(j, int):
        return ref[0, j * BLOCK:(j + 1) * BLOCK, :]
    return ref[0, pl.ds(pl.multiple_of(j * BLOCK, BLOCK), BLOCK), :]


def _attn_a_block(i, nb, rows, sink_ref, q_ref, k_ref, v_ref, bband_ref, bmeta_ref, o_ref):
    prev = jnp.maximum(i - 1, 0)
    nxt = jnp.minimum(i + 1, nb - 1)
    kband = jnp.concatenate([_blk(k_ref, prev), _blk(k_ref, i), _blk(k_ref, nxt)], axis=0)
    vband = jnp.concatenate([_blk(v_ref, prev), _blk(v_ref, i), _blk(v_ref, nxt)], axis=0)
    kmeta = k_ref[0, 0:BLOCK, :]
    vmeta = v_ref[0, 0:BLOCK, :]
    col = lax.broadcasted_iota(jnp.int32, (1, 3 * BLOCK), 1)
    lo = jnp.where(i == 0, 2 * BLOCK, jnp.where(i == 1, BLOCK, 0))
    hi = jnp.where(i == nb - 1, 2 * BLOCK, 3 * BLOCK)
    dead = (col < lo) | (col >= hi)
    lane = lax.broadcasted_iota(jnp.int32, (1, LANES), 1)
    mi = jnp.minimum(i, 2)
    outs = []
    for kv in range(N_KV):
        qs = _stack_q(q_ref, rows, [(lane >= HEAD_DIM) == bool(kv)])
        bb = jnp.concatenate([bband_ref[2 * kv], bband_ref[2 * kv + 1]], axis=0)
        bm = jnp.concatenate([bmeta_ref[mi, 2 * kv], bmeta_ref[mi, 2 * kv + 1]], axis=0)
        sb = jnp.where(dead, NEG, _dot_nt(qs, kband) + bb)
        sm = _dot_nt(qs, kmeta) + bm
        sk = jnp.concatenate([jnp.full((BLOCK, 1), sink_ref[2 * kv], F32),
                              jnp.full((BLOCK, 1), sink_ref[2 * kv + 1], F32)], axis=0)
        m = jnp.maximum(jnp.maximum(jnp.max(sb, axis=-1, keepdims=True), jnp.max(sm, axis=-1, keepdims=True)), sk)
        pb = jnp.exp2(sb - m)
        pm = jnp.exp2(sm - m)
        den = jnp.sum(pb, axis=-1, keepdims=True) + jnp.sum(pm, axis=-1, keepdims=True) + jnp.exp2(sk - m)
        outs.append((_dot(pb.astype(BF16), vband) + _dot(pm.astype(BF16), vmeta)) / den)
    for t in range(2):
        y = jnp.where(lane < HEAD_DIM, outs[0][t * BLOCK:(t + 1) * BLOCK], outs[1][t * BLOCK:(t + 1) * BLOCK])
        o_ref[0, rows, t * LANES:(t + 1) * LANES] = y.astype(BF16)


def _attn_a_kernel(sink_ref, q_ref, k_ref, v_ref, bband_ref, bmeta_ref, o_ref):
    nb = k_ref.shape[1] // BLOCK
    nsub = Q_TILE // BLOCK
    for sb in range(nsub):
        _attn_a_block(pl.program_id(1) * nsub + sb, nb, slice(sb * BLOCK, (sb + 1) * BLOCK),
                      sink_ref, q_ref, k_ref, v_ref, bband_ref, bmeta_ref, o_ref)


def _attn_a(qkv, sink, bband, bmeta):
    bn, L, _ = qkv.shape
    return pl.pallas_call(
        _attn_a_kernel,
        grid=(bn, L // Q_TILE),
        in_specs=[
            pl.BlockSpec(memory_space=pltpu.SMEM),
            pl.BlockSpec((1, Q_TILE, 2 * LANES), lambda b, i: (b, i, 0)),
            pl.BlockSpec((1, L, LANES), lambda b, i: (b, 0, 2)),
            pl.BlockSpec((1, L, LANES), lambda b, i: (b, 0, 3)),
            _const_spec((N_HEADS, BLOCK, 3 * BLOCK)),
            _const_spec((3, N_HEADS, BLOCK, BLOCK)),
        ],
        out_specs=pl.BlockSpec((1, Q_TILE, MIX_W), lambda b, i: (b, i, 0)),
        out_shape=jax.ShapeDtypeStruct((bn, L, MIX_W), BF16),
        compiler_params=_params(("parallel", "parallel")),
        name="attn_window",
    )(sink, qkv, qkv, qkv, bband, bmeta)


def _stack_maps(q_ref, masks_per_kv):
    parts = []
    for masks in masks_per_kv:
        for t in range(2):
            qt = q_ref[0, :, t * LANES:(t + 1) * LANES].astype(F32)
            for m in masks:
                parts.append(jnp.where(m, qt, 0.0).T.astype(BF16))
    return jnp.concatenate(parts, axis=1)


def _attn_sweep(qs_ref, k_ref, vt_ref, bias_fn, sa_ref, sb_ref, m_ref, acc_ref):
    nb = k_ref.shape[1] // BLOCK
    W = qs_ref.shape[1]
    half = W // 2
    ngrp = (nb - 1) // C_UNROLL
    keyrow = lax.broadcasted_iota(jnp.int32, (BLOCK, 1), 0)

    def rows(ref, j0, width):
        if isinstance(j0, int):
            return ref[0, j0 * BLOCK:(j0 + width) * BLOCK, :]
        return ref[0, pl.ds(pl.multiple_of(j0 * BLOCK, BLOCK), width * BLOCK), :]

    def cols(ref, j0, width):
        if isinstance(j0, int):
            return ref[0, :, j0 * BLOCK:(j0 + width) * BLOCK]
        return ref[0, :, pl.ds(pl.multiple_of(j0 * BLOCK, BLOCK), width * BLOCK)]

    def scores(j0, width):
        s_all = _dot(rows(k_ref, j0, width), qs_ref[...])
        parts = []
        for u in range(width):
            s = s_all[u * BLOCK:(u + 1) * BLOCK]
            b = bias_fn(j0 + u)
            if b is not None:
                s = s + b
            if isinstance(j0, int) and j0 + u == 0:
                s = jnp.where(keyrow >= FRONT, s, NEG)
            parts.append(s)
        return parts[0] if width == 1 else jnp.concatenate(parts, axis=0)

    def absorb(s, j0, width, first=False):
        mg = jnp.max(jnp.max(s.reshape(width * BLOCK // 8, 8, W), axis=0), axis=0, keepdims=True)
        m_new = mg if first else jnp.maximum(m_ref[...], mg)
        p = jnp.exp2(s - m_new).astype(BF16)
        vt = cols(vt_ref, j0, width)
        ones = jnp.ones((16, width * BLOCK), BF16)
        if not first:
            alpha = jnp.exp2(m_ref[...] - m_new)
        for kv in range(N_KV):
            w = _dot(jnp.concatenate([vt[kv * HEAD_DIM:(kv + 1) * HEAD_DIM], ones], axis=0),
                     p[:, kv * half:(kv + 1) * half])
            acc_ref[kv] = w if first else acc_ref[kv] * alpha[:, kv * half:(kv + 1) * half] + w
        m_ref[...] = m_new

    def group(n):
        return 1 + n * C_UNROLL

    absorb(scores(0, 1), 0, 1, first=True)
    if ngrp > 0:
        sa_ref[...] = scores(group(0), C_UNROLL)
        npair = (ngrp - 1) // 2

        def body(n, carry):
            g = 2 * n
            sb_ref[...] = scores(group(g + 1), C_UNROLL)
            absorb(sa_ref[...], group(g), C_UNROLL)
            sa_ref[...] = scores(group(g + 2), C_UNROLL)
            absorb(sb_ref[...], group(g + 1), C_UNROLL)
            return carry

        lax.fori_loop(0, npair, body, 0)
        g = 2 * npair
        if ngrp - g == 2:
            sb_ref[...] = scores(group(g + 1), C_UNROLL)
            absorb(sa_ref[...], group(g), C_UNROLL)
            absorb(sb_ref[...], group(g + 1), C_UNROLL)
        else:
            absorb(sa_ref[...], group(g), C_UNROLL)
    for j in range(1 + ngrp * C_UNROLL, nb):
        absorb(scores(j, 1), j, 1)


def _sweep_scratch(nmaps):
    W = nmaps * Q_TILE
    return [pltpu.VMEM((LANES, W), BF16), pltpu.VMEM((C_UNROLL * BLOCK, W), F32), pltpu.VMEM((C_UNROLL * BLOCK, W), F32),
            pltpu.VMEM((1, W), F32), pltpu.VMEM((N_KV, HEAD_DIM + 16, W // 2), F32)]


def _attn_b_kernel(q_ref, k_ref, vt_ref, o_ref, qs_ref, sa_ref, sb_ref, m_ref, acc_ref):
    lane = lax.broadcasted_iota(jnp.int32, (1, LANES), 1)
    head_b = (lane // (HEAD_DIM // 2)) % 2
    qs_ref[...] = _stack_maps(q_ref, [[head_b == kv] for kv in range(N_KV)])
    _attn_sweep(qs_ref, k_ref, vt_ref, lambda j: None, sa_ref, sb_ref, m_ref, acc_ref)
    ot = [acc_ref[kv][0:HEAD_DIM] / acc_ref[kv][HEAD_DIM:HEAD_DIM + 1] for kv in range(N_KV)]
    for g in range(2):
        yt = jnp.concatenate([ot[kv][:, g * Q_TILE:(g + 1) * Q_TILE] for kv in range(N_KV)], axis=0)
        o_ref[0, :, g * LANES:(g + 1) * LANES] = yt.T.astype(BF16)


def _attn_b(qkv, vt):
    bn, L, _ = qkv.shape
    return pl.pallas_call(
        _attn_b_kernel,
        grid=(bn, L // Q_TILE),
        in_specs=[
            pl.BlockSpec((1, Q_TILE, 2 * LANES), lambda b, i: (b, i, 2)),
            pl.BlockSpec((1, L, LANES), lambda b, i: (b, 0, 6)),
            pl.BlockSpec((1, LANES, L), lambda b, i: (b, 0, 0)),
        ],
        out_specs=pl.BlockSpec((1, Q_TILE, MIX_W), lambda b, i: (b, i, 0)),
        out_shape=jax.ShapeDtypeStruct((bn, L, MIX_W), BF16),
        scratch_shapes=_sweep_scratch(4),
        compiler_params=_params(("parallel", "parallel")),
        name="attn_rope",
    )(qkv, qkv, vt)


def _attn_c_kernel(sc_ref, q_ref, k_ref, vt_ref, bias_ref, gsub_ref, hsel_ref, o_ref,
                   qs_ref, sa_ref, sb_ref, m_ref, acc_ref):
    i = pl.program_id(1)
    lam = sc_ref[0]
    lane = lax.broadcasted_iota(jnp.int32, (1, LANES), 1)
    grp = lane // C_DIM
    qs_ref[...] = _stack_maps(q_ref, [[grp == 2 * kv, grp == 2 * kv + 1] for kv in range(N_KV)])

    def bias(j):
        nsub = Q_TILE // BLOCK
        ts = [jnp.clip(j - (i * nsub + sb) + 2, 0, 4) for sb in range(nsub)]
        per_head = [jnp.concatenate([bias_ref[t, h] for t in ts], axis=1) for h in range(N_HEADS)]
        return jnp.concatenate([per_head[h] for h in range(N_HEADS) for _ in range(2)], axis=1)

    _attn_sweep(qs_ref, k_ref, vt_ref, bias, sa_ref, sb_ref, m_ref, acc_ref)
    ot = [acc_ref[kv][0:HEAD_DIM] / acc_ref[kv][HEAD_DIM:HEAD_DIM + 1] for kv in range(N_KV)]
    for g in range(2):
        dt = [ot[kv][:, (2 * g) * Q_TILE:(2 * g + 1) * Q_TILE] - lam * ot[kv][:, (2 * g + 1) * Q_TILE:(2 * g + 2) * Q_TILE]
              for kv in range(N_KV)]
        y = jnp.concatenate(dt, axis=0).T
        ms = _split_dot(y * y, hsel_ref[...]) * (1.0 / HEAD_DIM)
        y = y * lax.rsqrt(ms + RMS_EPS) * gsub_ref[...] * sc_ref[1]
        o_ref[0, :, g * LANES:(g + 1) * LANES] = y.astype(BF16)


def _attn_c(qkv, vt, scal, bias5t, gsub, hsel):
    bn, L, _ = qkv.shape
    return pl.pallas_call(
        _attn_c_kernel,
        grid=(bn, L // Q_TILE),
        in_specs=[
            pl.BlockSpec(memory_space=pltpu.SMEM),
            pl.BlockSpec((1, Q_TILE, 2 * LANES), lambda b, i: (b, i, 4)),
            pl.BlockSpec((1, L, LANES), lambda b, i: (b, 0, 10)),
            pl.BlockSpec((1, LANES, L), lambda b, i: (b, 0, 0)),
            _const_spec((5, N_HEADS, BLOCK, BLOCK)),
            _const_spec((1, LANES)),
            _const_spec((LANES, LANES)),
        ],
        out_specs=pl.BlockSpec((1, Q_TILE, MIX_W), lambda b, i: (b, i, 0)),
        out_shape=jax.ShapeDtypeStruct((bn, L, MIX_W), BF16),
        scratch_shapes=_sweep_scratch(8),
        compiler_params=_params(("parallel", "parallel")),
        name="attn_diff",
    )(scal, qkv, qkv, vt, bias5t, gsub, hsel)


def _split3_dot(sel, x):
    x1 = x.astype(BF16)
    r1 = x - x1.astype(F32)
    x2 = r1.astype(BF16)
    x3 = (r1 - x2.astype(F32)).astype(BF16)
    return _dot(sel, x1) + _dot(sel, x2) + _dot(sel, x3)


def _hgrn_chunk(forward, q, kk, g, v, tri_ref, bd_ref, st_ref):
    W = MIX_W
    b = _split3_dot(tri_ref[...], g)
    tot = b[BLOCK - 1:BLOCK, :]
    if forward:
        x, e = b, tot - b
    else:
        x, e = tot - b + g, b - g
    sub = lax.broadcasted_iota(jnp.int32, (1, 8, 1), 1)
    x3, q3, k3, v3 = (a.reshape(BLOCK // 8, 8, W) for a in (x, q, kk, v))
    o = _dot((q * kk).astype(BF16), bd_ref[...]) * v
    for d in range(1, 8):
        sh = d if forward else 8 - d
        ok = (sub >= d) if forward else (sub + d < 8)
        e_d = jnp.exp2(jnp.where(ok, x3 - pltpu.roll(x3, sh, 1), NEG)) * q3 * pltpu.roll(k3, sh, 1)
        o = o + _dot(e_d.reshape(BLOCK, W).astype(BF16), bd_ref[...]) * pltpu.roll(v3, sh, 1).reshape(BLOCK, W)
    row = lax.broadcasted_iota(jnp.int32, (BLOCK, 1), 0)
    qcol = lax.broadcasted_iota(jnp.int32, (1, N_HEADS * BLOCK), 1) % BLOCK
    lane_head = lax.broadcasted_iota(jnp.int32, (1, W), 1) // HEAD_DIM
    at = jnp.zeros((BLOCK, N_HEADS * BLOCK), F32)
    for m in (8, 16, 32, 64):
        mid = m - 1 if forward else m
        xm = jnp.concatenate([jnp.broadcast_to(x[i + mid:i + mid + 1, :], (2 * m, W))
                              for i in range(0, BLOCK, 2 * m)], axis=0)
        second = (row // m) % 2 == 1
        qside = second if forward else jnp.logical_not(second)
        qt = (q * jnp.exp2(jnp.where(qside, x - xm, NEG))).astype(BF16)
        kt = (kk * jnp.exp2(jnp.where(qside, NEG, xm - x))).astype(BF16)
        qstack = jnp.concatenate([jnp.where(lane_head == h, qt, jnp.zeros_like(qt)) for h in range(N_HEADS)], axis=0)
        a = _dot_nt(kt, qstack)
        at = at + jnp.where(row // (2 * m) == qcol // (2 * m), a, 0.0)
    vt = v.T.astype(BF16)
    ot = _dot(vt, at.astype(BF16))
    o = o + jnp.concatenate([ot[h * HEAD_DIM:(h + 1) * HEAD_DIM, h * BLOCK:(h + 1) * BLOCK]
                             for h in range(N_HEADS)], axis=0).T
    qd = (q * jnp.exp2(x)).astype(BF16)
    kd = (kk * jnp.exp2(e)).astype(BF16)
    gam = jnp.exp2(tot)
    hmask = (lax.broadcasted_iota(jnp.int32, (LANES, LANES), 0) // HEAD_DIM
             == lax.broadcasted_iota(jnp.int32, (LANES, LANES), 1) // HEAD_DIM)
    inter = []
    for hf in range(2):
        lanes = slice(hf * LANES, (hf + 1) * LANES)
        st = st_ref[hf]
        inter.append(_dot_nt(qd[:, lanes], st.astype(BF16)))
        st_ref[hf] = st * gam[:, lanes] + jnp.where(hmask, _dot(vt[lanes, :], kd[:, lanes]), 0.0)
    return o + jnp.concatenate(inter, axis=1)


def _hgrn_direction(forward, first_tile, q_ref, z_ref, v_ref, lb_ref, tri_ref, bd_ref, o_ref, st_ref):
    R = q_ref.shape[1]
    z = z_ref[0]
    q = q_ref[0] * (HEAD_DIM ** -0.5)
    v = v_ref[0]
    lb = lb_ref[...]
    row = lax.broadcasted_iota(jnp.int32, (R, 1), 0)
    f = lb + (1.0 - lb) * jax.nn.sigmoid(z)
    g = jnp.log(jnp.maximum(f, F_FLOOR)) * LOG2E
    kk = (1.0 - lb) * jax.nn.sigmoid(-z)
    kk = jnp.where(row < jnp.where(first_tile, FRONT, 0), 0.0, kk)
    nblk = R // BLOCK
    for n in range(nblk):
        r0 = (n if forward else nblk - 1 - n) * BLOCK
        rows = slice(r0, r0 + BLOCK)
        o_ref[0, rows, :] = _hgrn_chunk(forward, q[rows], kk[rows], g[rows], v[rows], tri_ref, bd_ref, st_ref)


def _hgrn_kernel(qf_ref, zf_ref, vf_ref, qb_ref, zb_ref, vb_ref, lb_ref, tri_ref, bd_ref, of_ref, ob_ref, st_ref):
    j = pl.program_id(1)
    nt = pl.num_programs(1)

    @pl.when(j == 0)
    def _():
        st_ref[...] = jnp.zeros_like(st_ref)

    _hgrn_direction(True, j == 0, qf_ref, zf_ref, vf_ref, lb_ref.at[0], tri_ref, bd_ref, of_ref, st_ref.at[0])
    _hgrn_direction(False, j == nt - 1, qb_ref, zb_ref, vb_ref, lb_ref.at[1], tri_ref, bd_ref, ob_ref, st_ref.at[1])


def _hgrn(dproj, lb2, tri, bd):
    bn, L, _ = dproj.shape
    nt = L // ROW_TILE
    fw = lambda c: pl.BlockSpec((1, ROW_TILE, MIX_W), lambda b, j: (b, j, c))
    bw = lambda c: pl.BlockSpec((1, ROW_TILE, MIX_W), lambda b, j: (b, nt - 1 - j, c))
    return pl.pallas_call(
        _hgrn_kernel,
        grid=(bn, nt),
        in_specs=[fw(0), fw(1), fw(3), bw(0), bw(2), bw(3), _const_spec((2, 1, MIX_W)), _const_spec((BLOCK, BLOCK)),
                  _const_spec((MIX_W, MIX_W))],
        out_specs=[pl.BlockSpec((1, ROW_TILE, MIX_W), lambda b, j: (b, j, 0)),
                   pl.BlockSpec((1, ROW_TILE, MIX_W), lambda b, j: (b, nt - 1 - j, 0))],
        out_shape=[jax.ShapeDtypeStruct((bn, L, MIX_W), F32)] * 2,
        scratch_shapes=[pltpu.VMEM((2, 2, LANES, LANES), F32)],
        compiler_params=_params(("parallel", "arbitrary")),
        name="hgrn",
    )(dproj, dproj, dproj, dproj, dproj, dproj, lb2, tri, bd)


def _merge_kernel(h_ref, ya_ref, yb_ref, yc_ref, of_ref, ob_ref, dg_ref, g_ref, gout_ref, bd_ref,
                  wgz_ref, wb_ref, wo_ref, o_ref):
    h = h_ref[0]
    u = _rms(h, g_ref[...]).astype(BF16)
    od = of_ref[0] + ob_ref[0]
    ms = _split_dot(od * od, bd_ref[...]) * (1.0 / HEAD_DIM)
    yd = (od * lax.rsqrt(ms + RMS_EPS) * gout_ref[...] * jax.nn.silu(dg_ref[0])).astype(BF16)
    ys = (ya_ref[0], yb_ref[0], yc_ref[0], yd)
    merged = None
    for n in range(N_BRANCH):
        gate = jax.nn.sigmoid(_dot(u, wgz_ref[:, n * D_MODEL:(n + 1) * D_MODEL]))
        term = gate * _dot(ys[n], wb_ref[n])
        merged = term if merged is None else merged + term
    o_ref[0] = h + _dot(merged.astype(BF16), wo_ref[...])


def _merge(h, ya, yb, yc, of, ob, dproj, g, gout, bd, wgz, wb, wo):
    bn, L, _ = h.shape
    rows = bn * L
    tm = _dense_tile(rows)
    flat = lambda a: a.reshape(1, rows, a.shape[-1])
    h, ya, yb, yc, of, ob, dproj = (flat(a) for a in (h, ya, yb, yc, of, ob, dproj))
    row = lambda w, c=0: pl.BlockSpec((1, tm, w), lambda i: (0, i, c))
    out = pl.pallas_call(
        _merge_kernel,
        grid=(rows // tm,),
        in_specs=[row(D_MODEL), row(MIX_W), row(MIX_W), row(MIX_W), row(MIX_W), row(MIX_W), row(MIX_W, 4),
                  _const_spec((1, D_MODEL)), _const_spec((1, MIX_W)), _const_spec((MIX_W, MIX_W)),
                  _const_spec((D_MODEL, N_BRANCH * D_MODEL)), _const_spec((N_BRANCH, MIX_W, D_MODEL)),
                  _const_spec((D_MODEL, D_MODEL))],
        out_specs=row(D_MODEL),
        out_shape=jax.ShapeDtypeStruct((1, rows, D_MODEL), F32),
        compiler_params=_params(("parallel",)),
        name="merge",
    )(h, ya, yb, yc, of, ob, dproj, g, gout, bd, wgz, wb, wo)
    return out.reshape(bn, L, D_MODEL)


def _ffn_kernel(h_ref, g_ref, wg_ref, wu_ref, wd_ref, gfin_ref, o_ref, *, final):
    h = h_ref[0]
    u = _rms(h, g_ref[...]).astype(BF16)
    half = D_FF // 2
    out = h
    for c in range(2):
        a = _dot(u, wg_ref[:, c * half:(c + 1) * half])
        t = (jax.nn.silu(a) * _dot(u, wu_ref[:, c * half:(c + 1) * half])).astype(BF16)
        out = out + _dot(t, wd_ref[c * half:(c + 1) * half, :])
    o_ref[0] = _rms(out, gfin_ref[...]) if final else out


def _ffn(h, g, wg, wu, wd, gfin, final):
    bn, L, _ = h.shape
    rows = bn * L
    tm = _dense_tile(rows)
    row = pl.BlockSpec((1, tm, D_MODEL), lambda i: (0, i, 0))
    out = pl.pallas_call(
        functools.partial(_ffn_kernel, final=final),
        grid=(rows // tm,),
        in_specs=[row, _const_spec((1, D_MODEL)), _const_spec((D_MODEL, D_FF)), _const_spec((D_MODEL, D_FF)),
                  _const_spec((D_FF, D_MODEL)), _const_spec((1, D_MODEL))],
        out_specs=row,
        out_shape=jax.ShapeDtypeStruct((1, rows, D_MODEL), F32),
        compiler_params=_params(("parallel",)),
        name="ffn",
    )(h.reshape(1, rows, D_MODEL), g, wg, wu, wd, gfin)
    return out.reshape(bn, L, D_MODEL)


def _t5_bucket(rel):
    half = N_BUCKETS // 2
    exact = half // 2
    n = jnp.abs(rel)
    nf = jnp.maximum(n, exact).astype(F32)
    big = exact + (jnp.log(nf / exact) / math.log(MAX_DIST / exact) * (half - exact)).astype(jnp.int32)
    big = jnp.minimum(big, half - 1)
    return jnp.where(rel > 0, half, 0) + jnp.where(n < exact, n, big)


def _lookup(table, bucket):
    hit = bucket[None, ..., None] == jnp.arange(N_BUCKETS)
    return jnp.sum(jnp.where(hit, table.T.reshape((N_HEADS,) + (1,) * bucket.ndim + (N_BUCKETS,)), 0.0), axis=-1)


def _bias_tables(rel_bias):
    bias_a = rel_bias[:, :N_HEADS].astype(F32) * LOG2E
    bias_c = rel_bias[:, N_HEADS:].astype(F32) * LOG2E
    r = jnp.arange(BLOCK)[:, None]
    rel_band = (jnp.arange(3 * BLOCK)[None, :] - BLOCK) - r
    bband = jnp.where(jnp.abs(rel_band)[None] <= WINDOW, _lookup(bias_a, _t5_bucket(rel_band)), NEG)
    lane = jnp.arange(BLOCK)[None, :]
    rel_m = lane[None] - (jnp.arange(3)[:, None, None] * BLOCK + r[None])
    bmeta = jnp.where((lane >= FRONT)[None, None], jnp.moveaxis(_lookup(bias_a, _t5_bucket(rel_m)), 0, 1), NEG)
    rel5t = (jnp.arange(5)[:, None, None] - 2) * BLOCK + r[None] - lane[None]
    bias5t = jnp.moveaxis(_lookup(bias_c, _t5_bucket(rel5t)), 0, 1)
    return bband, bmeta, bias5t


def _rope_layout():
    return np.concatenate([np.zeros(32, int), np.ones(32, int)] * 2)


def _rope_cols(w, heads):
    lead = w.shape[:-1]
    w = w.reshape(lead + (heads // 2, 2, 2, 2, HEAD_DIM // 4))
    if heads == N_HEADS:
        w = jnp.moveaxis(w, (-5, -4, -3, -2), (-3, -5, -2, -4))
    else:
        w = jnp.moveaxis(w, (-4, -3, -2), (-3, -2, -4))
    return w.reshape(lead + (heads * HEAD_DIM,))


def _pair_cols(w):
    lead = w.shape[:-1]
    return jnp.swapaxes(w.reshape(lead + (N_KV, 2, HEAD_DIM)), -3, -2).reshape(lead + (N_HEADS * HEAD_DIM,))


def _rope_tables(L, S):
    rows = S // GRID_W
    row = jnp.concatenate([jnp.zeros((FRONT,), jnp.int32), -jnp.ones((N_META,), jnp.int32),
                           jnp.repeat(jnp.arange(rows, dtype=jnp.int32), GRID_W)]).astype(F32)
    col = jnp.concatenate([jnp.zeros((FRONT,), jnp.int32), jnp.arange(N_META, dtype=jnp.int32),
                           jnp.tile(jnp.arange(GRID_W, dtype=jnp.int32), rows)]).astype(F32)
    half = HEAD_DIM // 2
    inv = ROPE_THETA ** (-jnp.arange(0, half, 2, dtype=F32) / half)
    ang = jnp.concatenate([row[:, None] * inv[None, :], col[:, None] * inv[None, :]], axis=-1)
    cos_t = jnp.tile(jnp.cos(ang), (1, 4))
    sin_t = jnp.concatenate([-jnp.tile(jnp.sin(ang), (1, 2)), jnp.tile(jnp.sin(ang), (1, 2))], axis=-1)
    return cos_t, sin_t


def _rope_gain(g):
    g = jnp.swapaxes(g.astype(F32).reshape(2, 2, HEAD_DIM // 4), 0, 1).reshape(2, 1, HEAD_DIM // 2)
    return jnp.broadcast_to(g, (2, 2, HEAD_DIM // 2)).reshape(LANES)


def _qkv_weight(wl):
    a, b, c = wl[:, 0:512], wl[:, 512:1024], wl[:, 1024:1536]
    return jnp.concatenate([_pair_cols(a[:, :256]), a[:, 256:],
                            _rope_cols(b[:, :256], N_HEADS), _rope_cols(b[:, 256:384], N_KV), b[:, 384:],
                            _pair_cols(c[:, :256]), c[:, 256:]], axis=1)


def kernel(x, meta_tokens, rel_bias, hgrn_lb_logits, ln_mix, w_in, attn_sink, qk_norm_q, qk_norm_k, diff_lambda,
           diff_subnorm, hgrn_out_norm, w_branch, w_out, ln_ffn, w_ffn_gate, w_ffn_up, w_ffn_down, ln_final):
    bn, S, _ = x.shape
    L = META_END + S
    depth = w_in.shape[0]
    assert L % ROW_TILE == 0 and L % Q_TILE == 0 and S % GRID_W == 0
    h = jnp.concatenate([jnp.zeros((bn, FRONT, D_MODEL), x.dtype),
                         jnp.broadcast_to(meta_tokens.astype(x.dtype)[None], (bn, N_META, D_MODEL)), x], axis=1)
    bband, bmeta, bias5t = _bias_tables(rel_bias)
    cos_t, sin_t = _rope_tables(L, S)
    rhead = _rope_layout()
    lane = np.arange(LANES)
    hsel_rope = jnp.asarray(rhead[:, None] == rhead[None, :], BF16)
    hsel_half = jnp.asarray((lane[:, None] // HEAD_DIM) == (lane[None, :] // HEAD_DIM), BF16)
    w256 = np.arange(MIX_W)
    bd = jnp.asarray((w256[:, None] // HEAD_DIM) == (w256[None, :] // HEAD_DIM), BF16)
    tri = jnp.asarray(np.tril(np.ones((BLOCK, BLOCK))), BF16)
    cs = np.ones((1, QKV_W), np.float32)
    cs[0, 0:256] = HEAD_DIM ** -0.5 * LOG2E
    cs[0, 1024:1280] = C_DIM ** -0.5 * LOG2E
    cs = jnp.asarray(cs)
    lb_p = jax.nn.softmax(hgrn_lb_logits.astype(F32), axis=1)
    lb_all = jnp.cumsum(lb_p, axis=1) - lb_p[:, :1]

    for l in range(depth):
        wl = w_in[l]
        wqkv = _qkv_weight(wl).astype(BF16)
        wd = wl[:, QKV_W:QKV_W + D_W].astype(BF16)
        wgz = wl[:, QKV_W + D_W:].astype(BF16)
        gn = jnp.stack([_rope_gain(qk_norm_q[l]) * (HEAD_DIM ** -0.5 * LOG2E)] * 2 + [_rope_gain(qk_norm_k[l])])
        qkv, dproj, vtb, vtc = _in_proj(h, ln_mix[l][None], wqkv, wd, cs, cos_t, sin_t, gn, hsel_rope)
        ya = _attn_a(qkv, attn_sink[l].astype(F32) * LOG2E, bband, bmeta)
        yb = _attn_b(qkv, vtb)
        lam_init = 0.8 - 0.6 * math.exp(-0.3 * l)
        lam_p = diff_lambda[l].astype(F32)
        lam = jnp.exp(jnp.sum(lam_p[0] * lam_p[1])) - jnp.exp(jnp.sum(lam_p[2] * lam_p[3])) + lam_init
        scal = jnp.stack([lam, jnp.asarray(1.0 - lam_init, F32)])
        gsub = jnp.tile(diff_subnorm[l].astype(F32), 2)[None]
        yc = _attn_c(qkv, vtc, scal, bias5t, gsub, hsel_half)
        lb2 = lb_all[:, l][:, None, :]
        of, ob = _hgrn(dproj, lb2, tri, bd)
        wb = w_branch[l].astype(BF16)
        wb_att = jnp.swapaxes(wb[:3].reshape(3, N_KV, 2, HEAD_DIM, D_MODEL), 1, 2).reshape(3, MIX_W, D_MODEL)
        wb = jnp.concatenate([wb_att, wb[3:]], axis=0)
        gout = jnp.tile(hgrn_out_norm[l].astype(F32), N_HEADS)[None]
        h = _merge(h, ya, yb, yc, of, ob, dproj, ln_mix[l][None], gout, bd, wgz, wb, w_out[l].astype(BF16))
        h = _ffn(h, ln_ffn[l][None], w_ffn_gate[l].astype(BF16), w_ffn_up[l].astype(BF16), w_ffn_down[l].astype(BF16),
                 ln_final[None], final=l == depth - 1)
    return h[:, META_END:]
```

```python
import functools
import math

import jax
import jax.numpy as jnp
import numpy as np
from jax import lax
from jax.experimental import pallas as pl
from jax.experimental.pallas import tpu as pltpu

D_MODEL = 1024
N_BRANCH = 4
MIX_W = D_MODEL // N_BRANCH
HEAD_DIM = 64
BLOCK = 128
WINDOW = 128
ROPE_THETA = 10000.0
GRID_W = 64
N_HEADS = 4
N_KV = 2
C_DIM = 32
N_META = 16
FRONT = (-N_META) % BLOCK
META_END = FRONT + N_META
N_BUCKETS = 32
MAX_DIST = 128
D_FF = -(-(8 * D_MODEL) // (3 * 256)) * 256
RMS_EPS = 1e-6
NEG = -1e30
F_FLOOR = 1e-30
LOG2E = 1.4426950408889634
LANES = 128
ROW_TILE = 384
DENSE_TILES = (768, 512, 384, 128)
C_UNROLL = 4
Q_TILE = 384
QKV_W = 3 * 512
D_W = 5 * MIX_W
VMEM_LIMIT = 56 * 1024 * 1024

F32 = jnp.float32
BF16 = jnp.bfloat16


def _dot(a, b):
    return jnp.dot(a, b, preferred_element_type=F32)


def _dot_nt(a, b):
    return lax.dot_general(a, b, (((1,), (1,)), ((), ())), preferred_element_type=F32)


def _split_dot(x, sel):
    hi = x.astype(BF16)
    lo = (x - hi.astype(F32)).astype(BF16)
    return _dot(hi, sel) + _dot(lo, sel)


def _rms(x, g):
    return x * lax.rsqrt(jnp.mean(x * x, axis=-1, keepdims=True) + RMS_EPS) * g


def _params(sem):
    return pltpu.CompilerParams(dimension_semantics=sem, vmem_limit_bytes=VMEM_LIMIT)


def _const_spec(shape):
    nd = len(shape)
    return pl.BlockSpec(shape, lambda *_: (0,) * nd, pipeline_mode=pl.Buffered(1))


def _dense_tile(rows):
    return next(t for t in DENSE_TILES if rows % t == 0)


def _in_proj_kernel(h_ref, g_ref, wqkv_ref, wd_ref, cs_ref, cos_ref, sin_ref, gn_ref, hsel_ref,
                    qkv_ref, d_ref, vtb_ref, vtc_ref):
    u = _rms(h_ref[0], g_ref[...]).astype(BF16)
    acc = _dot(u, wqkv_ref[...]) * cs_ref[...]
    qkv_ref[0, :, 0:512] = acc[:, 0:512].astype(BF16)
    for t in range(3):
        c0 = 512 + LANES * t
        xt = acc[:, c0:c0 + LANES]
        ms = _split_dot(xt * xt, hsel_ref[...]) * (1.0 / HEAD_DIM)
        xt = xt * lax.rsqrt(ms + RMS_EPS) * gn_ref[t:t + 1, :]
        xt = xt * cos_ref[...] + pltpu.roll(xt, LANES // 2, 1) * sin_ref[...]
        qkv_ref[0, :, c0:c0 + LANES] = xt.astype(BF16)
    qkv_ref[0, :, 896:QKV_W] = acc[:, 896:QKV_W].astype(BF16)
    vtb_ref[0] = acc[:, 896:1024].T.astype(BF16)
    vtc_ref[0] = acc[:, 1408:QKV_W].T.astype(BF16)
    d_ref[0] = _dot(u, wd_ref[...])


def _in_proj(h, g, wqkv, wd, cs, cos_t, sin_t, gn, hsel):
    bn, L, _ = h.shape
    nt = L // ROW_TILE
    return pl.pallas_call(
        _in_proj_kernel,
        grid=(bn, nt),
        in_specs=[
            pl.BlockSpec((1, ROW_TILE, D_MODEL), lambda b, i: (b, i, 0)),
            _const_spec((1, D_MODEL)),
            _const_spec((D_MODEL, QKV_W)),
            _const_spec((D_MODEL, D_W)),
            _const_spec((1, QKV_W)),
            pl.BlockSpec((ROW_TILE, LANES), lambda b, i: (i, 0)),
            pl.BlockSpec((ROW_TILE, LANES), lambda b, i: (i, 0)),
            _const_spec((3, LANES)),
            _const_spec((LANES, LANES)),
        ],
        out_specs=[
            pl.BlockSpec((1, ROW_TILE, QKV_W), lambda b, i: (b, i, 0)),
            pl.BlockSpec((1, ROW_TILE, D_W), lambda b, i: (b, i, 0)),
            pl.BlockSpec((1, LANES, ROW_TILE), lambda b, i: (b, 0, i)),
            pl.BlockSpec((1, LANES, ROW_TILE), lambda b, i: (b, 0, i)),
        ],
        out_shape=[jax.ShapeDtypeStruct((bn, L, QKV_W), BF16), jax.ShapeDtypeStruct((bn, L, D_W), F32),
                   jax.ShapeDtypeStruct((bn, LANES, L), BF16), jax.ShapeDtypeStruct((bn, LANES, L), BF16)],
        compiler_params=_params(("parallel", "parallel")),
        name="in_proj",
    )(h, g, wqkv, wd, cs, cos_t, sin_t, gn, hsel)


def _stack_q(q_ref, rows, masks):
    parts = []
    for t in range(2):
        qt = q_ref[0, rows, t * LANES:(t + 1) * LANES]
        for m in masks:
            parts.append(jnp.where(m, qt, jnp.zeros_like(qt)))
    return jnp.concatenate(parts, axis=0)


def _blk(ref, j):
    if isinstance(j, int):
        return ref[0, j * BLOCK:(j + 1) * BLOCK, :]
    return ref[0, pl.ds(pl.multiple_of(j * BLOCK, BLOCK), BLOCK), :]


def _attn_a_block(i, nb, rows, sink_ref, q_ref, k_ref, v_ref, bband_ref, bmeta_ref, o_ref):
    prev = jnp.maximum(i - 1, 0)
    nxt = jnp.minimum(i + 1, nb - 1)
    kband = jnp.concatenate([_blk(k_ref, prev), _blk(k_ref, i), _blk(k_ref, nxt)], axis=0)
    vband = jnp.concatenate([_blk(v_ref, prev), _blk(v_ref, i), _blk(v_ref, nxt)], axis=0)
    kmeta = k_ref[0, 0:BLOCK, :]
    vmeta = v_ref[0, 0:BLOCK, :]
    col = lax.broadcasted_iota(jnp.int32, (1, 3 * BLOCK), 1)
    lo = jnp.where(i == 0, 2 * BLOCK, jnp.where(i == 1, BLOCK, 0))
    hi = jnp.where(i == nb - 1, 2 * BLOCK, 3 * BLOCK)
    dead = (col < lo) | (col >= hi)
    lane = lax.broadcasted_iota(jnp.int32, (1, LANES), 1)
    mi = jnp.minimum(i, 2)
    outs = []
    for kv in range(N_KV):
        qs = _stack_q(q_ref, rows, [(lane >= HEAD_DIM) == bool(kv)])
        bb = jnp.concatenate([bband_ref[2 * kv], bband_ref[2 * kv + 1]], axis=0)
        bm = jnp.concatenate([bmeta_ref[mi, 2 * kv], bmeta_ref[mi, 2 * kv + 1]], axis=0)
        sb = jnp.where(dead, NEG, _dot_nt(qs, kband) + bb)
        sm = _dot_nt(qs, kmeta) + bm
        sk = jnp.concatenate([jnp.full((BLOCK, 1), sink_ref[2 * kv], F32),
                              jnp.full((BLOCK, 1), sink_ref[2 * kv + 1], F32)], axis=0)
        m = jnp.maximum(jnp.maximum(jnp.max(sb, axis=-1, keepdims=True), jnp.max(sm, axis=-1, keepdims=True)), sk)
        pb = jnp.exp2(sb - m)
        pm = jnp.exp2(sm - m)
        den = jnp.sum(pb, axis=-1, keepdims=True) + jnp.sum(pm, axis=-1, keepdims=True) + jnp.exp2(sk - m)
        outs.append((_dot(pb.astype(BF16), vband) + _dot(pm.astype(BF16), vmeta)) / den)
    for t in range(2):
        y = jnp.where(lane < HEAD_DIM, outs[0][t * BLOCK:(t + 1) * BLOCK], outs[1][t * BLOCK:(t + 1) * BLOCK])
        o_ref[0, rows, t * LANES:(t + 1) * LANES] = y.astype(BF16)


def _attn_a_kernel(sink_ref, q_ref, k_ref, v_ref, bband_ref, bmeta_ref, o_ref):
    nb = k_ref.shape[1] // BLOCK
    nsub = Q_TILE // BLOCK
    for sb in range(nsub):
        _attn_a_block(pl.program_id(1) * nsub + sb, nb, slice(sb * BLOCK, (sb + 1) * BLOCK),
                      sink_ref, q_ref, k_ref, v_ref, bband_ref, bmeta_ref, o_ref)


def _attn_a(qkv, sink, bband, bmeta):
    bn, L, _ = qkv.shape
    return pl.pallas_call(
        _attn_a_kernel,
        grid=(bn, L // Q_TILE),
        in_specs=[
            pl.BlockSpec(memory_space=pltpu.SMEM),
            pl.BlockSpec((1, Q_TILE, 2 * LANES), lambda b, i: (b, i, 0)),
            pl.BlockSpec((1, L, LANES), lambda b, i: (b, 0, 2)),
            pl.BlockSpec((1, L, LANES), lambda b, i: (b, 0, 3)),
            _const_spec((N_HEADS, BLOCK, 3 * BLOCK)),
            _const_spec((3, N_HEADS, BLOCK, BLOCK)),
        ],
        out_specs=pl.BlockSpec((1, Q_TILE, MIX_W), lambda b, i: (b, i, 0)),
        out_shape=jax.ShapeDtypeStruct((bn, L, MIX_W), BF16),
        compiler_params=_params(("parallel", "parallel")),
        name="attn_window",
    )(sink, qkv, qkv, qkv, bband, bmeta)


def _stack_maps(q_ref, masks_per_kv):
    parts = []
    for masks in masks_per_kv:
        for t in range(2):
            qt = q_ref[0, :, t * LANES:(t + 1) * LANES].astype(F32)
            for m in masks:
                parts.append(jnp.where(m, qt, 0.0).T.astype(BF16))
    return jnp.concatenate(parts, axis=1)


def _attn_sweep(qs_ref, k_ref, vt_ref, bias_fn, sa_ref, sb_ref, m_ref, acc_ref):
    nb = k_ref.shape[1] // BLOCK
    W = qs_ref.shape[1]
    half = W // 2
    ngrp = (nb - 1) // C_UNROLL
    keyrow = lax.broadcasted_iota(jnp.int32, (BLOCK, 1), 0)

    def rows(ref, j0, width):
        if isinstance(j0, int):
            return ref[0, j0 * BLOCK:(j0 + width) * BLOCK, :]
        return ref[0, pl.ds(pl.multiple_of(j0 * BLOCK, BLOCK), width * BLOCK), :]

    def cols(ref, j0, width):
        if isinstance(j0, int):
            return ref[0, :, j0 * BLOCK:(j0 + width) * BLOCK]
        return ref[0, :, pl.ds(pl.multiple_of(j0 * BLOCK, BLOCK), width * BLOCK)]

    def scores(j0, width):
        s_all = _dot(rows(k_ref, j0, width), qs_ref[...])
        parts = []
        for u in range(width):
            s = s_all[u * BLOCK:(u + 1) * BLOCK]
            b = bias_fn(j0 + u)
            if b is not None:
                s = s + b
            if isinstance(j0, int) and j0 + u == 0:
                s = jnp.where(keyrow >= FRONT, s, NEG)
            parts.append(s)
        return parts[0] if width == 1 else jnp.concatenate(parts, axis=0)

    def absorb(s, j0, width, first=False):
        mg = jnp.max(jnp.max(s.reshape(width * BLOCK // 8, 8, W), axis=0), axis=0, keepdims=True)
        m_new = mg if first else jnp.maximum(m_ref[...], mg)
        p = jnp.exp2(s - m_new).astype(BF16)
        vt = cols(vt_ref, j0, width)
        ones = jnp.ones((16, width * BLOCK), BF16)
        if not first:
            alpha = jnp.exp2(m_ref[...] - m_new)
        for kv in range(N_KV):
            w = _dot(jnp.concatenate([vt[kv * HEAD_DIM:(kv + 1) * HEAD_DIM], ones], axis=0),
                     p[:, kv * half:(kv + 1) * half])
            acc_ref[kv] = w if first else acc_ref[kv] * alpha[:, kv * half:(kv + 1) * half] + w
        m_ref[...] = m_new

    def group(n):
        return 1 + n * C_UNROLL

    absorb(scores(0, 1), 0, 1, first=True)
    if ngrp > 0:
        sa_ref[...] = scores(group(0), C_UNROLL)
        npair = (ngrp - 1) // 2

        def body(n, carry):
            g = 2 * n
            sb_ref[...] = scores(group(g + 1), C_UNROLL)
            absorb(sa_ref[...], group(g), C_UNROLL)
            sa_ref[...] = scores(group(g + 2), C_UNROLL)
            absorb(sb_ref[...], group(g + 1), C_UNROLL)
            return carry

        lax.fori_loop(0, npair, body, 0)
        g = 2 * npair
        if ngrp - g == 2:
            sb_ref[...] = scores(group(g + 1), C_UNROLL)
            absorb(sa_ref[...], group(g), C_UNROLL)
            absorb(sb_ref[...], group(g + 1), C_UNROLL)
        else:
            absorb(sa_ref[...], group(g), C_UNROLL)
    for j in range(1 + ngrp * C_UNROLL, nb):
        absorb(scores(j, 1), j, 1)


def _sweep_scratch(nmaps):
    W = nmaps * Q_TILE
    return [pltpu.VMEM((LANES, W), BF16), pltpu.VMEM((C_UNROLL * BLOCK, W), F32), pltpu.VMEM((C_UNROLL * BLOCK, W), F32),
            pltpu.VMEM((1, W), F32), pltpu.VMEM((N_KV, HEAD_DIM + 16, W // 2), F32)]


def _attn_b_kernel(q_ref, k_ref, vt_ref, o_ref, qs_ref, sa_ref, sb_ref, m_ref, acc_ref):
    lane = lax.broadcasted_iota(jnp.int32, (1, LANES), 1)
    head_b = (lane // (HEAD_DIM // 2)) % 2
    qs_ref[...] = _stack_maps(q_ref, [[head_b == kv] for kv in range(N_KV)])
    _attn_sweep(qs_ref, k_ref, vt_ref, lambda j: None, sa_ref, sb_ref, m_ref, acc_ref)
    ot = [acc_ref[kv][0:HEAD_DIM] / acc_ref[kv][HEAD_DIM:HEAD_DIM + 1] for kv in range(N_KV)]
    for g in range(2):
        yt = jnp.concatenate([ot[kv][:, g * Q_TILE:(g + 1) * Q_TILE] for kv in range(N_KV)], axis=0)
        o_ref[0, :, g * LANES:(g + 1) * LANES] = yt.T.astype(BF16)


def _attn_b(qkv, vt):
    bn, L, _ = qkv.shape
    return pl.pallas_call(
        _attn_b_kernel,
        grid=(bn, L // Q_TILE),
        in_specs=[
            pl.BlockSpec((1, Q_TILE, 2 * LANES), lambda b, i: (b, i, 2)),
            pl.BlockSpec((1, L, LANES), lambda b, i: (b, 0, 6)),
            pl.BlockSpec((1, LANES, L), lambda b, i: (b, 0, 0)),
        ],
        out_specs=pl.BlockSpec((1, Q_TILE, MIX_W), lambda b, i: (b, i, 0)),
        out_shape=jax.ShapeDtypeStruct((bn, L, MIX_W), BF16),
        scratch_shapes=_sweep_scratch(4),
        compiler_params=_params(("parallel", "parallel")),
        name="attn_rope",
    )(qkv, qkv, vt)


def _attn_c_kernel(sc_ref, q_ref, k_ref, vt_ref, bias_ref, gsub_ref, hsel_ref, o_ref,
                   qs_ref, sa_ref, sb_ref, m_ref, acc_ref):
    i = pl.program_id(1)
    lam = sc_ref[0]
    lane = lax.broadcasted_iota(jnp.int32, (1, LANES), 1)
    grp = lane // C_DIM
    qs_ref[...] = _stack_maps(q_ref, [[grp == 2 * kv, grp == 2 * kv + 1] for kv in range(N_KV)])

    def bias(j):
        nsub = Q_TILE // BLOCK
        ts = [jnp.clip(j - (i * nsub + sb) + 2, 0, 4) for sb in range(nsub)]
        per_head = [jnp.concatenate([bias_ref[t, h] for t in ts], axis=1) for h in range(N_HEADS)]
        return jnp.concatenate([per_head[h] for h in range(N_HEADS) for _ in range(2)], axis=1)

    _attn_sweep(qs_ref, k_ref, vt_ref, bias, sa_ref, sb_ref, m_ref, acc_ref)
    ot = [acc_ref[kv][0:HEAD_DIM] / acc_ref[kv][HEAD_DIM:HEAD_DIM + 1] for kv in range(N_KV)]
    for g in range(2):
        dt = [ot[kv][:, (2 * g) * Q_TILE:(2 * g + 1) * Q_TILE] - lam * ot[kv][:, (2 * g + 1) * Q_TILE:(2 * g + 2) * Q_TILE]
              for kv in range(N_KV)]
        y = jnp.concatenate(dt, axis=0).T
        ms = _split_dot(y * y, hsel_ref[...]) * (1.0 / HEAD_DIM)
        y = y * lax.rsqrt(ms + RMS_EPS) * gsub_ref[...] * sc_ref[1]
        o_ref[0, :, g * LANES:(g + 1) * LANES] = y.astype(BF16)


def _attn_c(qkv, vt, scal, bias5t, gsub, hsel):
    bn, L, _ = qkv.shape
    return pl.pallas_call(
        _attn_c_kernel,
        grid=(bn, L // Q_TILE),
        in_specs=[
            pl.BlockSpec(memory_space=pltpu.SMEM),
            pl.BlockSpec((1, Q_TILE, 2 * LANES), lambda b, i: (b, i, 4)),
            pl.BlockSpec((1, L, LANES), lambda b, i: (b, 0, 10)),
            pl.BlockSpec((1, LANES, L), lambda b, i: (b, 0, 0)),
            _const_spec((5, N_HEADS, BLOCK, BLOCK)),
            _const_spec((1, LANES)),
            _const_spec((LANES, LANES)),
        ],
        out_specs=pl.BlockSpec((1, Q_TILE, MIX_W), lambda b, i: (b, i, 0)),
        out_shape=jax.ShapeDtypeStruct((bn, L, MIX_W), BF16),
        scratch_shapes=_sweep_scratch(8),
        compiler_params=_params(("parallel", "parallel")),
        name="attn_diff",
    )(scal, qkv, qkv, vt, bias5t, gsub, hsel)


def _split3_dot(sel, x):
    x1 = x.astype(BF16)
    r1 = x - x1.astype(F32)
    x2 = r1.astype(BF16)
    x3 = (r1 - x2.astype(F32)).astype(BF16)
    return _dot(sel, x1) + _dot(sel, x2) + _dot(sel, x3)


def _hgrn_chunk(forward, q, kk, g, v, tri_ref, bd_ref, st_ref):
    W = MIX_W
    b = _split3_dot(tri_ref[...], g)
    tot = b[BLOCK - 1:BLOCK, :]
    if forward:
        x, e = b, tot - b
    else:
        x, e = tot - b + g, b - g
    sub = lax.broadcasted_iota(jnp.int32, (1, 8, 1), 1)
    x3, q3, k3, v3 = (a.reshape(BLOCK // 8, 8, W) for a in (x, q, kk, v))
    o = _dot((q * kk).astype(BF16), bd_ref[...]) * v
    for d in range(1, 8):
        sh = d if forward else 8 - d
        ok = (sub >= d) if forward else (sub + d < 8)
        e_d = jnp.exp2(jnp.where(ok, x3 - pltpu.roll(x3, sh, 1), NEG)) * q3 * pltpu.roll(k3, sh, 1)
        o = o + _dot(e_d.reshape(BLOCK, W).astype(BF16), bd_ref[...]) * pltpu.roll(v3, sh, 1).reshape(BLOCK, W)
    row = lax.broadcasted_iota(jnp.int32, (BLOCK, 1), 0)
    qcol = lax.broadcasted_iota(jnp.int32, (1, N_HEADS * BLOCK), 1) % BLOCK
    lane_head = lax.broadcasted_iota(jnp.int32, (1, W), 1) // HEAD_DIM
    at = jnp.zeros((BLOCK, N_HEADS * BLOCK), F32)
    for m in (8, 16, 32, 64):
        mid = m - 1 if forward else m
        xm = jnp.concatenate([jnp.broadcast_to(x[i + mid:i + mid + 1, :], (2 * m, W))
                              for i in range(0, BLOCK, 2 * m)], axis=0)
        second = (row // m) % 2 == 1
        qside = second if forward else jnp.logical_not(second)
        qt = (q * jnp.exp2(jnp.where(qside, x - xm, NEG))).astype(BF16)
        kt = (kk * jnp.exp2(jnp.where(qside, NEG, xm - x))).astype(BF16)
        qstack = jnp.concatenate([jnp.where(lane_head == h, qt, jnp.zeros_like(qt)) for h in range(N_HEADS)], axis=0)
        a = _dot_nt(kt, qstack)
        at = at + jnp.where(row // (2 * m) == qcol // (2 * m), a, 0.0)
    vt = v.T.astype(BF16)
    ot = _dot(vt, at.astype(BF16))
    o = o + jnp.concatenate([ot[h * HEAD_DIM:(h + 1) * HEAD_DIM, h * BLOCK:(h + 1) * BLOCK]
                             for h in range(N_HEADS)], axis=0).T
    qd = (q * jnp.exp2(x)).astype(BF16)
    kd = (kk * jnp.exp2(e)).astype(BF16)
    gam = jnp.exp2(tot)
    hmask = (lax.broadcasted_iota(jnp.int32, (LANES, LANES), 0) // HEAD_DIM
             == lax.broadcasted_iota(jnp.int32, (LANES, LANES), 1) // HEAD_DIM)
    inter = []
    for hf in range(2):
        lanes = slice(hf * LANES, (hf + 1) * LANES)
        st = st_ref[hf]
        inter.append(_dot_nt(qd[:, lanes], st.astype(BF16)))
        st_ref[hf] = st * gam[:, lanes] + jnp.where(hmask, _dot(vt[lanes, :], kd[:, lanes]), 0.0)
    return o + jnp.concatenate(inter, axis=1)


def _hgrn_direction(forward, first_tile, q_ref, z_ref, v_ref, lb_ref, tri_ref, bd_ref, o_ref, st_ref):
    R = q_ref.shape[1]
    z = z_ref[0]
    q = q_ref[0] * (HEAD_DIM ** -0.5)
    v = v_ref[0]
    lb = lb_ref[...]
    row = lax.broadcasted_iota(jnp.int32, (R, 1), 0)
    sig = jax.nn.sigmoid(z)
    f = lb + (1.0 - lb) * sig
    g = jnp.log(jnp.maximum(f, F_FLOOR)) * LOG2E
    kk = (1.0 - lb) * (1.0 - sig)
    kk = jnp.where(row < jnp.where(first_tile, FRONT, 0), 0.0, kk)
    nblk = R // BLOCK
    for n in range(nblk):
        r0 = (n if forward else nblk - 1 - n) * BLOCK
        rows = slice(r0, r0 + BLOCK)
        o_ref[0, rows, :] = _hgrn_chunk(forward, q[rows], kk[rows], g[rows], v[rows], tri_ref, bd_ref, st_ref)


def _hgrn_kernel(qf_ref, zf_ref, vf_ref, qb_ref, zb_ref, vb_ref, lb_ref, tri_ref, bd_ref, of_ref, ob_ref, st_ref):
    j = pl.program_id(1)
    nt = pl.num_programs(1)

    @pl.when(j == 0)
    def _():
        st_ref[...] = jnp.zeros_like(st_ref)

    _hgrn_direction(True, j == 0, qf_ref, zf_ref, vf_ref, lb_ref.at[0], tri_ref, bd_ref, of_ref, st_ref.at[0])
    _hgrn_direction(False, j == nt - 1, qb_ref, zb_ref, vb_ref, lb_ref.at[1], tri_ref, bd_ref, ob_ref, st_ref.at[1])


def _hgrn(dproj, lb2, tri, bd):
    bn, L, _ = dproj.shape
    nt = L // ROW_TILE
    fw = lambda c: pl.BlockSpec((1, ROW_TILE, MIX_W), lambda b, j: (b, j, c))
    bw = lambda c: pl.BlockSpec((1, ROW_TILE, MIX_W), lambda b, j: (b, nt - 1 - j, c))
    return pl.pallas_call(
        _hgrn_kernel,
        grid=(bn, nt),
        in_specs=[fw(0), fw(1), fw(3), bw(0), bw(2), bw(3), _const_spec((2, 1, MIX_W)), _const_spec((BLOCK, BLOCK)),
                  _const_spec((MIX_W, MIX_W))],
        out_specs=[pl.BlockSpec((1, ROW_TILE, MIX_W), lambda b, j: (b, j, 0)),
                   pl.BlockSpec((1, ROW_TILE, MIX_W), lambda b, j: (b, nt - 1 - j, 0))],
        out_shape=[jax.ShapeDtypeStruct((bn, L, MIX_W), F32)] * 2,
        scratch_shapes=[pltpu.VMEM((2, 2, LANES, LANES), F32)],
        compiler_params=_params(("parallel", "arbitrary")),
        name="hgrn",
    )(dproj, dproj, dproj, dproj, dproj, dproj, lb2, tri, bd)


def _merge_kernel(h_ref, ya_ref, yb_ref, yc_ref, of_ref, ob_ref, dg_ref, g_ref, gout_ref, bd_ref,
                  wgz_ref, wb_ref, wo_ref, o_ref):
    h = h_ref[0]
    u = _rms(h, g_ref[...]).astype(BF16)
    od = of_ref[0] + ob_ref[0]
    ms = _split_dot(od * od, bd_ref[...]) * (1.0 / HEAD_DIM)
    yd = (od * lax.rsqrt(ms + RMS_EPS) * gout_ref[...] * jax.nn.silu(dg_ref[0])).astype(BF16)
    ys = (ya_ref[0], yb_ref[0], yc_ref[0], yd)
    merged = None
    for n in range(N_BRANCH):
        gate = jax.nn.sigmoid(_dot(u, wgz_ref[:, n * D_MODEL:(n + 1) * D_MODEL]))
        term = gate * _dot(ys[n], wb_ref[n])
        merged = term if merged is None else merged + term
    o_ref[0] = h + _dot(merged.astype(BF16), wo_ref[...])


def _merge(h, ya, yb, yc, of, ob, dproj, g, gout, bd, wgz, wb, wo):
    bn, L, _ = h.shape
    rows = bn * L
    tm = _dense_tile(rows)
    flat = lambda a: a.reshape(1, rows, a.shape[-1])
    h, ya, yb, yc, of, ob, dproj = (flat(a) for a in (h, ya, yb, yc, of, ob, dproj))
    row = lambda w, c=0: pl.BlockSpec((1, tm, w), lambda i: (0, i, c))
    out = pl.pallas_call(
        _merge_kernel,
        grid=(rows // tm,),
        in_specs=[row(D_MODEL), row(MIX_W), row(MIX_W), row(MIX_W), row(MIX_W), row(MIX_W), row(MIX_W, 4),
                  _const_spec((1, D_MODEL)), _const_spec((1, MIX_W)), _const_spec((MIX_W, MIX_W)),
                  _const_spec((D_MODEL, N_BRANCH * D_MODEL)), _const_spec((N_BRANCH, MIX_W, D_MODEL)),
                  _const_spec((D_MODEL, D_MODEL))],
        out_specs=row(D_MODEL),
        out_shape=jax.ShapeDtypeStruct((1, rows, D_MODEL), F32),
        compiler_params=_params(("parallel",)),
        name="merge",
    )(h, ya, yb, yc, of, ob, dproj, g, gout, bd, wgz, wb, wo)
    return out.reshape(bn, L, D_MODEL)


def _ffn_kernel(h_ref, g_ref, wg_ref, wu_ref, wd_ref, gfin_ref, o_ref, *, final):
    h = h_ref[0]
    u = _rms(h, g_ref[...]).astype(BF16)
    half = D_FF // 2
    out = h
    for c in range(2):
        a = _dot(u, wg_ref[:, c * half:(c + 1) * half])
        t = (jax.nn.silu(a) * _dot(u, wu_ref[:, c * half:(c + 1) * half])).astype(BF16)
        out = out + _dot(t, wd_ref[c * half:(c + 1) * half, :])
    o_ref[0] = _rms(out, gfin_ref[...]) if final else out


def _ffn(h, g, wg, wu, wd, gfin, final):
    bn, L, _ = h.shape
    weights = [_const_spec((1, D_MODEL)), _const_spec((D_MODEL, D_FF)), _const_spec((D_MODEL, D_FF)),
               _const_spec((D_FF, D_MODEL)), _const_spec((1, D_MODEL))]
    kern = functools.partial(_ffn_kernel, final=final)
    if final:
        S = L - META_END
        tm = _dense_tile(S)
        return pl.pallas_call(
            kern,
            grid=(bn, S // tm),
            in_specs=[pl.BlockSpec((pl.Element(1), pl.Element(tm), pl.Element(D_MODEL)),
                                   lambda b, i: (b, pl.multiple_of(META_END + i * tm, BLOCK), 0))] + weights,
            out_specs=pl.BlockSpec((1, tm, D_MODEL), lambda b, i: (b, i, 0)),
            out_shape=jax.ShapeDtypeStruct((bn, S, D_MODEL), F32),
            compiler_params=_params(("parallel", "parallel")),
            name="ffn_out",
        )(h, g, wg, wu, wd, gfin)
    rows = bn * L
    tm = _dense_tile(rows)
    row = pl.BlockSpec((1, tm, D_MODEL), lambda i: (0, i, 0))
    out = pl.pallas_call(
        kern,
        grid=(rows // tm,),
        in_specs=[row] + weights,
        out_specs=row,
        out_shape=jax.ShapeDtypeStruct((1, rows, D_MODEL), F32),
        compiler_params=_params(("parallel",)),
        name="ffn",
    )(h.reshape(1, rows, D_MODEL), g, wg, wu, wd, gfin)
    return out.reshape(bn, L, D_MODEL)


def _t5_bucket(rel):
    half = N_BUCKETS // 2
    exact = half // 2
    n = jnp.abs(rel)
    nf = jnp.maximum(n, exact).astype(F32)
    big = exact + (jnp.log(nf / exact) / math.log(MAX_DIST / exact) * (half - exact)).astype(jnp.int32)
    big = jnp.minimum(big, half - 1)
    return jnp.where(rel > 0, half, 0) + jnp.where(n < exact, n, big)


def _lookup(table, bucket):
    hit = bucket[None, ..., None] == jnp.arange(N_BUCKETS)
    return jnp.sum(jnp.where(hit, table.T.reshape((N_HEADS,) + (1,) * bucket.ndim + (N_BUCKETS,)), 0.0), axis=-1)


def _bias_tables(rel_bias):
    bias_a = rel_bias[:, :N_HEADS].astype(F32) * LOG2E
    bias_c = rel_bias[:, N_HEADS:].astype(F32) * LOG2E
    r = jnp.arange(BLOCK)[:, None]
    rel_band = (jnp.arange(3 * BLOCK)[None, :] - BLOCK) - r
    bband = jnp.where(jnp.abs(rel_band)[None] <= WINDOW, _lookup(bias_a, _t5_bucket(rel_band)), NEG)
    lane = jnp.arange(BLOCK)[None, :]
    rel_m = lane[None] - (jnp.arange(3)[:, None, None] * BLOCK + r[None])
    bmeta = jnp.where((lane >= FRONT)[None, None], jnp.moveaxis(_lookup(bias_a, _t5_bucket(rel_m)), 0, 1), NEG)
    rel5t = (jnp.arange(5)[:, None, None] - 2) * BLOCK + r[None] - lane[None]
    bias5t = jnp.moveaxis(_lookup(bias_c, _t5_bucket(rel5t)), 0, 1)
    return bband, bmeta, bias5t


def _rope_layout():
    return np.concatenate([np.zeros(32, int), np.ones(32, int)] * 2)


def _rope_cols(w, heads):
    lead = w.shape[:-1]
    w = w.reshape(lead + (heads // 2, 2, 2, 2, HEAD_DIM // 4))
    if heads == N_HEADS:
        w = jnp.moveaxis(w, (-5, -4, -3, -2), (-3, -5, -2, -4))
    else:
        w = jnp.moveaxis(w, (-4, -3, -2), (-3, -2, -4))
    return w.reshape(lead + (heads * HEAD_DIM,))


def _pair_cols(w):
    lead = w.shape[:-1]
    return jnp.swapaxes(w.reshape(lead + (N_KV, 2, HEAD_DIM)), -3, -2).reshape(lead + (N_HEADS * HEAD_DIM,))


def _rope_tables(L, S):
    rows = S // GRID_W
    row = jnp.concatenate([jnp.zeros((FRONT,), jnp.int32), -jnp.ones((N_META,), jnp.int32),
                           jnp.repeat(jnp.arange(rows, dtype=jnp.int32), GRID_W)]).astype(F32)
    col = jnp.concatenate([jnp.zeros((FRONT,), jnp.int32), jnp.arange(N_META, dtype=jnp.int32),
                           jnp.tile(jnp.arange(GRID_W, dtype=jnp.int32), rows)]).astype(F32)
    half = HEAD_DIM // 2
    inv = ROPE_THETA ** (-jnp.arange(0, half, 2, dtype=F32) / half)
    ang = jnp.concatenate([row[:, None] * inv[None, :], col[:, None] * inv[None, :]], axis=-1)
    cos_t = jnp.tile(jnp.cos(ang), (1, 4))
    sin_t = jnp.concatenate([-jnp.tile(jnp.sin(ang), (1, 2)), jnp.tile(jnp.sin(ang), (1, 2))], axis=-1)
    return cos_t, sin_t


def _rope_gain(g):
    g = jnp.swapaxes(g.astype(F32).reshape(2, 2, HEAD_DIM // 4), 0, 1).reshape(2, 1, HEAD_DIM // 2)
    return jnp.broadcast_to(g, (2, 2, HEAD_DIM // 2)).reshape(LANES)


def _qkv_weight(wl):
    a, b, c = wl[:, 0:512], wl[:, 512:1024], wl[:, 1024:1536]
    return jnp.concatenate([_pair_cols(a[:, :256]), a[:, 256:],
                            _rope_cols(b[:, :256], N_HEADS), _rope_cols(b[:, 256:384], N_KV), b[:, 384:],
                            _pair_cols(c[:, :256]), c[:, 256:]], axis=1)


def kernel(x, meta_tokens, rel_bias, hgrn_lb_logits, ln_mix, w_in, attn_sink, qk_norm_q, qk_norm_k, diff_lambda,
           diff_subnorm, hgrn_out_norm, w_branch, w_out, ln_ffn, w_ffn_gate, w_ffn_up, w_ffn_down, ln_final):
    bn, S, _ = x.shape
    L = META_END + S
    depth = w_in.shape[0]
    assert L % ROW_TILE == 0 and L % Q_TILE == 0 and S % GRID_W == 0
    h = jnp.concatenate([jnp.zeros((bn, FRONT, D_MODEL), x.dtype),
                         jnp.broadcast_to(meta_tokens.astype(x.dtype)[None], (bn, N_META, D_MODEL)), x], axis=1)
    bband, bmeta, bias5t = _bias_tables(rel_bias)
    cos_t, sin_t = _rope_tables(L, S)
    rhead = _rope_layout()
    lane = np.arange(LANES)
    hsel_rope = jnp.asarray(rhead[:, None] == rhead[None, :], BF16)
    hsel_half = jnp.asarray((lane[:, None] // HEAD_DIM) == (lane[None, :] // HEAD_DIM), BF16)
    w256 = np.arange(MIX_W)
    bd = jnp.asarray((w256[:, None] // HEAD_DIM) == (w256[None, :] // HEAD_DIM), BF16)
    tri = jnp.asarray(np.tril(np.ones((BLOCK, BLOCK))), BF16)
    cs = np.ones((1, QKV_W), np.float32)
    cs[0, 0:256] = HEAD_DIM ** -0.5 * LOG2E
    cs[0, 1024:1280] = C_DIM ** -0.5 * LOG2E
    cs = jnp.asarray(cs)
    lb_p = jax.nn.softmax(hgrn_lb_logits.astype(F32), axis=1)
    lb_all = jnp.cumsum(lb_p, axis=1) - lb_p[:, :1]

    for l in range(depth):
        wl = w_in[l]
        wqkv = _qkv_weight(wl).astype(BF16)
        wd = wl[:, QKV_W:QKV_W + D_W].astype(BF16)
        wgz = wl[:, QKV_W + D_W:].astype(BF16)
        gn = jnp.stack([_rope_gain(qk_norm_q[l]) * (HEAD_DIM ** -0.5 * LOG2E)] * 2 + [_rope_gain(qk_norm_k[l])])
        qkv, dproj, vtb, vtc = _in_proj(h, ln_mix[l][None], wqkv, wd, cs, cos_t, sin_t, gn, hsel_rope)
        ya = _attn_a(qkv, attn_sink[l].astype(F32) * LOG2E, bband, bmeta)
        yb = _attn_b(qkv, vtb)
        lam_init = 0.8 - 0.6 * math.exp(-0.3 * l)
        lam_p = diff_lambda[l].astype(F32)
        lam = jnp.exp(jnp.sum(lam_p[0] * lam_p[1])) - jnp.exp(jnp.sum(lam_p[2] * lam_p[3])) + lam_init
        scal = jnp.stack([lam, jnp.asarray(1.0 - lam_init, F32)])
        gsub = jnp.tile(diff_subnorm[l].astype(F32), 2)[None]
        yc = _attn_c(qkv, vtc, scal, bias5t, gsub, hsel_half)
        lb2 = lb_all[:, l][:, None, :]
        of, ob = _hgrn(dproj, lb2, tri, bd)
        wb = w_branch[l].astype(BF16)
        wb_att = jnp.swapaxes(wb[:3].reshape(3, N_KV, 2, HEAD_DIM, D_MODEL), 1, 2).reshape(3, MIX_W, D_MODEL)
        wb = jnp.concatenate([wb_att, wb[3:]], axis=0)
        gout = jnp.tile(hgrn_out_norm[l].astype(F32), N_HEADS)[None]
        h = _merge(h, ya, yb, yc, of, ob, dproj, ln_mix[l][None], gout, bd, wgz, wb, w_out[l].astype(BF16))
        h = _ffn(h, ln_ffn[l][None], w_ffn_gate[l].astype(BF16), w_ffn_up[l].astype(BF16), w_ffn_down[l].astype(BF16),
                 ln_final[None], final=l == depth - 1)
    return h
```

```python
import functools
import math

import jax
import jax.numpy as jnp
import numpy as np
from jax import lax
from jax.experimental import pallas as pl
from jax.experimental.pallas import tpu as pltpu

D_MODEL = 1024
N_BRANCH = 4
MIX_W = D_MODEL // N_BRANCH
HEAD_DIM = 64
BLOCK = 128
WINDOW = 128
ROPE_THETA = 10000.0
GRID_W = 64
N_HEADS = 4
N_KV = 2
C_DIM = 32
N_META = 16
FRONT = (-N_META) % BLOCK
META_END = FRONT + N_META
N_BUCKETS = 32
MAX_DIST = 128
D_FF = -(-(8 * D_MODEL) // (3 * 256)) * 256
RMS_EPS = 1e-6
NEG = -1e30
F_FLOOR = 1e-30
LOG2E = 1.4426950408889634
LANES = 128
ROW_TILE = 384
DENSE_TILES = (768, 512, 384, 128)
C_UNROLL = 4
Q_TILE = 384
Q_TILE_ROPE = 1408
QKV_W = 3 * 512
D_W = 5 * MIX_W
VMEM_LIMIT = 56 * 1024 * 1024

F32 = jnp.float32
BF16 = jnp.bfloat16


def _dot(a, b):
    return jnp.dot(a, b, preferred_element_type=F32)


def _dot_nt(a, b):
    return lax.dot_general(a, b, (((1,), (1,)), ((), ())), preferred_element_type=F32)


def _split_dot(x, sel):
    hi = x.astype(BF16)
    lo = (x - hi.astype(F32)).astype(BF16)
    return _dot(hi, sel) + _dot(lo, sel)


def _rms(x, g):
    return x * lax.rsqrt(jnp.mean(x * x, axis=-1, keepdims=True) + RMS_EPS) * g


def _params(sem):
    return pltpu.CompilerParams(dimension_semantics=sem, vmem_limit_bytes=VMEM_LIMIT)


def _const_spec(shape):
    nd = len(shape)
    return pl.BlockSpec(shape, lambda *_: (0,) * nd, pipeline_mode=pl.Buffered(1))


def _dense_tile(rows):
    return next(t for t in DENSE_TILES if rows % t == 0)


def _in_proj_kernel(*refs, embed):
    if embed:
        x_ref, pm_ref, refs, h_out_ref = refs[0], refs[1], refs[2:-1], refs[-1]
        xt = x_ref[0]
        first = jnp.concatenate([pm_ref[...], xt[:ROW_TILE - META_END]], axis=0)
        h = jnp.where(pl.program_id(1) == 0, first, xt)
        h_out_ref[0] = h
    else:
        h_ref, refs = refs[0], refs[1:]
        h = h_ref[0]
    g_ref, wqkv_ref, wd_ref, cs_ref, cos_ref, sin_ref, gn_ref, hsel_ref, qkv_ref, d_ref, vtb_ref, vtc_ref = refs
    u = _rms(h, g_ref[...]).astype(BF16)
    acc = _dot(u, wqkv_ref[...]) * cs_ref[...]
    qkv_ref[0, :, 0:512] = acc[:, 0:512].astype(BF16)
    for t in range(3):
        c0 = 512 + LANES * t
        xt = acc[:, c0:c0 + LANES]
        ms = _split_dot(xt * xt, hsel_ref[...]) * (1.0 / HEAD_DIM)
        xt = xt * lax.rsqrt(ms + RMS_EPS) * gn_ref[t:t + 1, :]
        xt = xt * cos_ref[...] + pltpu.roll(xt, LANES // 2, 1) * sin_ref[...]
        qkv_ref[0, :, c0:c0 + LANES] = xt.astype(BF16)
    qkv_ref[0, :, 896:QKV_W] = acc[:, 896:QKV_W].astype(BF16)
    vtb_ref[0] = acc[:, 896:1024].T.astype(BF16)
    vtc_ref[0] = acc[:, 1408:QKV_W].T.astype(BF16)
    d_ref[0] = _dot(u, wd_ref[...])


def _in_proj(h, g, wqkv, wd, cs, cos_t, sin_t, gn, hsel, padmeta=None):
    embed = padmeta is not None
    bn = h.shape[0]
    L = h.shape[1] + (META_END if embed else 0)
    nt = L // ROW_TILE
    row = lambda w: pl.BlockSpec((1, ROW_TILE, w), lambda b, i: (b, i, 0))
    vt = pl.BlockSpec((1, LANES, ROW_TILE), lambda b, i: (b, 0, i))
    if embed:
        src = [pl.BlockSpec((pl.Element(1), pl.Element(ROW_TILE), pl.Element(D_MODEL)),
                            lambda b, i: (b, pl.multiple_of(jnp.maximum(i * ROW_TILE - META_END, 0), BLOCK), 0)),
               _const_spec((META_END, D_MODEL))]
        args = (h, padmeta)
    else:
        src, args = [row(D_MODEL)], (h,)
    out_specs = [row(QKV_W), row(D_W), vt, vt] + ([row(D_MODEL)] if embed else [])
    out_shape = [jax.ShapeDtypeStruct((bn, L, QKV_W), BF16), jax.ShapeDtypeStruct((bn, L, D_W), F32),
                 jax.ShapeDtypeStruct((bn, LANES, L), BF16), jax.ShapeDtypeStruct((bn, LANES, L), BF16)]
    return pl.pallas_call(
        functools.partial(_in_proj_kernel, embed=embed),
        grid=(bn, nt),
        in_specs=src + [
            _const_spec((1, D_MODEL)),
            _const_spec((D_MODEL, QKV_W)),
            _const_spec((D_MODEL, D_W)),
            _const_spec((1, QKV_W)),
            pl.BlockSpec((ROW_TILE, LANES), lambda b, i: (i, 0)),
            pl.BlockSpec((ROW_TILE, LANES), lambda b, i: (i, 0)),
            _const_spec((3, LANES)),
            _const_spec((LANES, LANES)),
        ],
        out_specs=out_specs,
        out_shape=out_shape + ([jax.ShapeDtypeStruct((bn, L, D_MODEL), F32)] if embed else []),
        compiler_params=_params(("parallel", "parallel")),
        name="in_proj",
    )(*args, g, wqkv, wd, cs, cos_t, sin_t, gn, hsel)


def _stack_q(q_ref, rows, masks):
    parts = []
    for t in range(2):
        qt = q_ref[0, rows, t * LANES:(t + 1) * LANES]
        for m in masks:
            parts.append(jnp.where(m, qt, jnp.zeros_like(qt)))
    return jnp.concatenate(parts, axis=0)


def _blk(ref, j):
    if isinstance(j, int):
        return ref[0, j * BLOCK:(j + 1) * BLOCK, :]
    return ref[0, pl.ds(pl.multiple_of(j * BLOCK, BLOCK), BLOCK), :]


def _attn_a_block(i, nb, rows, sink_ref, q_ref, k_ref, v_ref, bband_ref, bmeta_ref, o_ref):
    prev = jnp.maximum(i - 1, 0)
    nxt = jnp.minimum(i + 1, nb - 1)
    kband = jnp.concatenate([_blk(k_ref, prev), _blk(k_ref, i), _blk(k_ref, nxt)], axis=0)
    vband = jnp.concatenate([_blk(v_ref, prev), _blk(v_ref, i), _blk(v_ref, nxt)], axis=0)
    kmeta = k_ref[0, 0:BLOCK, :]
    vmeta = v_ref[0, 0:BLOCK, :]
    col = lax.broadcasted_iota(jnp.int32, (1, 3 * BLOCK), 1)
    lo = jnp.where(i == 0, 2 * BLOCK, jnp.where(i == 1, BLOCK, 0))
    hi = jnp.where(i == nb - 1, 2 * BLOCK, 3 * BLOCK)
    dead = (col < lo) | (col >= hi)
    lane = lax.broadcasted_iota(jnp.int32, (1, LANES), 1)
    mi = jnp.minimum(i, 2)
    outs = []
    for kv in range(N_KV):
        qs = _stack_q(q_ref, rows, [(lane >= HEAD_DIM) == bool(kv)])
        bb = jnp.concatenate([bband_ref[2 * kv], bband_ref[2 * kv + 1]], axis=0)
        bm = jnp.concatenate([bmeta_ref[mi, 2 * kv], bmeta_ref[mi, 2 * kv + 1]], axis=0)
        sb = jnp.where(dead, NEG, _dot_nt(qs, kband) + bb)
        sm = _dot_nt(qs, kmeta) + bm
        sk = jnp.concatenate([jnp.full((BLOCK, 1), sink_ref[2 * kv], F32),
                              jnp.full((BLOCK, 1), sink_ref[2 * kv + 1], F32)], axis=0)
        m = jnp.maximum(jnp.maximum(jnp.max(sb, axis=-1, keepdims=True), jnp.max(sm, axis=-1, keepdims=True)), sk)
        pb = jnp.exp2(sb - m)
        pm = jnp.exp2(sm - m)
        den = jnp.sum(pb, axis=-1, keepdims=True) + jnp.sum(pm, axis=-1, keepdims=True) + jnp.exp2(sk - m)
        outs.append((_dot(pb.astype(BF16), vband) + _dot(pm.astype(BF16), vmeta)) / den)
    for t in range(2):
        y = jnp.where(lane < HEAD_DIM, outs[0][t * BLOCK:(t + 1) * BLOCK], outs[1][t * BLOCK:(t + 1) * BLOCK])
        o_ref[0, rows, t * LANES:(t + 1) * LANES] = y.astype(BF16)


def _attn_a_kernel(sink_ref, q_ref, k_ref, v_ref, bband_ref, bmeta_ref, o_ref):
    nb = k_ref.shape[1] // BLOCK
    nsub = Q_TILE // BLOCK
    for sb in range(nsub):
        _attn_a_block(pl.program_id(1) * nsub + sb, nb, slice(sb * BLOCK, (sb + 1) * BLOCK),
                      sink_ref, q_ref, k_ref, v_ref, bband_ref, bmeta_ref, o_ref)


def _attn_a(qkv, sink, bband, bmeta):
    bn, L, _ = qkv.shape
    return pl.pallas_call(
        _attn_a_kernel,
        grid=(bn, L // Q_TILE),
        in_specs=[
            pl.BlockSpec(memory_space=pltpu.SMEM),
            pl.BlockSpec((1, Q_TILE, 2 * LANES), lambda b, i: (b, i, 0)),
            pl.BlockSpec((1, L, LANES), lambda b, i: (b, 0, 2)),
            pl.BlockSpec((1, L, LANES), lambda b, i: (b, 0, 3)),
            _const_spec((N_HEADS, BLOCK, 3 * BLOCK)),
            _const_spec((3, N_HEADS, BLOCK, BLOCK)),
        ],
        out_specs=pl.BlockSpec((1, Q_TILE, MIX_W), lambda b, i: (b, i, 0)),
        out_shape=jax.ShapeDtypeStruct((bn, L, MIX_W), BF16),
        compiler_params=_params(("parallel", "parallel")),
        name="attn_window",
    )(sink, qkv, qkv, qkv, bband, bmeta)


def _stack_maps(q_ref, masks_per_kv):
    parts = []
    for masks in masks_per_kv:
        for t in range(2):
            qt = q_ref[0, :, t * LANES:(t + 1) * LANES].astype(F32)
            for m in masks:
                parts.append(jnp.where(m, qt, 0.0).T.astype(BF16))
    return jnp.concatenate(parts, axis=1)


def _attn_sweep(qs_ref, k_ref, vt_ref, bias_fn, sa_ref, sb_ref, m_ref, acc_ref):
    nb = k_ref.shape[1] // BLOCK
    W = qs_ref.shape[1]
    half = W // 2
    ngrp = (nb - 1) // C_UNROLL
    keyrow = lax.broadcasted_iota(jnp.int32, (BLOCK, 1), 0)

    def rows(ref, j0, width):
        if isinstance(j0, int):
            return ref[0, j0 * BLOCK:(j0 + width) * BLOCK, :]
        return ref[0, pl.ds(pl.multiple_of(j0 * BLOCK, BLOCK), width * BLOCK), :]

    def cols(ref, j0, width):
        if isinstance(j0, int):
            return ref[0, :, j0 * BLOCK:(j0 + width) * BLOCK]
        return ref[0, :, pl.ds(pl.multiple_of(j0 * BLOCK, BLOCK), width * BLOCK)]

    def scores(j0, width):
        s_all = _dot(rows(k_ref, j0, width), qs_ref[...])
        parts = []
        for u in range(width):
            s = s_all[u * BLOCK:(u + 1) * BLOCK]
            b = bias_fn(j0 + u)
            if b is not None:
                s = s + b
            if isinstance(j0, int) and j0 + u == 0:
                s = jnp.where(keyrow >= FRONT, s, NEG)
            parts.append(s)
        return parts[0] if width == 1 else jnp.concatenate(parts, axis=0)

    def absorb(s, j0, width, first=False):
        mg = jnp.max(jnp.max(s.reshape(width * BLOCK // 8, 8, W), axis=0), axis=0, keepdims=True)
        m_new = mg if first else jnp.maximum(m_ref[...], mg)
        p = jnp.exp2(s - m_new).astype(BF16)
        vt = cols(vt_ref, j0, width)
        ones = jnp.ones((16, width * BLOCK), BF16)
        if not first:
            alpha = jnp.exp2(m_ref[...] - m_new)
        for kv in range(N_KV):
            w = _dot(jnp.concatenate([vt[kv * HEAD_DIM:(kv + 1) * HEAD_DIM], ones], axis=0),
                     p[:, kv * half:(kv + 1) * half])
            acc_ref[kv] = w if first else acc_ref[kv] * alpha[:, kv * half:(kv + 1) * half] + w
        m_ref[...] = m_new

    def group(n):
        return 1 + n * C_UNROLL

    absorb(scores(0, 1), 0, 1, first=True)
    if ngrp > 0:
        sa_ref[...] = scores(group(0), C_UNROLL)
        npair = (ngrp - 1) // 2

        def body(n, carry):
            g = 2 * n
            sb_ref[...] = scores(group(g + 1), C_UNROLL)
            absorb(sa_ref[...], group(g), C_UNROLL)
            sa_ref[...] = scores(group(g + 2), C_UNROLL)
            absorb(sb_ref[...], group(g + 1), C_UNROLL)
            return carry

        lax.fori_loop(0, npair, body, 0)
        g = 2 * npair
        if ngrp - g == 2:
            sb_ref[...] = scores(group(g + 1), C_UNROLL)
            absorb(sa_ref[...], group(g), C_UNROLL)
            absorb(sb_ref[...], group(g + 1), C_UNROLL)
        else:
            absorb(sa_ref[...], group(g), C_UNROLL)
    for j in range(1 + ngrp * C_UNROLL, nb):
        absorb(scores(j, 1), j, 1)


def _sweep_scratch(nmaps, qt):
    W = nmaps * qt
    return [pltpu.VMEM((LANES, W), BF16), pltpu.VMEM((C_UNROLL * BLOCK, W), F32), pltpu.VMEM((C_UNROLL * BLOCK, W), F32),
            pltpu.VMEM((1, W), F32), pltpu.VMEM((N_KV, HEAD_DIM + 16, W // 2), F32)]


def _attn_b_kernel(q_ref, k_ref, vt_ref, o_ref, qs_ref, sa_ref, sb_ref, m_ref, acc_ref):
    lane = lax.broadcasted_iota(jnp.int32, (1, LANES), 1)
    head_b = (lane // (HEAD_DIM // 2)) % 2
    qs_ref[...] = _stack_maps(q_ref, [[head_b == kv] for kv in range(N_KV)])
    _attn_sweep(qs_ref, k_ref, vt_ref, lambda j: None, sa_ref, sb_ref, m_ref, acc_ref)
    qt = q_ref.shape[1]
    ot = [acc_ref[kv][0:HEAD_DIM] / acc_ref[kv][HEAD_DIM:HEAD_DIM + 1] for kv in range(N_KV)]
    for g in range(2):
        yt = jnp.concatenate([ot[kv][:, g * qt:(g + 1) * qt] for kv in range(N_KV)], axis=0)
        o_ref[0, :, g * LANES:(g + 1) * LANES] = yt.T.astype(BF16)


def _attn_b(qkv, vt):
    bn, L, _ = qkv.shape
    return pl.pallas_call(
        _attn_b_kernel,
        grid=(bn, L // Q_TILE_ROPE),
        in_specs=[
            pl.BlockSpec((1, Q_TILE_ROPE, 2 * LANES), lambda b, i: (b, i, 2)),
            pl.BlockSpec((1, L, LANES), lambda b, i: (b, 0, 6)),
            pl.BlockSpec((1, LANES, L), lambda b, i: (b, 0, 0)),
        ],
        out_specs=pl.BlockSpec((1, Q_TILE_ROPE, MIX_W), lambda b, i: (b, i, 0)),
        out_shape=jax.ShapeDtypeStruct((bn, L, MIX_W), BF16),
        scratch_shapes=_sweep_scratch(4, Q_TILE_ROPE),
        compiler_params=_params(("parallel", "parallel")),
        name="attn_rope",
    )(qkv, qkv, vt)


def _attn_c_kernel(sc_ref, q_ref, k_ref, vt_ref, bias_ref, gsub_ref, hsel_ref, o_ref,
                   qs_ref, sa_ref, sb_ref, m_ref, acc_ref):
    i = pl.program_id(1)
    lam = sc_ref[0]
    lane = lax.broadcasted_iota(jnp.int32, (1, LANES), 1)
    grp = lane // C_DIM
    qs_ref[...] = _stack_maps(q_ref, [[grp == 2 * kv, grp == 2 * kv + 1] for kv in range(N_KV)])

    def bias(j):
        nsub = Q_TILE // BLOCK
        ts = [jnp.clip(j - (i * nsub + sb) + 2, 0, 4) for sb in range(nsub)]
        per_head = [jnp.concatenate([bias_ref[t, h] for t in ts], axis=1) for h in range(N_HEADS)]
        return jnp.concatenate([per_head[h] for h in range(N_HEADS) for _ in range(2)], axis=1)

    _attn_sweep(qs_ref, k_ref, vt_ref, bias, sa_ref, sb_ref, m_ref, acc_ref)
    ot = [acc_ref[kv][0:HEAD_DIM] / acc_ref[kv][HEAD_DIM:HEAD_DIM + 1] for kv in range(N_KV)]
    for g in range(2):
        dt = [ot[kv][:, (2 * g) * Q_TILE:(2 * g + 1) * Q_TILE] - lam * ot[kv][:, (2 * g + 1) * Q_TILE:(2 * g + 2) * Q_TILE]
              for kv in range(N_KV)]
        y = jnp.concatenate(dt, axis=0).T
        ms = _split_dot(y * y, hsel_ref[...]) * (1.0 / HEAD_DIM)
        y = y * lax.rsqrt(ms + RMS_EPS) * gsub_ref[...] * sc_ref[1]
        o_ref[0, :, g * LANES:(g + 1) * LANES] = y.astype(BF16)


def _attn_c(qkv, vt, scal, bias5t, gsub, hsel):
    bn, L, _ = qkv.shape
    return pl.pallas_call(
        _attn_c_kernel,
        grid=(bn, L // Q_TILE),
        in_specs=[
            pl.BlockSpec(memory_space=pltpu.SMEM),
            pl.BlockSpec((1, Q_TILE, 2 * LANES), lambda b, i: (b, i, 4)),
            pl.BlockSpec((1, L, LANES), lambda b, i: (b, 0, 10)),
            pl.BlockSpec((1, LANES, L), lambda b, i: (b, 0, 0)),
            _const_spec((5, N_HEADS, BLOCK, BLOCK)),
            _const_spec((1, LANES)),
            _const_spec((LANES, LANES)),
        ],
        out_specs=pl.BlockSpec((1, Q_TILE, MIX_W), lambda b, i: (b, i, 0)),
        out_shape=jax.ShapeDtypeStruct((bn, L, MIX_W), BF16),
        scratch_shapes=_sweep_scratch(8, Q_TILE),
        compiler_params=_params(("parallel", "parallel")),
        name="attn_diff",
    )(scal, qkv, qkv, vt, bias5t, gsub, hsel)


def _split3_dot(sel, x):
    x1 = x.astype(BF16)
    r1 = x - x1.astype(F32)
    x2 = r1.astype(BF16)
    x3 = (r1 - x2.astype(F32)).astype(BF16)
    return _dot(sel, x1) + _dot(sel, x2) + _dot(sel, x3)


def _hgrn_chunk(forward, q, kk, g, v, tri_ref, bd_ref, st_ref):
    W = MIX_W
    b = _split3_dot(tri_ref[...], g)
    tot = b[BLOCK - 1:BLOCK, :]
    if forward:
        x, e = b, tot - b
    else:
        x, e = tot - b + g, b - g
    sub = lax.broadcasted_iota(jnp.int32, (1, 8, 1), 1)
    x3, q3, k3, v3 = (a.reshape(BLOCK // 8, 8, W) for a in (x, q, kk, v))
    o = _dot((q * kk).astype(BF16), bd_ref[...]) * v
    for d in range(1, 8):
        sh = d if forward else 8 - d
        ok = (sub >= d) if forward else (sub + d < 8)
        e_d = jnp.exp2(jnp.where(ok, x3 - pltpu.roll(x3, sh, 1), NEG)) * q3 * pltpu.roll(k3, sh, 1)
        o = o + _dot(e_d.reshape(BLOCK, W).astype(BF16), bd_ref[...]) * pltpu.roll(v3, sh, 1).reshape(BLOCK, W)
    row = lax.broadcasted_iota(jnp.int32, (BLOCK, 1), 0)
    qcol = lax.broadcasted_iota(jnp.int32, (1, N_HEADS * BLOCK), 1) % BLOCK
    lane_head = lax.broadcasted_iota(jnp.int32, (1, W), 1) // HEAD_DIM
    at = jnp.zeros((BLOCK, N_HEADS * BLOCK), F32)
    for m in (8, 16, 32, 64):
        mid = m - 1 if forward else m
        xm = jnp.concatenate([jnp.broadcast_to(x[i + mid:i + mid + 1, :], (2 * m, W))
                              for i in range(0, BLOCK, 2 * m)], axis=0)
        second = (row // m) % 2 == 1
        qside = second if forward else jnp.logical_not(second)
        qt = (q * jnp.exp2(jnp.where(qside, x - xm, NEG))).astype(BF16)
        kt = (kk * jnp.exp2(jnp.where(qside, NEG, xm - x))).astype(BF16)
        qstack = jnp.concatenate([jnp.where(lane_head == h, qt, jnp.zeros_like(qt)) for h in range(N_HEADS)], axis=0)
        a = _dot_nt(kt, qstack)
        at = at + jnp.where(row // (2 * m) == qcol // (2 * m), a, 0.0)
    vt = v.T.astype(BF16)
    ot = _dot(vt, at.astype(BF16))
    o = o + jnp.concatenate([ot[h * HEAD_DIM:(h + 1) * HEAD_DIM, h * BLOCK:(h + 1) * BLOCK]
                             for h in range(N_HEADS)], axis=0).T
    qd = (q * jnp.exp2(x)).astype(BF16)
    kd = (kk * jnp.exp2(e)).astype(BF16)
    gam = jnp.exp2(tot)
    hmask = (lax.broadcasted_iota(jnp.int32, (LANES, LANES), 0) // HEAD_DIM
             == lax.broadcasted_iota(jnp.int32, (LANES, LANES), 1) // HEAD_DIM)
    inter = []
    for hf in range(2):
        lanes = slice(hf * LANES, (hf + 1) * LANES)
        st = st_ref[hf]
        inter.append(_dot_nt(qd[:, lanes], st.astype(BF16)))
        st_ref[hf] = st * gam[:, lanes] + jnp.where(hmask, _dot(vt[lanes, :], kd[:, lanes]), 0.0)
    return o + jnp.concatenate(inter, axis=1)


def _hgrn_direction(forward, first_tile, q_ref, z_ref, v_ref, lb_ref, tri_ref, bd_ref, o_ref, st_ref):
    R = q_ref.shape[1]
    z = z_ref[0]
    q = q_ref[0] * (HEAD_DIM ** -0.5)
    v = v_ref[0]
    lb = lb_ref[...]
    row = lax.broadcasted_iota(jnp.int32, (R, 1), 0)
    sig = jax.nn.sigmoid(z)
    f = lb + (1.0 - lb) * sig
    g = jnp.log(jnp.maximum(f, F_FLOOR)) * LOG2E
    kk = (1.0 - lb) * (1.0 - sig)
    kk = jnp.where(row < jnp.where(first_tile, FRONT, 0), 0.0, kk)
    nblk = R // BLOCK
    for n in range(nblk):
        r0 = (n if forward else nblk - 1 - n) * BLOCK
        rows = slice(r0, r0 + BLOCK)
        o_ref[0, rows, :] = _hgrn_chunk(forward, q[rows], kk[rows], g[rows], v[rows], tri_ref, bd_ref, st_ref)


def _hgrn_kernel(qf_ref, zf_ref, vf_ref, qb_ref, zb_ref, vb_ref, lb_ref, tri_ref, bd_ref, of_ref, ob_ref, st_ref):
    j = pl.program_id(1)
    nt = pl.num_programs(1)

    @pl.when(j == 0)
    def _():
        st_ref[...] = jnp.zeros_like(st_ref)

    _hgrn_direction(True, j == 0, qf_ref, zf_ref, vf_ref, lb_ref.at[0], tri_ref, bd_ref, of_ref, st_ref.at[0])
    _hgrn_direction(False, j == nt - 1, qb_ref, zb_ref, vb_ref, lb_ref.at[1], tri_ref, bd_ref, ob_ref, st_ref.at[1])


def _hgrn(dproj, lb2, tri, bd):
    bn, L, _ = dproj.shape
    nt = L // ROW_TILE
    fw = lambda c: pl.BlockSpec((1, ROW_TILE, MIX_W), lambda b, j: (b, j, c))
    bw = lambda c: pl.BlockSpec((1, ROW_TILE, MIX_W), lambda b, j: (b, nt - 1 - j, c))
    return pl.pallas_call(
        _hgrn_kernel,
        grid=(bn, nt),
        in_specs=[fw(0), fw(1), fw(3), bw(0), bw(2), bw(3), _const_spec((2, 1, MIX_W)), _const_spec((BLOCK, BLOCK)),
                  _const_spec((MIX_W, MIX_W))],
        out_specs=[pl.BlockSpec((1, ROW_TILE, MIX_W), lambda b, j: (b, j, 0)),
                   pl.BlockSpec((1, ROW_TILE, MIX_W), lambda b, j: (b, nt - 1 - j, 0))],
        out_shape=[jax.ShapeDtypeStruct((bn, L, MIX_W), F32)] * 2,
        scratch_shapes=[pltpu.VMEM((2, 2, LANES, LANES), F32)],
        compiler_params=_params(("parallel", "arbitrary")),
        name="hgrn",
    )(dproj, dproj, dproj, dproj, dproj, dproj, lb2, tri, bd)


def _merge_kernel(h_ref, ya_ref, yb_ref, yc_ref, of_ref, ob_ref, dg_ref, g_ref, gout_ref, bd_ref,
                  wgz_ref, wb_ref, wo_ref, o_ref):
    h = h_ref[0]
    u = _rms(h, g_ref[...]).astype(BF16)
    od = of_ref[0] + ob_ref[0]
    ms = _split_dot(od * od, bd_ref[...]) * (1.0 / HEAD_DIM)
    yd = (od * lax.rsqrt(ms + RMS_EPS) * gout_ref[...] * jax.nn.silu(dg_ref[0])).astype(BF16)
    ys = (ya_ref[0], yb_ref[0], yc_ref[0], yd)
    merged = None
    for n in range(N_BRANCH):
        gate = jax.nn.sigmoid(_dot(u, wgz_ref[:, n * D_MODEL:(n + 1) * D_MODEL]))
        term = gate * _dot(ys[n], wb_ref[n])
        merged = term if merged is None else merged + term
    o_ref[0] = h + _dot(merged.astype(BF16), wo_ref[...])


def _merge(h, ya, yb, yc, of, ob, dproj, g, gout, bd, wgz, wb, wo):
    bn, L, _ = h.shape
    rows = bn * L
    tm = _dense_tile(rows)
    flat = lambda a: a.reshape(1, rows, a.shape[-1])
    h, ya, yb, yc, of, ob, dproj = (flat(a) for a in (h, ya, yb, yc, of, ob, dproj))
    row = lambda w, c=0: pl.BlockSpec((1, tm, w), lambda i: (0, i, c))
    out = pl.pallas_call(
        _merge_kernel,
        grid=(rows // tm,),
        in_specs=[row(D_MODEL), row(MIX_W), row(MIX_W), row(MIX_W), row(MIX_W), row(MIX_W), row(MIX_W, 4),
                  _const_spec((1, D_MODEL)), _const_spec((1, MIX_W)), _const_spec((MIX_W, MIX_W)),
                  _const_spec((D_MODEL, N_BRANCH * D_MODEL)), _const_spec((N_BRANCH, MIX_W, D_MODEL)),
                  _const_spec((D_MODEL, D_MODEL))],
        out_specs=row(D_MODEL),
        out_shape=jax.ShapeDtypeStruct((1, rows, D_MODEL), F32),
        compiler_params=_params(("parallel",)),
        name="merge",
    )(h, ya, yb, yc, of, ob, dproj, g, gout, bd, wgz, wb, wo)
    return out.reshape(bn, L, D_MODEL)


def _ffn_kernel(h_ref, g_ref, wg_ref, wu_ref, wd_ref, gfin_ref, o_ref, *, final):
    h = h_ref[0]
    u = _rms(h, g_ref[...]).astype(BF16)
    half = D_FF // 2
    out = h
    for c in range(2):
        a = _dot(u, wg_ref[:, c * half:(c + 1) * half])
        t = (jax.nn.silu(a) * _dot(u, wu_ref[:, c * half:(c + 1) * half])).astype(BF16)
        out = out + _dot(t, wd_ref[c * half:(c + 1) * half, :])
    o_ref[0] = _rms(out, gfin_ref[...]) if final else out


def _ffn(h, g, wg, wu, wd, gfin, final):
    bn, L, _ = h.shape
    weights = [_const_spec((1, D_MODEL)), _const_spec((D_MODEL, D_FF)), _const_spec((D_MODEL, D_FF)),
               _const_spec((D_FF, D_MODEL)), _const_spec((1, D_MODEL))]
    kern = functools.partial(_ffn_kernel, final=final)
    if final:
        S = L - META_END
        tm = _dense_tile(S)
        return pl.pallas_call(
            kern,
            grid=(bn, S // tm),
            in_specs=[pl.BlockSpec((pl.Element(1), pl.Element(tm), pl.Element(D_MODEL)),
                                   lambda b, i: (b, pl.multiple_of(META_END + i * tm, BLOCK), 0))] + weights,
            out_specs=pl.BlockSpec((1, tm, D_MODEL), lambda b, i: (b, i, 0)),
            out_shape=jax.ShapeDtypeStruct((bn, S, D_MODEL), F32),
            compiler_params=_params(("parallel", "parallel")),
            name="ffn_out",
        )(h, g, wg, wu, wd, gfin)
    rows = bn * L
    tm = _dense_tile(rows)
    row = pl.BlockSpec((1, tm, D_MODEL), lambda i: (0, i, 0))
    out = pl.pallas_call(
        kern,
        grid=(rows // tm,),
        in_specs=[row] + weights,
        out_specs=row,
        out_shape=jax.ShapeDtypeStruct((1, rows, D_MODEL), F32),
        compiler_params=_params(("parallel",)),
        name="ffn",
    )(h.reshape(1, rows, D_MODEL), g, wg, wu, wd, gfin)
    return out.reshape(bn, L, D_MODEL)


def _t5_bucket(rel):
    half = N_BUCKETS // 2
    exact = half // 2
    n = jnp.abs(rel)
    nf = jnp.maximum(n, exact).astype(F32)
    big = exact + (jnp.log(nf / exact) / math.log(MAX_DIST / exact) * (half - exact)).astype(jnp.int32)
    big = jnp.minimum(big, half - 1)
    return jnp.where(rel > 0, half, 0) + jnp.where(n < exact, n, big)


def _lookup(table, bucket):
    bucket = jnp.bitwise_and(bucket, N_BUCKETS - 1)
    hit = bucket[None, ..., None] == jnp.arange(N_BUCKETS)
    return jnp.sum(jnp.where(hit, table.T.reshape((N_HEADS,) + (1,) * bucket.ndim + (N_BUCKETS,)), 0.0), axis=-1)


def _bias_tables(rel_bias):
    bias_a = rel_bias[:, :N_HEADS].astype(F32) * LOG2E
    bias_c = rel_bias[:, N_HEADS:].astype(F32) * LOG2E
    r = jnp.arange(BLOCK)[:, None]
    rel_band = (jnp.arange(3 * BLOCK)[None, :] - BLOCK) - r
    bband = jnp.where(jnp.abs(rel_band)[None] <= WINDOW, _lookup(bias_a, _t5_bucket(rel_band)), NEG)
    lane = jnp.arange(BLOCK)[None, :]
    rel_m = lane[None] - (jnp.arange(3)[:, None, None] * BLOCK + r[None])
    bmeta = jnp.where((lane >= FRONT)[None, None], jnp.moveaxis(_lookup(bias_a, _t5_bucket(rel_m)), 0, 1), NEG)
    rel5t = (jnp.arange(5)[:, None, None] - 2) * BLOCK + r[None] - lane[None]
    bias5t = jnp.moveaxis(_lookup(bias_c, _t5_bucket(rel5t)), 0, 1)
    return bband, bmeta, bias5t


def _rope_layout():
    return np.concatenate([np.zeros(32, int), np.ones(32, int)] * 2)


def _rope_cols(w, heads):
    lead = w.shape[:-1]
    w = w.reshape(lead + (heads // 2, 2, 2, 2, HEAD_DIM // 4))
    if heads == N_HEADS:
        w = jnp.moveaxis(w, (-5, -4, -3, -2), (-3, -5, -2, -4))
    else:
        w = jnp.moveaxis(w, (-4, -3, -2), (-3, -2, -4))
    return w.reshape(lead + (heads * HEAD_DIM,))


def _pair_cols(w):
    lead = w.shape[:-1]
    return jnp.swapaxes(w.reshape(lead + (N_KV, 2, HEAD_DIM)), -3, -2).reshape(lead + (N_HEADS * HEAD_DIM,))


def _rope_tables(L, S):
    rows = S // GRID_W
    row = jnp.concatenate([jnp.zeros((FRONT,), jnp.int32), -jnp.ones((N_META,), jnp.int32),
                           jnp.repeat(jnp.arange(rows, dtype=jnp.int32), GRID_W)]).astype(F32)
    col = jnp.concatenate([jnp.zeros((FRONT,), jnp.int32), jnp.arange(N_META, dtype=jnp.int32),
                           jnp.tile(jnp.arange(GRID_W, dtype=jnp.int32), rows)]).astype(F32)
    half = HEAD_DIM // 2
    inv = ROPE_THETA ** (-jnp.arange(0, half, 2, dtype=F32) / half)
    ang = jnp.concatenate([row[:, None] * inv[None, :], col[:, None] * inv[None, :]], axis=-1)
    cos_t = jnp.tile(jnp.cos(ang), (1, 4))
    sin_t = jnp.concatenate([-jnp.tile(jnp.sin(ang), (1, 2)), jnp.tile(jnp.sin(ang), (1, 2))], axis=-1)
    return cos_t, sin_t


def _rope_gain(g):
    g = jnp.swapaxes(g.astype(F32).reshape(2, 2, HEAD_DIM // 4), 0, 1).reshape(2, 1, HEAD_DIM // 2)
    return jnp.broadcast_to(g, (2, 2, HEAD_DIM // 2)).reshape(LANES)


def _qkv_weight(wl):
    a, b, c = wl[:, 0:512], wl[:, 512:1024], wl[:, 1024:1536]
    return jnp.concatenate([_pair_cols(a[:, :256]), a[:, 256:],
                            _rope_cols(b[:, :256], N_HEADS), _rope_cols(b[:, 256:384], N_KV), b[:, 384:],
                            _pair_cols(c[:, :256]), c[:, 256:]], axis=1)


def kernel(x, meta_tokens, rel_bias, hgrn_lb_logits, ln_mix, w_in, attn_sink, qk_norm_q, qk_norm_k, diff_lambda,
           diff_subnorm, hgrn_out_norm, w_branch, w_out, ln_ffn, w_ffn_gate, w_ffn_up, w_ffn_down, ln_final):
    bn, S, _ = x.shape
    L = META_END + S
    depth = w_in.shape[0]
    assert L % ROW_TILE == 0 and L % Q_TILE == 0 and L % Q_TILE_ROPE == 0 and S % GRID_W == 0
    padmeta = jnp.concatenate([jnp.zeros((FRONT, D_MODEL), x.dtype), meta_tokens.astype(x.dtype)], axis=0)
    h = x
    bband, bmeta, bias5t = _bias_tables(rel_bias)
    cos_t, sin_t = _rope_tables(L, S)
    rhead = _rope_layout()
    lane = np.arange(LANES)
    hsel_rope = jnp.asarray(rhead[:, None] == rhead[None, :], BF16)
    hsel_half = jnp.asarray((lane[:, None] // HEAD_DIM) == (lane[None, :] // HEAD_DIM), BF16)
    w256 = np.arange(MIX_W)
    bd = jnp.asarray((w256[:, None] // HEAD_DIM) == (w256[None, :] // HEAD_DIM), BF16)
    tri = jnp.asarray(np.tril(np.ones((BLOCK, BLOCK))), BF16)
    cs = np.ones((1, QKV_W), np.float32)
    cs[0, 0:256] = HEAD_DIM ** -0.5 * LOG2E
    cs[0, 1024:1280] = C_DIM ** -0.5 * LOG2E
    cs = jnp.asarray(cs)
    lb_p = jax.nn.softmax(hgrn_lb_logits.astype(F32), axis=1)
    lb_all = jnp.cumsum(lb_p, axis=1) - lb_p[:, :1]

    for l in range(depth):
        wl = w_in[l]
        wqkv = _qkv_weight(wl).astype(BF16)
        wd = wl[:, QKV_W:QKV_W + D_W].astype(BF16)
        wgz = wl[:, QKV_W + D_W:].astype(BF16)
        gn = jnp.stack([_rope_gain(qk_norm_q[l]) * (HEAD_DIM ** -0.5 * LOG2E)] * 2 + [_rope_gain(qk_norm_k[l])])
        if l == 0:
            qkv, dproj, vtb, vtc, h = _in_proj(x, ln_mix[l][None], wqkv, wd, cs, cos_t, sin_t, gn, hsel_rope, padmeta)
        else:
            qkv, dproj, vtb, vtc = _in_proj(h, ln_mix[l][None], wqkv, wd, cs, cos_t, sin_t, gn, hsel_rope)
        ya = _attn_a(qkv, attn_sink[l].astype(F32) * LOG2E, bband, bmeta)
        yb = _attn_b(qkv, vtb)
        lam_init = 0.8 - 0.6 * math.exp(-0.3 * l)
        lam_p = diff_lambda[l].astype(F32)
        lam = jnp.exp(jnp.sum(lam_p[0] * lam_p[1])) - jnp.exp(jnp.sum(lam_p[2] * lam_p[3])) + lam_init
        scal = jnp.stack([lam, jnp.asarray(1.0 - lam_init, F32)])
        gsub = jnp.tile(diff_subnorm[l].astype(F32), 2)[None]
        yc = _attn_c(qkv, vtc, scal, bias5t, gsub, hsel_half)
        lb2 = lb_all[:, l][:, None, :]
        of, ob = _hgrn(dproj, lb2, tri, bd)
        wb = w_branch[l].astype(BF16)
        wb_att = jnp.swapaxes(wb[:3].reshape(3, N_KV, 2, HEAD_DIM, D_MODEL), 1, 2).reshape(3, MIX_W, D_MODEL)
        wb = jnp.concatenate([wb_att, wb[3:]], axis=0)
        gout = jnp.tile(hgrn_out_norm[l].astype(F32), N_HEADS)[None]
        h = _merge(h, ya, yb, yc, of, ob, dproj, ln_mix[l][None], gout, bd, wgz, wb, w_out[l].astype(BF16))
        h = _ffn(h, ln_ffn[l][None], w_ffn_gate[l].astype(BF16), w_ffn_up[l].astype(BF16), w_ffn_down[l].astype(BF16),
                 ln_final[None], final=l == depth - 1)
    return h
```

```python
import functools
import math

import jax
import jax.numpy as jnp
import numpy as np
from jax import lax
from jax.experimental import pallas as pl
from jax.experimental.pallas import tpu as pltpu

D_MODEL = 1024
N_BRANCH = 4
MIX_W = D_MODEL // N_BRANCH
HEAD_DIM = 64
BLOCK = 128
WINDOW = 128
ROPE_THETA = 10000.0
GRID_W = 64
N_HEADS = 4
N_KV = 2
C_DIM = 32
N_META = 16
FRONT = (-N_META) % BLOCK
META_END = FRONT + N_META
N_BUCKETS = 32
MAX_DIST = 128
D_FF = -(-(8 * D_MODEL) // (3 * 256)) * 256
RMS_EPS = 1e-6
NEG = -1e30
F_FLOOR = 1e-30
LOG2E = 1.4426950408889634
LANES = 128
ROW_TILE = 384
DENSE_TILES = (768, 512, 384, 128)
C_UNROLL = 4
Q_TILE = 384
Q_TILE_WIDE = 1408
QKV_W = 3 * 512
D_W = 5 * MIX_W
VMEM_LIMIT = 56 * 1024 * 1024

F32 = jnp.float32
BF16 = jnp.bfloat16


def _dot(a, b):
    return jnp.dot(a, b, preferred_element_type=F32)


def _dot_nt(a, b):
    return lax.dot_general(a, b, (((1,), (1,)), ((), ())), preferred_element_type=F32)


def _split_dot(x, sel):
    hi = x.astype(BF16)
    lo = (x - hi.astype(F32)).astype(BF16)
    return _dot(hi, sel) + _dot(lo, sel)


def _rms(x, g):
    return x * lax.rsqrt(jnp.mean(x * x, axis=-1, keepdims=True) + RMS_EPS) * g


def _params(sem):
    return pltpu.CompilerParams(dimension_semantics=sem, vmem_limit_bytes=VMEM_LIMIT)


def _const_spec(shape):
    nd = len(shape)
    return pl.BlockSpec(shape, lambda *_: (0,) * nd, pipeline_mode=pl.Buffered(1))


def _dense_tile(rows):
    return next(t for t in DENSE_TILES if rows % t == 0)


def _in_proj_kernel(*refs, embed):
    if embed:
        x_ref, pm_ref, refs, h_out_ref = refs[0], refs[1], refs[2:-1], refs[-1]
        xt = x_ref[0]
        first = jnp.concatenate([pm_ref[...], xt[:ROW_TILE - META_END]], axis=0)
        h = jnp.where(pl.program_id(1) == 0, first, xt)
        h_out_ref[0] = h
    else:
        h_ref, refs = refs[0], refs[1:]
        h = h_ref[0]
    g_ref, wqkv_ref, wd_ref, cs_ref, cos_ref, sin_ref, gn_ref, hsel_ref, qkv_ref, d_ref, vtb_ref, vtc_ref = refs
    u = _rms(h, g_ref[...]).astype(BF16)
    acc = _dot(u, wqkv_ref[...]) * cs_ref[...]
    qkv_ref[0, :, 0:512] = acc[:, 0:512].astype(BF16)
    for t in range(3):
        c0 = 512 + LANES * t
        xt = acc[:, c0:c0 + LANES]
        ms = _split_dot(xt * xt, hsel_ref[...]) * (1.0 / HEAD_DIM)
        xt = xt * lax.rsqrt(ms + RMS_EPS) * gn_ref[t:t + 1, :]
        xt = xt * cos_ref[...] + pltpu.roll(xt, LANES // 2, 1) * sin_ref[...]
        qkv_ref[0, :, c0:c0 + LANES] = xt.astype(BF16)
    qkv_ref[0, :, 896:QKV_W] = acc[:, 896:QKV_W].astype(BF16)
    vtb_ref[0] = acc[:, 896:1024].T.astype(BF16)
    vtc_ref[0] = acc[:, 1408:QKV_W].T.astype(BF16)
    d_ref[0] = _dot(u, wd_ref[...])


def _in_proj(h, g, wqkv, wd, cs, cos_t, sin_t, gn, hsel, padmeta=None):
    embed = padmeta is not None
    bn = h.shape[0]
    L = h.shape[1] + (META_END if embed else 0)
    nt = L // ROW_TILE
    row = lambda w: pl.BlockSpec((1, ROW_TILE, w), lambda b, i: (b, i, 0))
    vt = pl.BlockSpec((1, LANES, ROW_TILE), lambda b, i: (b, 0, i))
    if embed:
        src = [pl.BlockSpec((pl.Element(1), pl.Element(ROW_TILE), pl.Element(D_MODEL)),
                            lambda b, i: (b, pl.multiple_of(jnp.maximum(i * ROW_TILE - META_END, 0), BLOCK), 0)),
               _const_spec((META_END, D_MODEL))]
        args = (h, padmeta)
    else:
        src, args = [row(D_MODEL)], (h,)
    out_specs = [row(QKV_W), row(D_W), vt, vt] + ([row(D_MODEL)] if embed else [])
    out_shape = [jax.ShapeDtypeStruct((bn, L, QKV_W), BF16), jax.ShapeDtypeStruct((bn, L, D_W), F32),
                 jax.ShapeDtypeStruct((bn, LANES, L), BF16), jax.ShapeDtypeStruct((bn, LANES, L), BF16)]
    return pl.pallas_call(
        functools.partial(_in_proj_kernel, embed=embed),
        grid=(bn, nt),
        in_specs=src + [
            _const_spec((1, D_MODEL)),
            _const_spec((D_MODEL, QKV_W)),
            _const_spec((D_MODEL, D_W)),
            _const_spec((1, QKV_W)),
            pl.BlockSpec((ROW_TILE, LANES), lambda b, i: (i, 0)),
            pl.BlockSpec((ROW_TILE, LANES), lambda b, i: (i, 0)),
            _const_spec((3, LANES)),
            _const_spec((LANES, LANES)),
        ],
        out_specs=out_specs,
        out_shape=out_shape + ([jax.ShapeDtypeStruct((bn, L, D_MODEL), F32)] if embed else []),
        compiler_params=_params(("parallel", "parallel")),
        name="in_proj",
    )(*args, g, wqkv, wd, cs, cos_t, sin_t, gn, hsel)


def _stack_q(q_ref, rows, masks):
    parts = []
    for t in range(2):
        qt = q_ref[0, rows, t * LANES:(t + 1) * LANES]
        for m in masks:
            parts.append(jnp.where(m, qt, jnp.zeros_like(qt)))
    return jnp.concatenate(parts, axis=0)


def _blk(ref, j):
    if isinstance(j, int):
        return ref[0, j * BLOCK:(j + 1) * BLOCK, :]
    return ref[0, pl.ds(pl.multiple_of(j * BLOCK, BLOCK), BLOCK), :]


def _attn_a_block(i, nb, rows, sink_ref, q_ref, k_ref, v_ref, bband_ref, bmeta_ref, o_ref):
    prev = jnp.maximum(i - 1, 0)
    nxt = jnp.minimum(i + 1, nb - 1)
    kband = jnp.concatenate([_blk(k_ref, prev), _blk(k_ref, i), _blk(k_ref, nxt)], axis=0)
    vband = jnp.concatenate([_blk(v_ref, prev), _blk(v_ref, i), _blk(v_ref, nxt)], axis=0)
    kmeta = k_ref[0, 0:BLOCK, :]
    vmeta = v_ref[0, 0:BLOCK, :]
    col = lax.broadcasted_iota(jnp.int32, (1, 3 * BLOCK), 1)
    lo = jnp.where(i == 0, 2 * BLOCK, jnp.where(i == 1, BLOCK, 0))
    hi = jnp.where(i == nb - 1, 2 * BLOCK, 3 * BLOCK)
    dead = (col < lo) | (col >= hi)
    lane = lax.broadcasted_iota(jnp.int32, (1, LANES), 1)
    mi = jnp.minimum(i, 2)
    outs = []
    for kv in range(N_KV):
        qs = _stack_q(q_ref, rows, [(lane >= HEAD_DIM) == bool(kv)])
        bb = jnp.concatenate([bband_ref[2 * kv], bband_ref[2 * kv + 1]], axis=0)
        bm = jnp.concatenate([bmeta_ref[mi, 2 * kv], bmeta_ref[mi, 2 * kv + 1]], axis=0)
        sb = jnp.where(dead, NEG, _dot_nt(qs, kband) + bb)
        sm = _dot_nt(qs, kmeta) + bm
        sk = jnp.concatenate([jnp.full((BLOCK, 1), sink_ref[2 * kv], F32),
                              jnp.full((BLOCK, 1), sink_ref[2 * kv + 1], F32)], axis=0)
        m = jnp.maximum(jnp.maximum(jnp.max(sb, axis=-1, keepdims=True), jnp.max(sm, axis=-1, keepdims=True)), sk)
        pb = jnp.exp2(sb - m)
        pm = jnp.exp2(sm - m)
        den = jnp.sum(pb, axis=-1, keepdims=True) + jnp.sum(pm, axis=-1, keepdims=True) + jnp.exp2(sk - m)
        outs.append((_dot(pb.astype(BF16), vband) + _dot(pm.astype(BF16), vmeta)) / den)
    for t in range(2):
        y = jnp.where(lane < HEAD_DIM, outs[0][t * BLOCK:(t + 1) * BLOCK], outs[1][t * BLOCK:(t + 1) * BLOCK])
        o_ref[0, rows, t * LANES:(t + 1) * LANES] = y.astype(BF16)


def _attn_a_kernel(sink_ref, q_ref, k_ref, v_ref, bband_ref, bmeta_ref, o_ref):
    nb = k_ref.shape[1] // BLOCK
    nsub = q_ref.shape[1] // BLOCK
    for sb in range(nsub):
        _attn_a_block(pl.program_id(1) * nsub + sb, nb, slice(sb * BLOCK, (sb + 1) * BLOCK),
                      sink_ref, q_ref, k_ref, v_ref, bband_ref, bmeta_ref, o_ref)


def _attn_a(qkv, sink, bband, bmeta):
    bn, L, _ = qkv.shape
    return pl.pallas_call(
        _attn_a_kernel,
        grid=(bn, L // Q_TILE_WIDE),
        in_specs=[
            pl.BlockSpec(memory_space=pltpu.SMEM),
            pl.BlockSpec((1, Q_TILE_WIDE, 2 * LANES), lambda b, i: (b, i, 0)),
            pl.BlockSpec((1, L, LANES), lambda b, i: (b, 0, 2)),
            pl.BlockSpec((1, L, LANES), lambda b, i: (b, 0, 3)),
            _const_spec((N_HEADS, BLOCK, 3 * BLOCK)),
            _const_spec((3, N_HEADS, BLOCK, BLOCK)),
        ],
        out_specs=pl.BlockSpec((1, Q_TILE_WIDE, MIX_W), lambda b, i: (b, i, 0)),
        out_shape=jax.ShapeDtypeStruct((bn, L, MIX_W), BF16),
        compiler_params=_params(("parallel", "parallel")),
        name="attn_window",
    )(sink, qkv, qkv, qkv, bband, bmeta)


def _stack_maps(q_ref, masks_per_kv):
    parts = []
    for masks in masks_per_kv:
        for t in range(2):
            qt = q_ref[0, :, t * LANES:(t + 1) * LANES].astype(F32)
            for m in masks:
                parts.append(jnp.where(m, qt, 0.0).T.astype(BF16))
    return jnp.concatenate(parts, axis=1)


def _attn_sweep(qs_ref, k_ref, vt_ref, bias_fn, sa_ref, sb_ref, m_ref, acc_ref):
    nb = k_ref.shape[1] // BLOCK
    W = qs_ref.shape[1]
    half = W // 2
    ngrp = (nb - 1) // C_UNROLL
    keyrow = lax.broadcasted_iota(jnp.int32, (BLOCK, 1), 0)

    def rows(ref, j0, width):
        if isinstance(j0, int):
            return ref[0, j0 * BLOCK:(j0 + width) * BLOCK, :]
        return ref[0, pl.ds(pl.multiple_of(j0 * BLOCK, BLOCK), width * BLOCK), :]

    def cols(ref, j0, width):
        if isinstance(j0, int):
            return ref[0, :, j0 * BLOCK:(j0 + width) * BLOCK]
        return ref[0, :, pl.ds(pl.multiple_of(j0 * BLOCK, BLOCK), width * BLOCK)]

    def scores(j0, width):
        s_all = _dot(rows(k_ref, j0, width), qs_ref[...])
        parts = []
        for u in range(width):
            s = s_all[u * BLOCK:(u + 1) * BLOCK]
            b = bias_fn(j0 + u)
            if b is not None:
                s = s + b
            if isinstance(j0, int) and j0 + u == 0:
                s = jnp.where(keyrow >= FRONT, s, NEG)
            parts.append(s)
        return parts[0] if width == 1 else jnp.concatenate(parts, axis=0)

    def absorb(s, j0, width, first=False):
        mg = jnp.max(jnp.max(s.reshape(width * BLOCK // 8, 8, W), axis=0), axis=0, keepdims=True)
        m_new = mg if first else jnp.maximum(m_ref[...], mg)
        p = jnp.exp2(s - m_new).astype(BF16)
        vt = cols(vt_ref, j0, width)
        ones = jnp.ones((16, width * BLOCK), BF16)
        if not first:
            alpha = jnp.exp2(m_ref[...] - m_new)
        for kv in range(N_KV):
            w = _dot(jnp.concatenate([vt[kv * HEAD_DIM:(kv + 1) * HEAD_DIM], ones], axis=0),
                     p[:, kv * half:(kv + 1) * half])
            acc_ref[kv] = w if first else acc_ref[kv] * alpha[:, kv * half:(kv + 1) * half] + w
        m_ref[...] = m_new

    def group(n):
        return 1 + n * C_UNROLL

    absorb(scores(0, 1), 0, 1, first=True)
    if ngrp > 0:
        sa_ref[...] = scores(group(0), C_UNROLL)
        npair = (ngrp - 1) // 2

        def body(n, carry):
            g = 2 * n
            sb_ref[...] = scores(group(g + 1), C_UNROLL)
            absorb(sa_ref[...], group(g), C_UNROLL)
            sa_ref[...] = scores(group(g + 2), C_UNROLL)
            absorb(sb_ref[...], group(g + 1), C_UNROLL)
            return carry

        lax.fori_loop(0, npair, body, 0)
        g = 2 * npair
        if ngrp - g == 2:
            sb_ref[...] = scores(group(g + 1), C_UNROLL)
            absorb(sa_ref[...], group(g), C_UNROLL)
            absorb(sb_ref[...], group(g + 1), C_UNROLL)
        else:
            absorb(sa_ref[...], group(g), C_UNROLL)
    for j in range(1 + ngrp * C_UNROLL, nb):
        absorb(scores(j, 1), j, 1)


def _sweep_scratch(nmaps, qt):
    W = nmaps * qt
    return [pltpu.VMEM((LANES, W), BF16), pltpu.VMEM((C_UNROLL * BLOCK, W), F32), pltpu.VMEM((C_UNROLL * BLOCK, W), F32),
            pltpu.VMEM((1, W), F32), pltpu.VMEM((N_KV, HEAD_DIM + 16, W // 2), F32)]


def _attn_b_kernel(q_ref, k_ref, vt_ref, o_ref, qs_ref, sa_ref, sb_ref, m_ref, acc_ref):
    lane = lax.broadcasted_iota(jnp.int32, (1, LANES), 1)
    head_b = (lane // (HEAD_DIM // 2)) % 2
    qs_ref[...] = _stack_maps(q_ref, [[head_b == kv] for kv in range(N_KV)])
    _attn_sweep(qs_ref, k_ref, vt_ref, lambda j: None, sa_ref, sb_ref, m_ref, acc_ref)
    qt = q_ref.shape[1]
    ot = [acc_ref[kv][0:HEAD_DIM] / acc_ref[kv][HEAD_DIM:HEAD_DIM + 1] for kv in range(N_KV)]
    for g in range(2):
        yt = jnp.concatenate([ot[kv][:, g * qt:(g + 1) * qt] for kv in range(N_KV)], axis=0)
        o_ref[0, :, g * LANES:(g + 1) * LANES] = yt.T.astype(BF16)


def _attn_b(qkv, vt):
    bn, L, _ = qkv.shape
    return pl.pallas_call(
        _attn_b_kernel,
        grid=(bn, L // Q_TILE_WIDE),
        in_specs=[
            pl.BlockSpec((1, Q_TILE_WIDE, 2 * LANES), lambda b, i: (b, i, 2)),
            pl.BlockSpec((1, L, LANES), lambda b, i: (b, 0, 6)),
            pl.BlockSpec((1, LANES, L), lambda b, i: (b, 0, 0)),
        ],
        out_specs=pl.BlockSpec((1, Q_TILE_WIDE, MIX_W), lambda b, i: (b, i, 0)),
        out_shape=jax.ShapeDtypeStruct((bn, L, MIX_W), BF16),
        scratch_shapes=_sweep_scratch(4, Q_TILE_WIDE),
        compiler_params=_params(("parallel", "parallel")),
        name="attn_rope",
    )(qkv, qkv, vt)


def _attn_c_kernel(sc_ref, q_ref, k_ref, vt_ref, bias_ref, gsub_ref, hsel_ref, o_ref,
                   qs_ref, sa_ref, sb_ref, m_ref, acc_ref):
    i = pl.program_id(1)
    lam = sc_ref[0]
    lane = lax.broadcasted_iota(jnp.int32, (1, LANES), 1)
    grp = lane // C_DIM
    qs_ref[...] = _stack_maps(q_ref, [[grp == 2 * kv, grp == 2 * kv + 1] for kv in range(N_KV)])

    def bias(j):
        nsub = Q_TILE // BLOCK
        ts = [jnp.clip(j - (i * nsub + sb) + 2, 0, 4) for sb in range(nsub)]
        per_head = [jnp.concatenate([bias_ref[t, h] for t in ts], axis=1) for h in range(N_HEADS)]
        return jnp.concatenate([per_head[h] for h in range(N_HEADS) for _ in range(2)], axis=1)

    _attn_sweep(qs_ref, k_ref, vt_ref, bias, sa_ref, sb_ref, m_ref, acc_ref)
    ot = [acc_ref[kv][0:HEAD_DIM] / acc_ref[kv][HEAD_DIM:HEAD_DIM + 1] for kv in range(N_KV)]
    for g in range(2):
        dt = [ot[kv][:, (2 * g) * Q_TILE:(2 * g + 1) * Q_TILE] - lam * ot[kv][:, (2 * g + 1) * Q_TILE:(2 * g + 2) * Q_TILE]
              for kv in range(N_KV)]
        y = jnp.concatenate(dt, axis=0).T
        ms = _split_dot(y * y, hsel_ref[...]) * (1.0 / HEAD_DIM)
        y = y * lax.rsqrt(ms + RMS_EPS) * gsub_ref[...] * sc_ref[1]
        o_ref[0, :, g * LANES:(g + 1) * LANES] = y.astype(BF16)


def _attn_c(qkv, vt, scal, bias5t, gsub, hsel):
    bn, L, _ = qkv.shape
    return pl.pallas_call(
        _attn_c_kernel,
        grid=(bn, L // Q_TILE),
        in_specs=[
            pl.BlockSpec(memory_space=pltpu.SMEM),
            pl.BlockSpec((1, Q_TILE, 2 * LANES), lambda b, i: (b, i, 4)),
            pl.BlockSpec((1, L, LANES), lambda b, i: (b, 0, 10)),
            pl.BlockSpec((1, LANES, L), lambda b, i: (b, 0, 0)),
            _const_spec((5, N_HEADS, BLOCK, BLOCK)),
            _const_spec((1, LANES)),
            _const_spec((LANES, LANES)),
        ],
        out_specs=pl.BlockSpec((1, Q_TILE, MIX_W), lambda b, i: (b, i, 0)),
        out_shape=jax.ShapeDtypeStruct((bn, L, MIX_W), BF16),
        scratch_shapes=_sweep_scratch(8, Q_TILE),
        compiler_params=_params(("parallel", "parallel")),
        name="attn_diff",
    )(scal, qkv, qkv, vt, bias5t, gsub, hsel)


def _split3_dot(sel, x):
    x1 = x.astype(BF16)
    r1 = x - x1.astype(F32)
    x2 = r1.astype(BF16)
    x3 = (r1 - x2.astype(F32)).astype(BF16)
    return _dot(sel, x1) + _dot(sel, x2) + _dot(sel, x3)


def _hgrn_chunk(forward, q, kk, g, v, tri_ref, bd_ref, st_ref):
    W = MIX_W
    b = _split3_dot(tri_ref[...], g)
    tot = b[BLOCK - 1:BLOCK, :]
    if forward:
        x, e = b, tot - b
    else:
        x, e = tot - b + g, b - g
    sub = lax.broadcasted_iota(jnp.int32, (1, 8, 1), 1)
    x3, q3, k3, v3 = (a.reshape(BLOCK // 8, 8, W) for a in (x, q, kk, v))
    o = _dot((q * kk).astype(BF16), bd_ref[...]) * v
    for d in range(1, 8):
        sh = d if forward else 8 - d
        ok = (sub >= d) if forward else (sub + d < 8)
        e_d = jnp.exp2(jnp.where(ok, x3 - pltpu.roll(x3, sh, 1), NEG)) * q3 * pltpu.roll(k3, sh, 1)
        o = o + _dot(e_d.reshape(BLOCK, W).astype(BF16), bd_ref[...]) * pltpu.roll(v3, sh, 1).reshape(BLOCK, W)
    row = lax.broadcasted_iota(jnp.int32, (BLOCK, 1), 0)
    qcol = lax.broadcasted_iota(jnp.int32, (1, N_HEADS * BLOCK), 1) % BLOCK
    lane_head = lax.broadcasted_iota(jnp.int32, (1, W), 1) // HEAD_DIM
    at = jnp.zeros((BLOCK, N_HEADS * BLOCK), F32)
    for m in (8, 16, 32, 64):
        mid = m - 1 if forward else m
        xm = jnp.concatenate([jnp.broadcast_to(x[i + mid:i + mid + 1, :], (2 * m, W))
                              for i in range(0, BLOCK, 2 * m)], axis=0)
        second = (row // m) % 2 == 1
        qside = second if forward else jnp.logical_not(second)
        qt = (q * jnp.exp2(jnp.where(qside, x - xm, NEG))).astype(BF16)
        kt = (kk * jnp.exp2(jnp.where(qside, NEG, xm - x))).astype(BF16)
        qstack = jnp.concatenate([jnp.where(lane_head == h, qt, jnp.zeros_like(qt)) for h in range(N_HEADS)], axis=0)
        a = _dot_nt(kt, qstack)
        at = at + jnp.where(row // (2 * m) == qcol // (2 * m), a, 0.0)
    vt = v.T.astype(BF16)
    ot = _dot(vt, at.astype(BF16))
    o = o + jnp.concatenate([ot[h * HEAD_DIM:(h + 1) * HEAD_DIM, h * BLOCK:(h + 1) * BLOCK]
                             for h in range(N_HEADS)], axis=0).T
    qd = (q * jnp.exp2(x)).astype(BF16)
    kd = (kk * jnp.exp2(e)).astype(BF16)
    gam = jnp.exp2(tot)
    hmask = (lax.broadcasted_iota(jnp.int32, (LANES, LANES), 0) // HEAD_DIM
             == lax.broadcasted_iota(jnp.int32, (LANES, LANES), 1) // HEAD_DIM)
    inter = []
    for hf in range(2):
        lanes = slice(hf * LANES, (hf + 1) * LANES)
        st = st_ref[hf]
        inter.append(_dot_nt(qd[:, lanes], st.astype(BF16)))
        st_ref[hf] = st * gam[:, lanes] + jnp.where(hmask, _dot(vt[lanes, :], kd[:, lanes]), 0.0)
    return o + jnp.concatenate(inter, axis=1)


def _hgrn_direction(forward, first_tile, q_ref, z_ref, v_ref, lb_ref, tri_ref, bd_ref, o_ref, st_ref):
    R = q_ref.shape[1]
    z = z_ref[0]
    q = q_ref[0] * (HEAD_DIM ** -0.5)
    v = v_ref[0]
    lb = lb_ref[...]
    row = lax.broadcasted_iota(jnp.int32, (R, 1), 0)
    sig = jax.nn.sigmoid(z)
    f = lb + (1.0 - lb) * sig
    g = jnp.log(jnp.maximum(f, F_FLOOR)) * LOG2E
    kk = (1.0 - lb) * (1.0 - sig)
    kk = jnp.where(row < jnp.where(first_tile, FRONT, 0), 0.0, kk)
    nblk = R // BLOCK
    for n in range(nblk):
        r0 = (n if forward else nblk - 1 - n) * BLOCK
        rows = slice(r0, r0 + BLOCK)
        o_ref[0, rows, :] = _hgrn_chunk(forward, q[rows], kk[rows], g[rows], v[rows], tri_ref, bd_ref, st_ref)


def _hgrn_kernel(qf_ref, zf_ref, vf_ref, qb_ref, zb_ref, vb_ref, lb_ref, tri_ref, bd_ref, of_ref, ob_ref, st_ref):
    j = pl.program_id(1)
    nt = pl.num_programs(1)

    @pl.when(j == 0)
    def _():
        st_ref[...] = jnp.zeros_like(st_ref)

    _hgrn_direction(True, j == 0, qf_ref, zf_ref, vf_ref, lb_ref.at[0], tri_ref, bd_ref, of_ref, st_ref.at[0])
    _hgrn_direction(False, j == nt - 1, qb_ref, zb_ref, vb_ref, lb_ref.at[1], tri_ref, bd_ref, ob_ref, st_ref.at[1])


def _hgrn(dproj, lb2, tri, bd):
    bn, L, _ = dproj.shape
    nt = L // ROW_TILE
    fw = lambda c: pl.BlockSpec((1, ROW_TILE, MIX_W), lambda b, j: (b, j, c))
    bw = lambda c: pl.BlockSpec((1, ROW_TILE, MIX_W), lambda b, j: (b, nt - 1 - j, c))
    return pl.pallas_call(
        _hgrn_kernel,
        grid=(bn, nt),
        in_specs=[fw(0), fw(1), fw(3), bw(0), bw(2), bw(3), _const_spec((2, 1, MIX_W)), _const_spec((BLOCK, BLOCK)),
                  _const_spec((MIX_W, MIX_W))],
        out_specs=[pl.BlockSpec((1, ROW_TILE, MIX_W), lambda b, j: (b, j, 0)),
                   pl.BlockSpec((1, ROW_TILE, MIX_W), lambda b, j: (b, nt - 1 - j, 0))],
        out_shape=[jax.ShapeDtypeStruct((bn, L, MIX_W), F32)] * 2,
        scratch_shapes=[pltpu.VMEM((2, 2, LANES, LANES), F32)],
        compiler_params=_params(("parallel", "arbitrary")),
        name="hgrn",
    )(dproj, dproj, dproj, dproj, dproj, dproj, lb2, tri, bd)


def _merge_kernel(h_ref, ya_ref, yb_ref, yc_ref, of_ref, ob_ref, dg_ref, g_ref, gout_ref, bd_ref,
                  wgz_ref, wb_ref, wo_ref, o_ref):
    h = h_ref[0]
    u = _rms(h, g_ref[...]).astype(BF16)
    od = of_ref[0] + ob_ref[0]
    ms = _split_dot(od * od, bd_ref[...]) * (1.0 / HEAD_DIM)
    yd = (od * lax.rsqrt(ms + RMS_EPS) * gout_ref[...] * jax.nn.silu(dg_ref[0])).astype(BF16)
    ys = (ya_ref[0], yb_ref[0], yc_ref[0], yd)
    merged = None
    for n in range(N_BRANCH):
        gate = jax.nn.sigmoid(_dot(u, wgz_ref[:, n * D_MODEL:(n + 1) * D_MODEL]))
        term = gate * _dot(ys[n], wb_ref[n])
        merged = term if merged is None else merged + term
    o_ref[0] = h + _dot(merged.astype(BF16), wo_ref[...])


def _merge(h, ya, yb, yc, of, ob, dproj, g, gout, bd, wgz, wb, wo):
    bn, L, _ = h.shape
    rows = bn * L
    tm = _dense_tile(rows)
    flat = lambda a: a.reshape(1, rows, a.shape[-1])
    h, ya, yb, yc, of, ob, dproj = (flat(a) for a in (h, ya, yb, yc, of, ob, dproj))
    row = lambda w, c=0: pl.BlockSpec((1, tm, w), lambda i: (0, i, c))
    out = pl.pallas_call(
        _merge_kernel,
        grid=(rows // tm,),
        in_specs=[row(D_MODEL), row(MIX_W), row(MIX_W), row(MIX_W), row(MIX_W), row(MIX_W), row(MIX_W, 4),
                  _const_spec((1, D_MODEL)), _const_spec((1, MIX_W)), _const_spec((MIX_W, MIX_W)),
                  _const_spec((D_MODEL, N_BRANCH * D_MODEL)), _const_spec((N_BRANCH, MIX_W, D_MODEL)),
                  _const_spec((D_MODEL, D_MODEL))],
        out_specs=row(D_MODEL),
        out_shape=jax.ShapeDtypeStruct((1, rows, D_MODEL), F32),
        compiler_params=_params(("parallel",)),
        name="merge",
    )(h, ya, yb, yc, of, ob, dproj, g, gout, bd, wgz, wb, wo)
    return out.reshape(bn, L, D_MODEL)


def _ffn_kernel(h_ref, g_ref, wg_ref, wu_ref, wd_ref, gfin_ref, o_ref, *, final):
    h = h_ref[0]
    u = _rms(h, g_ref[...]).astype(BF16)
    half = D_FF // 2
    out = h
    for c in range(2):
        a = _dot(u, wg_ref[:, c * half:(c + 1) * half])
        t = (jax.nn.silu(a) * _dot(u, wu_ref[:, c * half:(c + 1) * half])).astype(BF16)
        out = out + _dot(t, wd_ref[c * half:(c + 1) * half, :])
    o_ref[0] = _rms(out, gfin_ref[...]) if final else out


def _ffn(h, g, wg, wu, wd, gfin, final):
    bn, L, _ = h.shape
    weights = [_const_spec((1, D_MODEL)), _const_spec((D_MODEL, D_FF)), _const_spec((D_MODEL, D_FF)),
               _const_spec((D_FF, D_MODEL)), _const_spec((1, D_MODEL))]
    kern = functools.partial(_ffn_kernel, final=final)
    if final:
        S = L - META_END
        tm = _dense_tile(S)
        return pl.pallas_call(
            kern,
            grid=(bn, S // tm),
            in_specs=[pl.BlockSpec((pl.Element(1), pl.Element(tm), pl.Element(D_MODEL)),
                                   lambda b, i: (b, pl.multiple_of(META_END + i * tm, BLOCK), 0))] + weights,
            out_specs=pl.BlockSpec((1, tm, D_MODEL), lambda b, i: (b, i, 0)),
            out_shape=jax.ShapeDtypeStruct((bn, S, D_MODEL), F32),
            compiler_params=_params(("parallel", "parallel")),
            name="ffn_out",
        )(h, g, wg, wu, wd, gfin)
    rows = bn * L
    tm = _dense_tile(rows)
    row = pl.BlockSpec((1, tm, D_MODEL), lambda i: (0, i, 0))
    out = pl.pallas_call(
        kern,
        grid=(rows // tm,),
        in_specs=[row] + weights,
        out_specs=row,
        out_shape=jax.ShapeDtypeStruct((1, rows, D_MODEL), F32),
        compiler_params=_params(("parallel",)),
        name="ffn",
    )(h.reshape(1, rows, D_MODEL), g, wg, wu, wd, gfin)
    return out.reshape(bn, L, D_MODEL)


def _t5_bucket(rel):
    half = N_BUCKETS // 2
    exact = half // 2
    n = jnp.abs(rel)
    nf = jnp.maximum(n, exact).astype(F32)
    big = exact + (jnp.log(nf / exact) / math.log(MAX_DIST / exact) * (half - exact)).astype(jnp.int32)
    big = jnp.minimum(big, half - 1)
    return jnp.where(rel > 0, half, 0) + jnp.where(n < exact, n, big)


def _lookup(table, bucket):
    bucket = jnp.bitwise_and(bucket, N_BUCKETS - 1)
    hit = bucket[None, ..., None] == jnp.arange(N_BUCKETS)
    return jnp.sum(jnp.where(hit, table.T.reshape((N_HEADS,) + (1,) * bucket.ndim + (N_BUCKETS,)), 0.0), axis=-1)


def _bias_tables(rel_bias):
    bias_a = rel_bias[:, :N_HEADS].astype(F32) * LOG2E
    bias_c = rel_bias[:, N_HEADS:].astype(F32) * LOG2E
    r = jnp.arange(BLOCK)[:, None]
    rel_band = (jnp.arange(3 * BLOCK)[None, :] - BLOCK) - r
    bband = jnp.where(jnp.abs(rel_band)[None] <= WINDOW, _lookup(bias_a, _t5_bucket(rel_band)), NEG)
    lane = jnp.arange(BLOCK)[None, :]
    rel_m = lane[None] - (jnp.arange(3)[:, None, None] * BLOCK + r[None])
    bmeta = jnp.where((lane >= FRONT)[None, None], jnp.moveaxis(_lookup(bias_a, _t5_bucket(rel_m)), 0, 1), NEG)
    rel5t = (jnp.arange(5)[:, None, None] - 2) * BLOCK + r[None] - lane[None]
    bias5t = jnp.moveaxis(_lookup(bias_c, _t5_bucket(rel5t)), 0, 1)
    return bband, bmeta, bias5t


def _rope_layout():
    return np.concatenate([np.zeros(32, int), np.ones(32, int)] * 2)


def _rope_cols(w, heads):
    lead = w.shape[:-1]
    w = w.reshape(lead + (heads // 2, 2, 2, 2, HEAD_DIM // 4))
    if heads == N_HEADS:
        w = jnp.moveaxis(w, (-5, -4, -3, -2), (-3, -5, -2, -4))
    else:
        w = jnp.moveaxis(w, (-4, -3, -2), (-3, -2, -4))
    return w.reshape(lead + (heads * HEAD_DIM,))


def _pair_cols(w):
    lead = w.shape[:-1]
    return jnp.swapaxes(w.reshape(lead + (N_KV, 2, HEAD_DIM)), -3, -2).reshape(lead + (N_HEADS * HEAD_DIM,))


def _rope_tables(L, S):
    rows = S // GRID_W
    row = jnp.concatenate([jnp.zeros((FRONT,), jnp.int32), -jnp.ones((N_META,), jnp.int32),
                           jnp.repeat(jnp.arange(rows, dtype=jnp.int32), GRID_W)]).astype(F32)
    col = jnp.concatenate([jnp.zeros((FRONT,), jnp.int32), jnp.arange(N_META, dtype=jnp.int32),
                           jnp.tile(jnp.arange(GRID_W, dtype=jnp.int32), rows)]).astype(F32)
    half = HEAD_DIM // 2
    inv = ROPE_THETA ** (-jnp.arange(0, half, 2, dtype=F32) / half)
    ang = jnp.concatenate([row[:, None] * inv[None, :], col[:, None] * inv[None, :]], axis=-1)
    cos_t = jnp.tile(jnp.cos(ang), (1, 4))
    sin_t = jnp.concatenate([-jnp.tile(jnp.sin(ang), (1, 2)), jnp.tile(jnp.sin(ang), (1, 2))], axis=-1)
    return cos_t, sin_t


def _rope_gain(g):
    g = jnp.swapaxes(g.astype(F32).reshape(2, 2, HEAD_DIM // 4), 0, 1).reshape(2, 1, HEAD_DIM // 2)
    return jnp.broadcast_to(g, (2, 2, HEAD_DIM // 2)).reshape(LANES)


def _qkv_weight(wl):
    a, b, c = wl[:, 0:512], wl[:, 512:1024], wl[:, 1024:1536]
    return jnp.concatenate([_pair_cols(a[:, :256]), a[:, 256:],
                            _rope_cols(b[:, :256], N_HEADS), _rope_cols(b[:, 256:384], N_KV), b[:, 384:],
                            _pair_cols(c[:, :256]), c[:, 256:]], axis=1)


def kernel(x, meta_tokens, rel_bias, hgrn_lb_logits, ln_mix, w_in, attn_sink, qk_norm_q, qk_norm_k, diff_lambda,
           diff_subnorm, hgrn_out_norm, w_branch, w_out, ln_ffn, w_ffn_gate, w_ffn_up, w_ffn_down, ln_final):
    bn, S, _ = x.shape
    L = META_END + S
    depth = w_in.shape[0]
    assert L % ROW_TILE == 0 and L % Q_TILE == 0 and L % Q_TILE_WIDE == 0 and S % GRID_W == 0
    padmeta = jnp.concatenate([jnp.zeros((FRONT, D_MODEL), x.dtype), meta_tokens.astype(x.dtype)], axis=0)
    h = x
    bband, bmeta, bias5t = _bias_tables(rel_bias)
    cos_t, sin_t = _rope_tables(L, S)
    rhead = _rope_layout()
    lane = np.arange(LANES)
    hsel_rope = jnp.asarray(rhead[:, None] == rhead[None, :], BF16)
    hsel_half = jnp.asarray((lane[:, None] // HEAD_DIM) == (lane[None, :] // HEAD_DIM), BF16)
    w256 = np.arange(MIX_W)
    bd = jnp.asarray((w256[:, None] // HEAD_DIM) == (w256[None, :] // HEAD_DIM), BF16)
    tri = jnp.asarray(np.tril(np.ones((BLOCK, BLOCK))), BF16)
    cs = np.ones((1, QKV_W), np.float32)
    cs[0, 0:256] = HEAD_DIM ** -0.5 * LOG2E
    cs[0, 1024:1280] = C_DIM ** -0.5 * LOG2E
    cs = jnp.asarray(cs)
    lb_p = jax.nn.softmax(hgrn_lb_logits.astype(F32), axis=1)
    lb_all = jnp.cumsum(lb_p, axis=1) - lb_p[:, :1]

    for l in range(depth):
        wl = w_in[l]
        wqkv = _qkv_weight(wl).astype(BF16)
        wd = wl[:, QKV_W:QKV_W + D_W].astype(BF16)
        wgz = wl[:, QKV_W + D_W:].astype(BF16)
        gn = jnp.stack([_rope_gain(qk_norm_q[l]) * (HEAD_DIM ** -0.5 * LOG2E)] * 2 + [_rope_gain(qk_norm_k[l])])
        if l == 0:
            qkv, dproj, vtb, vtc, h = _in_proj(x, ln_mix[l][None], wqkv, wd, cs, cos_t, sin_t, gn, hsel_rope, padmeta)
        else:
            qkv, dproj, vtb, vtc = _in_proj(h, ln_mix[l][None], wqkv, wd, cs, cos_t, sin_t, gn, hsel_rope)
        ya = _attn_a(qkv, attn_sink[l].astype(F32) * LOG2E, bband, bmeta)
        yb = _attn_b(qkv, vtb)
        lam_init = 0.8 - 0.6 * math.exp(-0.3 * l)
        lam_p = diff_lambda[l].astype(F32)
        lam = jnp.exp(jnp.sum(lam_p[0] * lam_p[1])) - jnp.exp(jnp.sum(lam_p[2] * lam_p[3])) + lam_init
        scal = jnp.stack([lam, jnp.asarray(1.0 - lam_init, F32)])
        gsub = jnp.tile(diff_subnorm[l].astype(F32), 2)[None]
        yc = _attn_c(qkv, vtc, scal, bias5t, gsub, hsel_half)
        lb2 = lb_all[:, l][:, None, :]
        of, ob = _hgrn(dproj, lb2, tri, bd)
        wb = w_branch[l].astype(BF16)
        wb_att = jnp.swapaxes(wb[:3].reshape(3, N_KV, 2, HEAD_DIM, D_MODEL), 1, 2).reshape(3, MIX_W, D_MODEL)
        wb = jnp.concatenate([wb_att, wb[3:]], axis=0)
        gout = jnp.tile(hgrn_out_norm[l].astype(F32), N_HEADS)[None]
        h = _merge(h, ya, yb, yc, of, ob, dproj, ln_mix[l][None], gout, bd, wgz, wb, w_out[l].astype(BF16))
        h = _ffn(h, ln_ffn[l][None], w_ffn_gate[l].astype(BF16), w_ffn_up[l].astype(BF16), w_ffn_down[l].astype(BF16),
                 ln_final[None], final=l == depth - 1)
    return h
```
